```python
import jax, jax.numpy as jnp
from jax import lax
import numpy as np

D_MODEL = 1024
BATCH = 8
SEQ = 4096
DEPTH = 1
DEC_BATCH = 128
DEC_SEQ = 4
PAST_LEN = 8192
PAGE_SIZE = 128

MIX_WIDTH = D_MODEL
RET_HEADS = 4
RET_DK = 128
RET_DV = 128
RET_W = RET_HEADS * RET_DV
RET_CHUNK = 128
ATT_HEADS = 4
ATT_HD = 128
ATT_W = ATT_HEADS * ATT_HD
ATT_QBLK = 64
IDX_HEADS = 8
IDX_DIM = 64
IDX_TOPK_MAX = 256
MEM_TOKENS = 256
MEM_HEADS = 4
MEM_HD = D_MODEL // MEM_HEADS
PEER_HEADS = 8
PEER_NKEYS = 128
PEER_EXPERTS = PEER_NKEYS * PEER_NKEYS
PEER_QDIM = 256
PEER_HALF = PEER_QDIM // 2
PEER_TOPK = 16
PEER_BLK = 128
ROPE_THETA = 10000.0
LN_EPS = 1e-5
GN_EPS = 1e-5
DN_ALPHA = (2 * DEPTH) ** 0.25
DN_BETA = (8 * DEPTH) ** -0.25

_IN_SIZES = (RET_HEADS * RET_DK, RET_HEADS * RET_DK, RET_W, RET_W,
             ATT_W, ATT_W, ATT_W, IDX_HEADS * IDX_DIM, IDX_DIM, IDX_HEADS)
IN_COLS = sum(_IN_SIZES)
IN_SPLITS = tuple(sum(_IN_SIZES[:i + 1]) for i in range(len(_IN_SIZES) - 1))
_VALUE_GROUPS = (2, 6)

kernel_name = "hymba_retention_dsa_peer_step"

F32 = jnp.float32


def layer_norm(x, g, b):
    xf = x.astype(F32)
    mu = jnp.mean(xf, axis=-1, keepdims=True)
    var = jnp.mean(jnp.square(xf - mu), axis=-1, keepdims=True)
    y = (xf - mu) * lax.rsqrt(var + LN_EPS)
    return (y * g.astype(F32) + b.astype(F32)).astype(x.dtype)


def rope(x, pos):
    d = x.shape[-1]
    half = d // 2
    inv = ROPE_THETA ** (-jnp.arange(half, dtype=F32) * (2.0 / d))
    ang = pos.astype(F32)[:, None] * inv[None, :]
    cos = jnp.cos(ang)[:, None, :]
    sin = jnp.sin(ang)[:, None, :]
    xf = x.astype(F32)
    x1, x2 = xf[..., :half], xf[..., half:]
    return jnp.concatenate([x1 * cos - x2 * sin, x1 * sin + x2 * cos], axis=-1).astype(x.dtype)


def retention_log_decay():
    return jnp.log(1.0 - 2.0 ** (-5.0 - jnp.arange(RET_HEADS, dtype=F32)))


def mixer_inputs(x, pos, w_in):
    B, T, _ = x.shape
    proj = x @ w_in
    rq, rk, rv, rg, aq, ak, av, iq, ik, iw = jnp.split(proj, IN_SPLITS, axis=-1)
    rq = rope(rq.reshape(B, T, RET_HEADS, RET_DK), pos)
    rk = rope(rk.reshape(B, T, RET_HEADS, RET_DK), pos) * (RET_DK ** -0.5)
    rv = rv.reshape(B, T, RET_HEADS, RET_DV)
    aq = rope(aq.reshape(B, T, ATT_HEADS, ATT_HD), pos)
    ak = rope(ak.reshape(B, T, ATT_HEADS, ATT_HD), pos)
    av = av.reshape(B, T, ATT_HEADS, ATT_HD)
    iq = rope(iq.reshape(B, T, IDX_HEADS, IDX_DIM), pos)
    ik = rope(ik[:, :, None, :], pos)[:, :, 0, :]
    return rq, rk, rv, rg, aq, ak, av, iq, ik, iw


def retention_chunk(state, q, k, v, log_g):
    C = q.shape[1]
    qf, kf, vf = q.astype(F32), k.astype(F32), v.astype(F32)
    sf = state.astype(F32)
    j = jnp.arange(C, dtype=F32)
    diff = j[:, None] - j[None, :]
    dmat = jnp.where(diff >= 0, jnp.exp(log_g[:, None, None] * jnp.maximum(diff, 0.0)), 0.0)
    scores = jnp.einsum('bihd,bjhd->bhij', qf, kf) * dmat[None]
    o = jnp.einsum('bhij,bjhv->bihv', scores, vf)
    q_dec = jnp.exp(log_g[None, :] * (j[:, None] + 1.0))
    o = o + jnp.einsum('bihd,bhdv->bihv', qf, sf) * q_dec[None, :, :, None]
    k_dec = jnp.exp(log_g[None, :] * (C - 1.0 - j[:, None]))
    new = sf * jnp.exp(log_g * C)[None, :, None, None] + jnp.einsum('bjhd,jh,bjhv->bhdv', kf, k_dec, vf)
    return new.astype(state.dtype), o.astype(q.dtype)


def retention_prompt(q, k, v, log_g):
    B, T = q.shape[:2]
    nc = T // RET_CHUNK

    def to_chunks(a):
        return a.reshape(B, nc, RET_CHUNK, *a.shape[2:]).swapaxes(0, 1)

    def step(s, inp):
        return retention_chunk(s, inp[0], inp[1], inp[2], log_g)

    s0 = jnp.zeros((B, RET_HEADS, RET_DK, RET_DV), q.dtype)
    s_final, o = lax.scan(step, s0, (to_chunks(q), to_chunks(k), to_chunks(v)))
    return s_final, o.swapaxes(0, 1).reshape(B, T, RET_HEADS, RET_DV)


def index_scores(iq, iw, ik):
    r = jax.nn.relu(jnp.einsum('bqhd,bsd->bqhs', iq, ik).astype(F32))
    return jnp.einsum('bqhs,bqh->bqs', r, iw.astype(F32))


def gather_rows(a, idx):
    return jax.vmap(lambda aa, ii: aa[ii])(a, idx)


def sparse_attend(q, kg, vg, valid):
    s = jnp.einsum('bqhd,bqkhd->bqhk', q, kg).astype(F32) * (ATT_HD ** -0.5)
    s = jnp.where(valid[:, :, None, :], s, -jnp.inf)
    p = jax.nn.softmax(s, axis=-1)
    return jnp.einsum('bqhk,bqkhd->bqhd', p.astype(vg.dtype), vg)


def dsa_prompt(q, k, v, iq, ik, iw, topk):
    B, T = q.shape[:2]
    nb = T // ATT_QBLK
    kpos = jnp.arange(T)

    def block(i):
        start = i * ATT_QBLK
        qb = lax.dynamic_slice_in_dim(q, start, ATT_QBLK, axis=1)
        iqb = lax.dynamic_slice_in_dim(iq, start, ATT_QBLK, axis=1)
        iwb = lax.dynamic_slice_in_dim(iw, start, ATT_QBLK, axis=1)
        qpos = start + jnp.arange(ATT_QBLK)
        s = index_scores(iqb, iwb, ik)
        s = jnp.where((kpos[None, :] <= qpos[:, None])[None], s, -jnp.inf)
        _, sel = lax.top_k(s, topk)
        valid = sel <= qpos[None, :, None]
        return sparse_attend(qb, gather_rows(k, sel), gather_rows(v, sel), valid)

    out = lax.map(block, jnp.arange(nb))
    return out.swapaxes(0, 1).reshape(B, T, ATT_HEADS, ATT_HD)


def dsa_sample(q, k_new, v_new, iq, ik_new, iw, cache_k, cache_v, cache_idx_k, page_table, topk):
    DB, DS = q.shape[:2]
    n_pages = page_table.shape[1]
    L = PAST_LEN + DS
    ik_past = cache_idx_k[page_table].reshape(DB, n_pages * PAGE_SIZE, IDX_DIM)
    ik_all = jnp.concatenate([ik_past, ik_new.astype(ik_past.dtype)], axis=1)
    s = index_scores(iq, iw, ik_all)
    qpos = PAST_LEN + jnp.arange(DS)
    kpos = jnp.arange(L)
    s = jnp.where((kpos[None, :] <= qpos[:, None])[None], s, -jnp.inf)
    _, sel = lax.top_k(s, topk)
    valid = sel <= qpos[None, :, None]
    past_idx = jnp.minimum(sel, PAST_LEN - 1)
    phys = gather_rows(page_table, past_idx // PAGE_SIZE)
    off = past_idx % PAGE_SIZE
    is_new = (sel >= PAST_LEN)[..., None, None]
    new_idx = jnp.clip(sel - PAST_LEN, 0, DS - 1)
    kg = jnp.where(is_new, gather_rows(k_new, new_idx).astype(cache_k.dtype), cache_k[phys, off])
    vg = jnp.where(is_new, gather_rows(v_new, new_idx).astype(cache_v.dtype), cache_v[phys, off])
    return sparse_attend(q, kg, vg, valid)


def merge_mixers(ret_o, rg, att_o, ret_gn_g, ret_gn_b, w_out):
    B, T = ret_o.shape[:2]
    of = ret_o.astype(F32)
    mu = jnp.mean(of, axis=-1, keepdims=True)
    var = jnp.mean(jnp.square(of - mu), axis=-1, keepdims=True)
    n = ((of - mu) * lax.rsqrt(var + GN_EPS)).reshape(B, T, RET_W)
    n = n * ret_gn_g.astype(F32) + ret_gn_b.astype(F32)
    ret_y = (jax.nn.silu(rg.astype(F32)) * n).astype(rg.dtype)
    cat = jnp.concatenate([ret_y, att_o.reshape(B, T, ATT_W).astype(rg.dtype)], axis=-1)
    return cat @ w_out


def memory_attend(h, w_cq, mem_k, mem_v):
    B, T, _ = h.shape
    q = (h @ w_cq).reshape(B, T, MEM_HEADS, MEM_HD)
    s = jnp.einsum('bthd,bmhd->bhtm', q, mem_k).astype(F32) * (MEM_HD ** -0.5)
    p = jax.nn.softmax(s, axis=-1)
    return jnp.einsum('bhtm,bmhd->bthd', p.astype(mem_v.dtype), mem_v).reshape(B, T, D_MODEL)


def peer_ffn(x, w_pq, peer_keys_a, peer_keys_b, peer_u, peer_v):
    shp = x.shape
    xt = x.reshape(-1, shp[-1])
    n = xt.shape[0]
    nb = -(-n // PEER_BLK)
    xt = jnp.pad(xt, ((0, nb * PEER_BLK - n), (0, 0))).reshape(nb, PEER_BLK, shp[-1])

    def block(xb):
        q = (xb @ w_pq).reshape(PEER_BLK, PEER_HEADS, PEER_QDIM)
        s1 = jnp.einsum('nhd,kd->nhk', q[..., :PEER_HALF], peer_keys_a).astype(F32)
        s2 = jnp.einsum('nhd,kd->nhk', q[..., PEER_HALF:], peer_keys_b).astype(F32)
        v1, i1 = lax.top_k(s1, PEER_TOPK)
        v2, i2 = lax.top_k(s2, PEER_TOPK)
        cand = (v1[..., :, None] + v2[..., None, :]).reshape(PEER_BLK, PEER_HEADS, PEER_TOPK * PEER_TOPK)
        cid = (i1[..., :, None] * PEER_NKEYS + i2[..., None, :]).reshape(PEER_BLK, PEER_HEADS, PEER_TOPK * PEER_TOPK)
        sc, ci = lax.top_k(cand, PEER_TOPK)
        eidx = jnp.take_along_axis(cid, ci, axis=-1)
        g = jax.nn.softmax(sc, axis=-1)
        hid = jax.nn.gelu(jnp.einsum('nhkd,nd->nhk', peer_u[eidx], xb).astype(F32), approximate=False)
        return jnp.einsum('nhk,nhkd->nd', (g * hid).astype(peer_v.dtype), peer_v[eidx])

    y = lax.map(block, xt).reshape(nb * PEER_BLK, -1)[:n]
    return y.reshape(shp).astype(x.dtype)


def post_tail(x, mix, mem_k, mem_v, ln1_g, ln1_b, w_cq, w_co, ln2_g, ln2_b,
              w_pq, peer_keys_a, peer_keys_b, peer_u, peer_v, ln3_g, ln3_b):
    h = layer_norm(DN_ALPHA * x + mix, ln1_g, ln1_b)
    h = layer_norm(DN_ALPHA * h + memory_attend(h, w_cq, mem_k, mem_v) @ w_co, ln2_g, ln2_b)
    h = layer_norm(DN_ALPHA * h + peer_ffn(h, w_pq, peer_keys_a, peer_keys_b, peer_u, peer_v), ln3_g, ln3_b)
    return h


def setup_inputs(seed: int = 0) -> dict:
    key = jax.random.key(seed)
    ks = jax.random.split(key, 40)
    n_pages = PAST_LEN // PAGE_SIZE
    n_used = DEC_BATCH * n_pages
    n_pool = n_used + n_used // 4

    def nrm(k, shape, s):
        return jax.random.normal(k, shape, F32) * s

    col_scale = jnp.concatenate([jnp.full((sz,), DN_BETA if i in _VALUE_GROUPS else 1.0, F32)
                                 for i, sz in enumerate(_IN_SIZES)])
    return {
        "x_prompt": nrm(ks[0], (BATCH, SEQ, D_MODEL), 1.0),
        "x_sample": nrm(ks[1], (DEC_BATCH, DEC_SEQ, D_MODEL), 1.0),
        "mem_prompt": nrm(ks[2], (BATCH, MEM_TOKENS, D_MODEL), 1.0),
        "state_ret": nrm(ks[3], (DEC_BATCH, RET_HEADS, RET_DK, RET_DV), 0.1),
        "cache_k": nrm(ks[4], (n_pool, PAGE_SIZE, ATT_HEADS, ATT_HD), 1.0),
        "cache_v": nrm(ks[5], (n_pool, PAGE_SIZE, ATT_HEADS, ATT_HD), DN_BETA),
        "cache_idx_k": nrm(ks[6], (n_pool, PAGE_SIZE, IDX_DIM), 1.0),
        "cache_mem_k": nrm(ks[7], (DEC_BATCH, MEM_TOKENS, MEM_HEADS, MEM_HD), 1.0),
        "cache_mem_v": nrm(ks[8], (DEC_BATCH, MEM_TOKENS, MEM_HEADS, MEM_HD), DN_BETA),
        "page_table": jax.random.permutation(ks[9], n_pool)[:n_used].reshape(DEC_BATCH, n_pages).astype(jnp.int32),
        "w_in": nrm(ks[10], (D_MODEL, IN_COLS), D_MODEL ** -0.5) * col_scale[None, :],
        "ret_gn_g": 1.0 + nrm(ks[11], (RET_W,), 0.02),
        "ret_gn_b": nrm(ks[12], (RET_W,), 0.02),
        "w_out": nrm(ks[13], (MIX_WIDTH, D_MODEL), (MIX_WIDTH ** -0.5) * DN_BETA),
        "ln1_g": 1.0 + nrm(ks[14], (D_MODEL,), 0.02),
        "ln1_b": nrm(ks[15], (D_MODEL,), 0.02),
        "w_cq": nrm(ks[16], (D_MODEL, D_MODEL), D_MODEL ** -0.5),
        "w_ck": nrm(ks[17], (D_MODEL, D_MODEL), D_MODEL ** -0.5),
        "w_cv": nrm(ks[18], (D_MODEL, D_MODEL), (D_MODEL ** -0.5) * DN_BETA),
        "w_co": nrm(ks[19], (D_MODEL, D_MODEL), (D_MODEL ** -0.5) * DN_BETA),
        "ln2_g": 1.0 + nrm(ks[20], (D_MODEL,), 0.02),
        "ln2_b": nrm(ks[21], (D_MODEL,), 0.02),
        "w_pq": nrm(ks[22], (D_MODEL, PEER_HEADS * PEER_QDIM), D_MODEL ** -0.5),
        "peer_keys_a": nrm(ks[23], (PEER_NKEYS, PEER_HALF), PEER_HALF ** -0.5),
        "peer_keys_b": nrm(ks[24], (PEER_NKEYS, PEER_HALF), PEER_HALF ** -0.5),
        "peer_u": nrm(ks[25], (PEER_EXPERTS, D_MODEL), D_MODEL ** -0.5),
        "peer_v": nrm(ks[26], (PEER_EXPERTS, D_MODEL), ((PEER_HEADS * PEER_TOPK) ** -0.5) * DN_BETA),
        "ln3_g": 1.0 + nrm(ks[27], (D_MODEL,), 0.02),
        "ln3_b": nrm(ks[28], (D_MODEL,), 0.02),
    }


def reference(x_prompt, x_sample, mem_prompt, state_ret, cache_k, cache_v, cache_idx_k,
              cache_mem_k, cache_mem_v, page_table, w_in, ret_gn_g, ret_gn_b, w_out,
              ln1_g, ln1_b, w_cq, w_ck, w_cv, w_co, ln2_g, ln2_b, w_pq, peer_keys_a,
              peer_keys_b, peer_u, peer_v, ln3_g, ln3_b):
    log_g = retention_log_decay()

    B, T, _ = x_prompt.shape
    pos_p = jnp.arange(T, dtype=jnp.int32)
    for _layer in range(DEPTH):
        rq, rk, rv, rg, aq, ak, av, iq, ik, iw = mixer_inputs(x_prompt, pos_p, w_in)
        p_state_ret, ret_o = retention_prompt(rq, rk, rv, log_g)
        att_o = dsa_prompt(aq, ak, av, iq, ik, iw, min(IDX_TOPK_MAX, T // 4))
        mix_p = merge_mixers(ret_o, rg, att_o, ret_gn_g, ret_gn_b, w_out)
        p_mem_k = (mem_prompt @ w_ck).reshape(B, MEM_TOKENS, MEM_HEADS, MEM_HD)
        p_mem_v = (mem_prompt @ w_cv).reshape(B, MEM_TOKENS, MEM_HEADS, MEM_HD)
        y_prompt = post_tail(x_prompt, mix_p, p_mem_k, p_mem_v, ln1_g, ln1_b, w_cq, w_co, ln2_g, ln2_b,
                             w_pq, peer_keys_a, peer_keys_b, peer_u, peer_v, ln3_g, ln3_b)
        p_k, p_v, p_idx_k = ak, av, ik

        Ts = x_sample.shape[1]
        pos_s = PAST_LEN + jnp.arange(Ts, dtype=jnp.int32)
        sq, sk, sv, sg, saq, sak, sav, siq, sik, siw = mixer_inputs(x_sample, pos_s, w_in)
        s_state_ret, ret_o_s = retention_chunk(state_ret, sq, sk, sv, log_g)
        att_o_s = dsa_sample(saq, sak, sav, siq, sik, siw, cache_k, cache_v, cache_idx_k, page_table,
                             min(IDX_TOPK_MAX, (PAST_LEN + Ts) // 4))
        mix_s = merge_mixers(ret_o_s, sg, att_o_s, ret_gn_g, ret_gn_b, w_out)
        y_sample = post_tail(x_sample, mix_s, cache_mem_k, cache_mem_v, ln1_g, ln1_b, w_cq, w_co, ln2_g, ln2_b,
                             w_pq, peer_keys_a, peer_keys_b, peer_u, peer_v, ln3_g, ln3_b)

    return (y_prompt, y_sample, p_state_ret, p_k, p_v, p_idx_k, p_mem_k, p_mem_v,
            s_state_ret, sak, sav, sik)
```

```python
import functools

import jax
import jax.numpy as jnp
from jax import lax
from jax.experimental import pallas as pl
from jax.experimental.pallas import tpu as pltpu

F32 = jnp.float32
BF16 = jnp.bfloat16
I32 = jnp.int32

D_MODEL = 1024
DEPTH = 1
PAST_LEN = 8192
PAGE_SIZE = 128
RET_HEADS = 4
RET_DK = 128
RET_DV = 128
RET_W = RET_HEADS * RET_DV
RET_CHUNK = 128
ATT_HEADS = 4
ATT_HD = 128
ATT_W = ATT_HEADS * ATT_HD
IDX_HEADS = 8
IDX_DIM = 64
IDX_TOPK_MAX = 256
MEM_TOKENS = 256
MEM_HEADS = 4
MEM_HD = D_MODEL // MEM_HEADS
PEER_HEADS = 8
PEER_NKEYS = 128
PEER_EXPERTS = PEER_NKEYS * PEER_NKEYS
PEER_QDIM = 256
PEER_HALF = PEER_QDIM // 2
PEER_TOPK = 16
ROPE_THETA = 10000.0
LN_EPS = 1e-5
GN_EPS = 1e-5
DN_ALPHA = (2 * DEPTH) ** 0.25

LANES = 128
SUBLANES = 8
GROUP_W = 512
N_GROUPS = 8
IN_COLS_PAD = N_GROUPS * GROUP_W + LANES
VMEM_LIMIT = 56 * 1024 * 1024
INT_MIN = -(2 ** 31)
NEG_INF = float("-inf")

_NT = (((1,), (1,)), ((), ()))


def _cparams(sem):
    return pltpu.CompilerParams(dimension_semantics=sem, vmem_limit_bytes=VMEM_LIMIT)


def _rope_tables(pos):
    def tab(d):
        half = d // 2
        inv = ROPE_THETA ** (-jnp.arange(half, dtype=F32) * (2.0 / d))
        ang = pos.astype(F32)[:, None] * inv[None, :]
        cos, sin = jnp.cos(ang), jnp.sin(ang)
        reps = LANES // d
        return (jnp.tile(jnp.concatenate([cos, cos], axis=1), (1, reps)),
                jnp.tile(jnp.concatenate([-sin, sin], axis=1), (1, reps)))
    c128, s128 = tab(ATT_HD)
    c64, s64 = tab(IDX_DIM)
    return c128, s128, c64, s64


def _proj_kernel(x_ref, w_ref, c128_ref, s128_ref, c64_ref, s64_ref,
                 rq_ref, rk_ref, rv_ref, rg_ref, aq_ref, ak_ref, akb_ref, av_ref, avb_ref,
                 iq_ref, ikw_ref, ik2_ref):
    xb = x_ref[...].astype(BF16)
    c128 = c128_ref[...]
    s128 = s128_ref[...]
    c64 = c64_ref[...]
    s64 = s64_ref[...]
    tm = xb.shape[0]
    lane = lax.broadcasted_iota(I32, (tm, LANES), 1)
    first_half64 = (lane & (IDX_DIM // 2)) == 0

    def group(g, width=GROUP_W):
        return jnp.dot(xb, w_ref[:, g * GROUP_W:g * GROUP_W + width], preferred_element_type=F32)

    def rope128(p):
        return p * c128 + pltpu.roll(p, ATT_HD // 2, 1) * s128

    def rope64(p):
        rot = jnp.where(first_half64, pltpu.roll(p, LANES - IDX_DIM // 2, 1), pltpu.roll(p, IDX_DIM // 2, 1))
        return p * c64 + rot * s64

    p = group(0)
    for h in range(RET_HEADS):
        sl = slice(h * LANES, (h + 1) * LANES)
        rq_ref[:, sl] = rope128(p[:, sl]).astype(BF16)
    p = group(1)
    for h in range(RET_HEADS):
        sl = slice(h * LANES, (h + 1) * LANES)
        rk_ref[:, sl] = (rope128(p[:, sl]) * (RET_DK ** -0.5)).astype(BF16)
    rv_ref[...] = group(2).astype(BF16)
    rg_ref[...] = group(3)
    p = group(4)
    for h in range(ATT_HEADS):
        sl = slice(h * LANES, (h + 1) * LANES)
        aq_ref[:, sl] = rope128(p[:, sl]).astype(BF16)
    p = group(5)
    for h in range(ATT_HEADS):
        sl = slice(h * LANES, (h + 1) * LANES)
        r = rope128(p[:, sl])
        ak_ref[:, sl] = r
        akb_ref[:, sl] = r.astype(BF16)
    p = group(6)
    av_ref[...] = p
    avb_ref[...] = p.astype(BF16)
    p = group(7)
    for j in range(GROUP_W // LANES):
        sl = slice(j * LANES, (j + 1) * LANES)
        iq_ref[:, sl] = rope64(p[:, sl]).astype(BF16)
    p = group(8, LANES)
    is_key = lane < IDX_DIM
    r = jnp.where(is_key, rope64(p), p)
    ikw_ref[...] = r
    ik2_ref[...] = jnp.where(is_key, r, pltpu.roll(r, IDX_DIM, 1)).astype(BF16)


def _project(x2d, w_pad, tables, period_tiles, tm):
    n = x2d.shape[0]
    nt = n // tm
    row = lambda i: (i, 0)
    tab = lambda i: (i % period_tiles, 0)
    bspec = lambda w: pl.BlockSpec((tm, w), row)
    outs = [
        (GROUP_W, BF16), (GROUP_W, BF16), (GROUP_W, BF16), (GROUP_W, F32),
        (GROUP_W, BF16), (GROUP_W, F32), (GROUP_W, BF16), (GROUP_W, F32), (GROUP_W, BF16),
        (GROUP_W, BF16), (LANES, F32), (LANES, BF16),
    ]
    return pl.pallas_call(
        _proj_kernel,
        grid=(nt,),
        in_specs=[pl.BlockSpec((tm, D_MODEL), row),
                  pl.BlockSpec((D_MODEL, IN_COLS_PAD), lambda i: (0, 0))]
                 + [pl.BlockSpec((tm, LANES), tab)] * 4,
        out_specs=[bspec(w) for w, _ in outs],
        out_shape=[jax.ShapeDtypeStruct((n, w), dt) for w, dt in outs],
        compiler_params=_cparams(("parallel",)),
        name="proj_rope",
    )(x2d, w_pad, *tables)


def _pad_w_in(w_in):
    return jnp.pad(w_in.astype(BF16), ((0, 0), (0, IN_COLS_PAD - w_in.shape[1])))


def _retention_log_decay():
    return jnp.log(1.0 - 2.0 ** (-5.0 - jnp.arange(RET_HEADS, dtype=F32)))


def _retention_tables(chunk_len):
    log_g = _retention_log_decay()
    j = jnp.arange(RET_CHUNK, dtype=F32)
    diff = j[:, None] - j[None, :]
    dmat = jnp.where(diff >= 0, jnp.exp(log_g[:, None, None] * jnp.maximum(diff, 0.0)), 0.0)
    q_dec = jnp.exp(log_g[None, :] * (j[:, None] + 1.0))
    k_dec = jnp.exp(log_g[None, :] * (chunk_len - 1.0 - j[:, None]))
    k_dec = jnp.where(j[:, None] < chunk_len, k_dec, 0.0)
    s_dec = jnp.exp(log_g * chunk_len)
    widen = lambda t: jnp.repeat(t, LANES, axis=1)
    return dmat, widen(q_dec), widen(k_dec), s_dec


def _retention_kernel(sdec_ref, q_ref, k_ref, v_ref, g_ref, s0_ref, dmat_ref, qdec_ref, kdec_ref,
                      gng_ref, gnb_ref, y_ref, sout_ref, state_ref):
    c = pl.program_id(1)

    @pl.when(c == 0)
    def _():
        state_ref[...] = s0_ref[0]

    for h in range(RET_HEADS):
        sl = slice(h * LANES, (h + 1) * LANES)
        q = q_ref[:, sl]
        k = k_ref[:, sl]
        v = v_ref[:, sl]
        s_prev = state_ref[h]
        scores = lax.dot_general(q, k, _NT, preferred_element_type=F32) * dmat_ref[h]
        o = jnp.dot(scores.astype(BF16), v, preferred_element_type=F32)
        o = o + jnp.dot(q, s_prev.astype(BF16), preferred_element_type=F32) * qdec_ref[:, sl]
        kd_t = (k.astype(F32) * kdec_ref[:, sl]).T.astype(BF16)
        state_ref[h] = s_prev * sdec_ref[h] + jnp.dot(kd_t, v, preferred_element_type=F32)
        mu = jnp.mean(o, axis=-1, keepdims=True)
        var = jnp.mean(jnp.square(o - mu), axis=-1, keepdims=True)
        n = (o - mu) * lax.rsqrt(var + GN_EPS) * gng_ref[:, sl] + gnb_ref[:, sl]
        g = g_ref[:, sl]
        y_ref[:, sl] = (g * (1.0 / (1.0 + jnp.exp(-g))) * n).astype(BF16)

    @pl.when(c == pl.num_programs(1) - 1)
    def _():
        sout_ref[0] = state_ref[...]


def _retention(q, k, v, g, s0, tables, gn_g, gn_b, nb, nc):
    dmat, qdec, kdec, sdec = tables
    n = q.shape[0]
    row = lambda b, c: (b * nc + c, 0)
    const2 = lambda b, c: (0, 0)
    blk = pl.BlockSpec((RET_CHUNK, RET_W), row)
    return pl.pallas_call(
        _retention_kernel,
        grid=(nb, nc),
        in_specs=[pl.BlockSpec(memory_space=pltpu.SMEM), blk, blk, blk, blk,
                  pl.BlockSpec((1, RET_HEADS, RET_DK, RET_DV), lambda b, c: (b, 0, 0, 0)),
                  pl.BlockSpec((RET_HEADS, RET_CHUNK, RET_CHUNK), lambda b, c: (0, 0, 0)),
                  pl.BlockSpec((RET_CHUNK, RET_W), const2), pl.BlockSpec((RET_CHUNK, RET_W), const2),
                  pl.BlockSpec((1, RET_W), const2), pl.BlockSpec((1, RET_W), const2)],
        out_specs=[blk, pl.BlockSpec((1, RET_HEADS, RET_DK, RET_DV), lambda b, c: (b, 0, 0, 0))],
        out_shape=[jax.ShapeDtypeStruct((n, RET_W), BF16),
                   jax.ShapeDtypeStruct((nb, RET_HEADS, RET_DK, RET_DV), F32)],
        scratch_shapes=[pltpu.VMEM((RET_HEADS, RET_DK, RET_DV), F32)],
        compiler_params=_cparams(("parallel", "arbitrary")),
        name="retention",
    )(sdec, q, k, v, g, s0, dmat, qdec, kdec, gn_g.reshape(1, RET_W), gn_b.reshape(1, RET_W))


def _sort_key(x):
    bits = pltpu.bitcast(x, I32)
    return bits ^ ((bits >> 31) & 0x7FFFFFFF)


def _index_scores(iq, w_all, ik2, lane):
    acc = None
    for h in range(IDX_HEADS):
        pair = iq[:, (h // 2) * LANES:(h // 2 + 1) * LANES]
        keep = (lane < IDX_DIM) if h % 2 == 0 else (lane >= IDX_DIM)
        qh = jnp.where(keep, pair, jnp.zeros_like(pair))
        s = lax.dot_general(qh, ik2, _NT, preferred_element_type=F32)
        term = jnp.maximum(s, 0.0) * w_all[:, IDX_DIM + h:IDX_DIM + h + 1]
        acc = term if acc is None else acc + term
    return acc


def _kth_largest_key(count_ge, rows, topk):
    def bit_body(bi, t):
        cand = t + lax.shift_left(jnp.int32(1), 31 - bi)
        return jnp.where(count_ge(cand) >= topk, cand, t)
    return lax.fori_loop(0, 32, bit_body, jnp.full((rows, LANES), INT_MIN, I32))


def _tie_limit(count_eq_below, need, rows, pos_bits):
    def bit_body(bi, p):
        cand = p + lax.shift_left(jnp.int32(1), pos_bits - 1 - bi)
        return jnp.where(count_eq_below(cand) <= need, cand, p)
    return lax.fori_loop(0, pos_bits, bit_body, jnp.zeros((rows, LANES), I32))


def _dsa_prompt_kernel(aq_ref, iq_ref, ikw_ref, ik2_ref, ak_ref, av_ref, o_ref, keys_ref, *, tq, kc, topk, pos_bits):
    i = pl.program_id(1)
    q0 = i * tq
    nk = (q0 + tq + kc - 1) // kc
    lane = lax.broadcasted_iota(I32, (tq, LANES), 1)
    qpos = q0 + lax.broadcasted_iota(I32, (tq, kc), 0)
    col = lax.broadcasted_iota(I32, (tq, kc), 1)
    iq = iq_ref[...]
    w_all = ikw_ref[...]

    def score_chunk(c, carry):
        k0 = pl.multiple_of(c * kc, kc)
        s = _index_scores(iq, w_all, ik2_ref[pl.ds(k0, kc), :], lane)
        keys_ref[c] = jnp.where(k0 + col <= qpos, _sort_key(s), INT_MIN)
        return carry
    lax.fori_loop(0, nk, score_chunk, 0)

    def lane_total(a):
        return jnp.broadcast_to(jnp.sum(a, axis=1, keepdims=True), (tq, LANES))

    def count_ge(cand):
        def body(c, a):
            key = keys_ref[c]
            for g in range(kc // LANES):
                a = a + jnp.where(key[:, g * LANES:(g + 1) * LANES] >= cand, 1, 0)
            return a
        return lane_total(lax.fori_loop(0, nk, body, jnp.zeros((tq, LANES), I32)))

    t = _kth_largest_key(count_ge, tq, topk)
    n_ge = count_ge(t)
    need = topk - count_ge(t + 1)

    def count_eq_below(p):
        def body(c, a):
            key = keys_ref[c]
            for g in range(kc // LANES):
                kpos = c * kc + g * LANES + lane
                hit = jnp.where(key[:, g * LANES:(g + 1) * LANES] == t, 1, 0)
                a = a + jnp.where(kpos < p, hit, 0)
            return a
        return lane_total(lax.fori_loop(0, nk, body, jnp.zeros((tq, LANES), I32)))

    has_excess = jnp.max(jnp.where(n_ge > topk, 1, 0)) > 0
    plim = lax.cond(has_excess,
                    lambda: _tie_limit(count_eq_below, need, tq, pos_bits),
                    lambda: jnp.full((tq, LANES), 2 ** pos_bits, I32))
    t_row = t[:, :1]
    p_row = plim[:, :1]

    aq = aq_ref[...]
    scale = ATT_HD ** -0.5

    def attend_chunk(c, carry):
        k0 = pl.multiple_of(c * kc, kc)
        key = keys_ref[c]
        sel = (key > t_row) | ((key == t_row) & (k0 + col < p_row))
        sel = sel & (key != INT_MIN)
        kch = ak_ref[pl.ds(k0, kc), :]
        vch = av_ref[pl.ds(k0, kc), :]
        out = []
        for h in range(ATT_HEADS):
            m, l, acc = carry[h]
            sl = slice(h * LANES, (h + 1) * LANES)
            s = lax.dot_general(aq[:, sl], kch[:, sl], _NT, preferred_element_type=F32) * scale
            mc = jnp.max(jnp.where(sel, s, NEG_INF), axis=1, keepdims=True)
            m_new = jnp.maximum(m, mc)
            m_safe = jnp.where(m_new == NEG_INF, 0.0, m_new)
            p = jnp.where(sel, jnp.exp(s - m_safe), 0.0)
            alpha = jnp.exp(m - m_safe)
            l_new = alpha * l + jnp.sum(p, axis=1, keepdims=True)
            acc_new = alpha * acc + jnp.dot(p.astype(BF16), vch[:, sl], preferred_element_type=F32)
            out.append((m_new, l_new, acc_new))
        return tuple(out)

    init = tuple((jnp.full((tq, 1), NEG_INF, F32), jnp.zeros((tq, 1), F32), jnp.zeros((tq, LANES), F32))
                 for _ in range(ATT_HEADS))
    fin = lax.fori_loop(0, nk, attend_chunk, init)
    for h in range(ATT_HEADS):
        m, l, acc = fin[h]
        o_ref[:, h * LANES:(h + 1) * LANES] = (acc / l).astype(BF16)


def _dsa_prompt(aq, iq, ikw, ik2, akb, avb, nb, t_len, topk, tq=128, kc=512):
    n = aq.shape[0]
    nq = t_len // tq
    kc = min(kc, t_len)
    qrow = lambda b, i: (b * nq + i, 0)
    brow = lambda b, i: (b, 0)
    kern = functools.partial(_dsa_prompt_kernel, tq=tq, kc=kc, topk=topk, pos_bits=(t_len - 1).bit_length() + 1)
    return pl.pallas_call(
        kern,
        grid=(nb, nq),
        in_specs=[pl.BlockSpec((tq, ATT_W), qrow), pl.BlockSpec((tq, GROUP_W), qrow), pl.BlockSpec((tq, LANES), qrow),
                  pl.BlockSpec((t_len, LANES), brow), pl.BlockSpec((t_len, ATT_W), brow),
                  pl.BlockSpec((t_len, ATT_W), brow)],
        out_specs=pl.BlockSpec((tq, ATT_W), qrow),
        out_shape=jax.ShapeDtypeStruct((n, ATT_W), BF16),
        scratch_shapes=[pltpu.VMEM((t_len // kc, tq, kc), I32)],
        compiler_params=_cparams(("parallel", "arbitrary")),
        name="dsa_prompt",
    )(aq, iq, ikw, ik2, akb, avb)


QPAD = SUBLANES


def _dsa_sample_select_kernel(pt_ref, iq_ref, w_ref, iknew_ref, *rest, pg, n_pages, n_new, topk, pos_bits):
    idx_refs = rest[:pg]
    sel_ref = rest[pg]
    keys_ref = rest[pg + 1]
    p = pl.program_id(1)
    iq = iq_ref[0]
    w = w_ref[0]
    lane = lax.broadcasted_iota(I32, (QPAD, LANES), 1)
    row = lax.broadcasted_iota(I32, (QPAD, LANES), 0)

    def scores(ik):
        s = lax.dot_general(iq, ik.astype(BF16), _NT, preferred_element_type=F32)
        r = jnp.maximum(s, 0.0) * w
        return jnp.sum(r.reshape(QPAD, IDX_HEADS, r.shape[1]), axis=1)

    for j in range(pg):
        keys_ref[p * pg + j] = _sort_key(scores(idx_refs[j][0]))

    @pl.when(p == pl.num_programs(1) - 1)
    def _():
        s_new = scores(iknew_ref[0])
        keys_ref[n_pages] = jnp.where((lane <= row) & (lane < n_new), _sort_key(s_new), INT_MIN)
        nblk = n_pages + 1

        def lane_total(a):
            return jnp.broadcast_to(jnp.sum(a, axis=1, keepdims=True), (QPAD, LANES))

        def count_ge(cand):
            body = lambda c, a: a + jnp.where(keys_ref[c] >= cand, 1, 0)
            return lane_total(lax.fori_loop(0, nblk, body, jnp.zeros((QPAD, LANES), I32)))

        t = _kth_largest_key(count_ge, QPAD, topk)
        need = topk - count_ge(t + 1)

        def count_eq_below(lim):
            def body(c, a):
                hit = jnp.where(keys_ref[c] == t, 1, 0)
                return a + jnp.where(c * LANES + lane < lim, hit, 0)
            return lane_total(lax.fori_loop(0, nblk, body, jnp.zeros((QPAD, LANES), I32)))

        has_excess = jnp.max(jnp.where(count_ge(t) > topk, 1, 0)) > 0
        plim = lax.cond(has_excess,
                        lambda: _tie_limit(count_eq_below, need, QPAD, pos_bits),
                        lambda: jnp.full((QPAD, LANES), 2 ** pos_bits, I32))

        def emit(c, carry):
            key = keys_ref[c]
            sel = (key > t) | ((key == t) & (c * LANES + lane < plim))
            sel_ref[0, c] = jnp.where(sel & (key != INT_MIN), 1, 0)
            return carry
        lax.fori_loop(0, nblk, emit, 0)


def _dsa_sample_select(pt_flat, iq_rows, w_rows, ik_new, cache_idx_k, nb, n_pages, n_new, topk, pg):
    kern = functools.partial(_dsa_sample_select_kernel, pg=pg, n_pages=n_pages, n_new=n_new, topk=topk,
                             pos_bits=((n_pages + 1) * LANES).bit_length())
    per_b = lambda shape: pl.BlockSpec((1,) + shape, lambda b, p, pt: (b,) + (0,) * len(shape))
    page = lambda j: pl.BlockSpec((1, PAGE_SIZE, IDX_DIM), lambda b, p, pt: (pt[b * n_pages + p * pg + j], 0, 0))
    return pl.pallas_call(
        kern,
        grid_spec=pltpu.PrefetchScalarGridSpec(
            num_scalar_prefetch=1,
            grid=(nb, n_pages // pg),
            in_specs=[per_b((QPAD * IDX_HEADS, IDX_DIM)), per_b((QPAD * IDX_HEADS, 1)), per_b((LANES, IDX_DIM))]
                     + [page(j) for j in range(pg)],
            out_specs=per_b((n_pages + 1, QPAD, LANES)),
            scratch_shapes=[pltpu.VMEM((n_pages + 1, QPAD, LANES), I32)]),
        out_shape=jax.ShapeDtypeStruct((nb, n_pages + 1, QPAD, LANES), I32),
        compiler_params=_cparams(("parallel", "arbitrary")),
        name="dsa_sample_select",
    )(pt_flat, iq_rows, w_rows, ik_new, *([cache_idx_k] * pg))


def _dsa_sample_attend_kernel(pt_ref, q_ref, sel_ref, knew_ref, vnew_ref, *rest, pg, n_pages):
    k_refs = rest[:pg]
    v_refs = rest[pg:2 * pg]
    o_ref = rest[2 * pg]
    m_ref, l_ref, acc_ref = rest[2 * pg + 1:]
    p = pl.program_id(1)
    scale = ATT_HD ** -0.5

    @pl.when(p == 0)
    def _():
        m_ref[...] = jnp.full(m_ref.shape, NEG_INF, F32)
        l_ref[...] = jnp.zeros(l_ref.shape, F32)
        acc_ref[...] = jnp.zeros(acc_ref.shape, F32)

    q = q_ref[0].astype(BF16)

    def update(h, sel, kh, vh):
        sl = slice(h * LANES, (h + 1) * LANES)
        s = lax.dot_general(q[:, sl], kh.astype(BF16), _NT, preferred_element_type=F32) * scale
        m = m_ref[h]
        mc = jnp.max(jnp.where(sel, s, NEG_INF), axis=1, keepdims=True)
        m_new = jnp.maximum(m, mc)
        m_safe = jnp.where(m_new == NEG_INF, 0.0, m_new)
        pr = jnp.where(sel, jnp.exp(s - m_safe[:, :1]), 0.0)
        alpha = jnp.exp(m - m_safe)
        l_ref[h] = alpha * l_ref[h] + jnp.sum(pr, axis=1, keepdims=True)
        acc_ref[h] = alpha * acc_ref[h] + jnp.dot(pr.astype(BF16), vh.astype(BF16), preferred_element_type=F32)
        m_ref[h] = m_new

    for j in range(pg):
        sel = sel_ref[0, p * pg + j] != 0
        for h in range(ATT_HEADS):
            update(h, sel, k_refs[j][0, :, h, :], v_refs[j][0, :, h, :])

    @pl.when(p == pl.num_programs(1) - 1)
    def _():
        sel = sel_ref[0, n_pages] != 0
        for h in range(ATT_HEADS):
            sl = slice(h * LANES, (h + 1) * LANES)
            update(h, sel, knew_ref[0, :, sl], vnew_ref[0, :, sl])
            o_ref[0, :, sl] = acc_ref[h] / l_ref[h]


def _dsa_sample_attend(pt_flat, q, sel, k_new, v_new, cache_k, cache_v, nb, n_pages, pg):
    kern = functools.partial(_dsa_sample_attend_kernel, pg=pg, n_pages=n_pages)
    per_b = lambda shape: pl.BlockSpec((1,) + shape, lambda b, p, pt: (b,) + (0,) * len(shape))
    page = lambda j: pl.BlockSpec((1, PAGE_SIZE, ATT_HEADS, ATT_HD),
                                  lambda b, p, pt: (pt[b * n_pages + p * pg + j], 0, 0, 0))
    return pl.pallas_call(
        kern,
        grid_spec=pltpu.PrefetchScalarGridSpec(
            num_scalar_prefetch=1,
            grid=(nb, n_pages // pg),
            in_specs=[per_b((QPAD, ATT_W)), per_b((n_pages + 1, QPAD, LANES)), per_b((LANES, ATT_W)),
                      per_b((LANES, ATT_W))] + [page(j) for j in range(pg)] * 2,
            out_specs=per_b((QPAD, ATT_W)),
            scratch_shapes=[pltpu.VMEM((ATT_HEADS, QPAD, LANES), F32)] * 3),
        out_shape=jax.ShapeDtypeStruct((nb, QPAD, ATT_W), F32),
        compiler_params=_cparams(("parallel", "arbitrary")),
        name="dsa_sample_attend",
    )(pt_flat, q, sel, k_new, v_new, *([cache_k] * pg), *([cache_v] * pg))


def _layer_norm(x, g, b):
    mu = jnp.mean(x, axis=-1, keepdims=True)
    var = jnp.mean(jnp.square(x - mu), axis=-1, keepdims=True)
    return (x - mu) * lax.rsqrt(var + LN_EPS) * g + b


def _matmul_kernel(x_ref, w_ref, o_ref):
    o_ref[...] = jnp.dot(x_ref[...].astype(BF16), w_ref[...], preferred_element_type=F32)


def _matmul(x, w_bf16, tm):
    n, kdim = x.shape
    m = w_bf16.shape[1]
    return pl.pallas_call(
        _matmul_kernel,
        grid=(n // tm,),
        in_specs=[pl.BlockSpec((tm, kdim), lambda i: (i, 0)), pl.BlockSpec((kdim, m), lambda i: (0, 0))],
        out_specs=pl.BlockSpec((tm, m), lambda i: (i, 0)),
        out_shape=jax.ShapeDtypeStruct((n, m), F32),
        compiler_params=_cparams(("parallel",)),
        name="mem_proj",
    )(x, w_bf16)


def _tail1_kernel(x_ref, ry_ref, ao_ref, mk_ref, mv_ref, wout_ref, wcq_ref, wco_ref,
                  ln1g_ref, ln1b_ref, ln2g_ref, ln2b_ref, h2_ref):
    x = x_ref[0]
    mix = jnp.dot(ry_ref[0].astype(BF16), wout_ref[:RET_W, :], preferred_element_type=F32)
    mix = mix + jnp.dot(ao_ref[0].astype(BF16), wout_ref[RET_W:, :], preferred_element_type=F32)
    h1 = _layer_norm(DN_ALPHA * x + mix, ln1g_ref[...], ln1b_ref[...])
    q = jnp.dot(h1.astype(BF16), wcq_ref[...], preferred_element_type=F32).astype(BF16)
    heads = []
    for h in range(MEM_HEADS):
        sl = slice(h * MEM_HD, (h + 1) * MEM_HD)
        mk = mk_ref[0, :, sl].astype(BF16)
        mv = mv_ref[0, :, sl].astype(BF16)
        s = lax.dot_general(q[:, sl], mk, _NT, preferred_element_type=F32) * (MEM_HD ** -0.5)
        e = jnp.exp(s - jnp.max(s, axis=-1, keepdims=True))
        p = e / jnp.sum(e, axis=-1, keepdims=True)
        heads.append(jnp.dot(p.astype(BF16), mv, preferred_element_type=F32).astype(BF16))
    ca = jnp.concatenate(heads, axis=1)
    h2 = DN_ALPHA * h1 + jnp.dot(ca, wco_ref[...], preferred_element_type=F32)
    h2_ref[0] = _layer_norm(h2, ln2g_ref[...], ln2b_ref[...])


def _tail1(x, ret_y, att_o, mem_k, mem_v, w_out, w_cq, w_co, ln1, ln2, tm):
    g, tg, d = x.shape
    tok = lambda w: pl.BlockSpec((1, tm, w), lambda b, i: (b, i, 0))
    mem = pl.BlockSpec((1, MEM_TOKENS, d), lambda b, i: (b, 0, 0))
    wsp = lambda r, c: pl.BlockSpec((r, c), lambda b, i: (0, 0))
    vec = wsp(1, d)
    return pl.pallas_call(
        _tail1_kernel,
        grid=(g, tg // tm),
        in_specs=[tok(d), tok(RET_W), tok(ATT_W), mem, mem, wsp(d, d), wsp(d, d), wsp(d, d), vec, vec, vec, vec],
        out_specs=tok(d),
        out_shape=jax.ShapeDtypeStruct((g, tg, d), F32),
        compiler_params=_cparams(("parallel", "arbitrary")),
        name="tail1",
    )(x, ret_y, att_o, mem_k, mem_v, w_out, w_cq, w_co,
      ln1[0].reshape(1, d), ln1[1].reshape(1, d), ln2[0].reshape(1, d), ln2[1].reshape(1, d))


HALF_EXPERTS = PEER_EXPERTS // 2
SHIFT_FIELD = HALF_EXPERTS.bit_length() - 1
ROW_TILE = (SUBLANES, LANES)


def _pack_expert_table(w):
    bits = lax.bitcast_convert_type(w.astype(BF16), jnp.uint16).astype(jnp.uint32)
    packed = bits[:HALF_EXPERTS] | (bits[HALF_EXPERTS:] << 16)
    return lax.bitcast_convert_type(packed, I32).reshape(HALF_EXPERTS, *ROW_TILE)


def _top_rows(s, k, payload=None, order=None):
    rows = s.shape[0]
    rid = lax.broadcasted_iota(I32, s.shape, 0)
    order = rid if order is None else order
    payload = rid if payload is None else payload
    big = jnp.int32(2 ** 30)
    vals, pays = [], []
    for _ in range(k):
        m = jnp.max(s, axis=0, keepdims=True)
        win = jnp.min(jnp.where(s == m, order, big), axis=0, keepdims=True)
        hit = order == win
        vals.append(m)
        pays.append(jnp.max(jnp.where(hit, payload, -1), axis=0, keepdims=True))
        s = jnp.where(hit, NEG_INF, s)
    return jnp.concatenate(vals, axis=0), jnp.concatenate(pays, axis=0)


def _peer_route_kernel(h_ref, wpq_ref, ka_ref, kb_ref, code_ref, gate_ref):
    hb = h_ref[...].astype(BF16)
    tm = hb.shape[0]
    ka = ka_ref[...]
    kb = kb_ref[...]
    a8 = lax.broadcasted_iota(I32, (SUBLANES, tm), 0)
    b16 = lax.broadcasted_iota(I32, (PEER_TOPK, tm), 0)
    for hd in range(PEER_HEADS):
        q = jnp.dot(hb, wpq_ref[:, hd * PEER_QDIM:(hd + 1) * PEER_QDIM], preferred_element_type=F32).astype(BF16)
        s1 = lax.dot_general(ka, q[:, :PEER_HALF], _NT, preferred_element_type=F32)
        s2 = lax.dot_general(kb, q[:, PEER_HALF:], _NT, preferred_element_type=F32)
        v1, i1 = _top_rows(s1, PEER_TOPK)
        v2, i2 = _top_rows(s2, PEER_TOPK)
        cand, cid, fid = [], [], []

        def block(vals, ids, flat, valid):
            cand.append(jnp.where(valid, vals, NEG_INF) if valid is not None else vals)
            cid.append(ids)
            fid.append(flat)

        block(v1[0:1] + v2, i1[0:1] * PEER_NKEYS + i2, b16, None)
        block(v1[1:2] + v2[:8], i1[1:2] * PEER_NKEYS + i2[:8], PEER_TOPK + a8, None)
        block(v1[8:] + v2[0:1], i1[8:] * PEER_NKEYS + i2[0:1], (a8 + 8) * PEER_TOPK, None)
        for b in range(5):
            lim = PEER_TOPK // (b + 1)
            block(v1[:8] + v2[b:b + 1], i1[:8] * PEER_NKEYS + i2[b:b + 1], a8 * PEER_TOPK + b,
                  (a8 >= 2) & (a8 < lim))
        sc, eid = _top_rows(jnp.concatenate(cand, axis=0), PEER_TOPK,
                            payload=jnp.concatenate(cid, axis=0), order=jnp.concatenate(fid, axis=0))
        e = jnp.exp(sc - sc[0:1])
        rows = slice(hd * PEER_TOPK, (hd + 1) * PEER_TOPK)
        gate_ref[rows, :] = e / jnp.sum(e, axis=0, keepdims=True)
        shift = 16 - ((eid >> SHIFT_FIELD) << 4)
        code_ref[rows, :] = (eid & (HALF_EXPERTS - 1)) | (shift << SHIFT_FIELD)


def _peer_route(h2d, w_pq, keys_a, keys_b, tm):
    n, d = h2d.shape
    slots = PEER_HEADS * PEER_TOPK
    return pl.pallas_call(
        _peer_route_kernel,
        grid=(n // tm,),
        in_specs=[pl.BlockSpec((tm, d), lambda i: (i, 0)),
                  pl.BlockSpec((d, PEER_HEADS * PEER_QDIM), lambda i: (0, 0)),
                  pl.BlockSpec((PEER_NKEYS, PEER_HALF), lambda i: (0, 0)),
                  pl.BlockSpec((PEER_NKEYS, PEER_HALF), lambda i: (0, 0))],
        out_specs=[pl.BlockSpec((slots, tm), lambda i: (0, i)), pl.BlockSpec((slots, tm), lambda i: (0, i))],
        out_shape=[jax.ShapeDtypeStruct((slots, n), I32), jax.ShapeDtypeStruct((slots, n), F32)],
        compiler_params=_cparams(("parallel",)),
        name="peer_route",
    )(h2d, w_pq, keys_a, keys_b)


def _expert_row(tab_ref, code):
    word = tab_ref[code & (HALF_EXPERTS - 1)]
    sh = jnp.full(ROW_TILE, code >> SHIFT_FIELD, I32)
    return pltpu.bitcast(lax.shift_left(word, sh) & jnp.int32(-65536), F32)


def _peer_hidden_kernel(code_ref, x_ref, gate_ref, tab_ref, coef_ref, part_ref, hid_ref):
    tm = x_ref.shape[0]
    slots = PEER_HEADS * PEER_TOPK
    lane = lax.broadcasted_iota(I32, (slots, tm), 1)
    hid_ref[...] = jnp.zeros((slots, tm), F32)

    def token(n, carry):
        x = x_ref[n]
        base = n * slots
        for g in range(slots // SUBLANES):
            rows = []
            for r in range(SUBLANES):
                prod = _expert_row(tab_ref, code_ref[base + g * SUBLANES + r]) * x
                rows.append(jnp.sum(prod, axis=0, keepdims=True))
            part_ref[g * SUBLANES:(g + 1) * SUBLANES, :] = jnp.concatenate(rows, axis=0)
        tot = jnp.sum(part_ref[...], axis=1, keepdims=True)
        hid_ref[...] = jnp.where(lane == n, tot, hid_ref[...])
        return carry
    lax.fori_loop(0, tm, token, 0)
    hid = hid_ref[...]
    gelu = 0.5 * hid * (1.0 + lax.erf(hid * (2.0 ** -0.5)))
    coef_ref[...] = gate_ref[...] * gelu


def _peer_hidden(codes_flat, x_tiles, gate, table, tm):
    n = x_tiles.shape[0]
    slots = PEER_HEADS * PEER_TOPK
    return pl.pallas_call(
        _peer_hidden_kernel,
        grid=(n // tm,),
        in_specs=[pl.BlockSpec((tm * slots,), lambda i: (i,), memory_space=pltpu.SMEM),
                  pl.BlockSpec((tm, *ROW_TILE), lambda i: (i, 0, 0)),
                  pl.BlockSpec((slots, tm), lambda i: (0, i)),
                  pl.BlockSpec(memory_space=pltpu.VMEM)],
        out_specs=pl.BlockSpec((slots, tm), lambda i: (0, i)),
        out_shape=jax.ShapeDtypeStruct((slots, n), F32),
        scratch_shapes=[pltpu.VMEM((slots, LANES), F32), pltpu.VMEM((slots, tm), F32)],
        compiler_params=_cparams(("arbitrary",)),
        name="peer_hidden",
    )(codes_flat, x_tiles, gate, table)


def _peer_combine_kernel(code_ref, coef_ref, tab_ref, o_ref):
    tm = o_ref.shape[0]
    slots = PEER_HEADS * PEER_TOPK

    def token(n, carry):
        base = n * slots
        acc = jnp.zeros(ROW_TILE, F32)
        for j in range(slots):
            acc = acc + coef_ref[base + j] * _expert_row(tab_ref, code_ref[base + j])
        o_ref[n] = acc
        return carry
    lax.fori_loop(0, tm, token, 0)


def _peer_combine(codes_flat, coef_flat, table, n, tm):
    slots = PEER_HEADS * PEER_TOPK
    smem = pl.BlockSpec((tm * slots,), lambda i: (i,), memory_space=pltpu.SMEM)
    return pl.pallas_call(
        _peer_combine_kernel,
        grid=(n // tm,),
        in_specs=[smem, smem, pl.BlockSpec(memory_space=pltpu.VMEM)],
        out_specs=pl.BlockSpec((tm, *ROW_TILE), lambda i: (i, 0, 0)),
        out_shape=jax.ShapeDtypeStruct((n, *ROW_TILE), F32),
        compiler_params=_cparams(("arbitrary",)),
        name="peer_combine",
    )(codes_flat, coef_flat, table)


def _ln3_kernel(h_ref, p_ref, g_ref, b_ref, y_ref):
    y_ref[...] = _layer_norm(DN_ALPHA * h_ref[...] + p_ref[...], g_ref[...], b_ref[...])


def _ln3(h2d, peer2d, g, b, tm):
    n, d = h2d.shape
    tok = pl.BlockSpec((tm, d), lambda i: (i, 0))
    vec = pl.BlockSpec((1, d), lambda i: (0, 0))
    return pl.pallas_call(
        _ln3_kernel,
        grid=(n // tm,),
        in_specs=[tok, tok, vec, vec],
        out_specs=tok,
        out_shape=jax.ShapeDtypeStruct((n, d), F32),
        compiler_params=_cparams(("parallel",)),
        name="ln3",
    )(h2d, peer2d, g.reshape(1, d), b.reshape(1, d))


def _peer_ffn_ln(h2d, w_pq, keys_a, keys_b, tab_u, tab_v, ln3, tm_route, tm_gather):
    n, d = h2d.shape
    codes, gate = _peer_route(h2d, w_pq, keys_a, keys_b, tm_route)
    codes_flat = codes.T.reshape(-1)
    coef = _peer_hidden(codes_flat, h2d.reshape(n, *ROW_TILE), gate, tab_u, tm_gather)
    out = _peer_combine(codes_flat, coef.T.reshape(-1), tab_v, n, tm_gather)
    return _ln3(h2d, out.reshape(n, d), ln3[0], ln3[1], tm_route)


TOKEN_TILE = 512
ROUTE_TILE = 256
GATHER_TILE = 128
SAMPLE_TPAD = 16
PAGES_PER_STEP = 8


def _pad_axis1(a, rows):
    return jnp.pad(a, ((0, 0), (0, rows - a.shape[1])) + ((0, 0),) * (a.ndim - 2))


def kernel(x_prompt, x_sample, mem_prompt, state_ret, cache_k, cache_v, cache_idx_k, cache_mem_k, cache_mem_v,
           page_table, w_in, ret_gn_g, ret_gn_b, w_out, ln1_g, ln1_b, w_cq, w_ck, w_cv, w_co, ln2_g, ln2_b,
           w_pq, peer_keys_a, peer_keys_b, peer_u, peer_v, ln3_g, ln3_b):
    nb, t_len, d = x_prompt.shape
    db, ds, _ = x_sample.shape
    n_pages = page_table.shape[1]
    past = n_pages * PAGE_SIZE
    bf = lambda w: w.astype(BF16)

    w_in_p = _pad_w_in(w_in)
    w_out_b, w_cq_b, w_co_b, w_pq_b = bf(w_out), bf(w_cq), bf(w_co), bf(w_pq)
    keys_a, keys_b = bf(peer_keys_a), bf(peer_keys_b)
    tab_u, tab_v = _pack_expert_table(peer_u), _pack_expert_table(peer_v)
    ln1, ln2, ln3 = (ln1_g, ln1_b), (ln2_g, ln2_b), (ln3_g, ln3_b)

    def tail(x, ret_y, att_o, mem_k, mem_v, tm):
        h2 = _tail1(x, ret_y, att_o, mem_k, mem_v, w_out_b, w_cq_b, w_co_b, ln1, ln2, tm)
        return h2

    def peer(h2d):
        return _peer_ffn_ln(h2d, w_pq_b, keys_a, keys_b, tab_u, tab_v, ln3, ROUTE_TILE, GATHER_TILE)

    n = nb * t_len
    tabs = _rope_tables(jnp.arange(t_len, dtype=I32))
    rq, rk, rv, rg, aq, ak, akb, av, avb, iq, ikw, ik2 = _project(
        x_prompt.reshape(n, d), w_in_p, tabs, t_len // TOKEN_TILE, TOKEN_TILE)
    ret_y, p_state = _retention(rq, rk, rv, rg, jnp.zeros((nb, RET_HEADS, RET_DK, RET_DV), F32),
                                _retention_tables(RET_CHUNK), ret_gn_g, ret_gn_b, nb, t_len // RET_CHUNK)
    att_o = _dsa_prompt(aq, iq, ikw, ik2, akb, avb, nb, t_len, min(IDX_TOPK_MAX, t_len // 4))
    mem_kv = _matmul(mem_prompt.reshape(nb * MEM_TOKENS, d), jnp.concatenate([bf(w_ck), bf(w_cv)], axis=1), MEM_TOKENS)
    p_mem_k = mem_kv[:, :d].reshape(nb, MEM_TOKENS, d)
    p_mem_v = mem_kv[:, d:].reshape(nb, MEM_TOKENS, d)
    h2 = tail(x_prompt, ret_y.reshape(nb, t_len, RET_W), att_o.reshape(nb, t_len, ATT_W), p_mem_k, p_mem_v, TOKEN_TILE)
    y_prompt = peer(h2.reshape(n, d)).reshape(nb, t_len, d)

    ns = db * ds
    pos_s = jnp.tile(past + jnp.arange(ds, dtype=I32), db)
    sq, sk, sv, sg, saq, sak, _, sav, _, siq, sikw, _ = _project(
        x_sample.reshape(ns, d), w_in_p, _rope_tables(pos_s), 1, ns)
    per_seq = lambda a: a.reshape(db, ds, a.shape[-1])
    chunk = lambda a: _pad_axis1(per_seq(a), RET_CHUNK).reshape(db * RET_CHUNK, a.shape[-1])
    ret_y_s, s_state = _retention(chunk(sq), chunk(sk), chunk(sv), chunk(sg), state_ret,
                                  _retention_tables(ds), ret_gn_g, ret_gn_b, db, 1)
    ret_y_s = ret_y_s.reshape(db, RET_CHUNK, RET_W)[:, :SAMPLE_TPAD]

    pt_flat = page_table.reshape(-1)
    iq_rows = _pad_axis1(per_seq(siq), QPAD).reshape(db, QPAD * IDX_HEADS, IDX_DIM)
    w_rows = _pad_axis1(per_seq(sikw[:, IDX_DIM:IDX_DIM + IDX_HEADS]), QPAD).reshape(db, QPAD * IDX_HEADS, 1)
    ik_new = _pad_axis1(per_seq(sikw[:, :IDX_DIM]), LANES)
    sel = _dsa_sample_select(pt_flat, iq_rows, w_rows, ik_new, cache_idx_k, db, n_pages, ds,
                             min(IDX_TOPK_MAX, (past + ds) // 4), PAGES_PER_STEP)
    att_s = _dsa_sample_attend(pt_flat, _pad_axis1(per_seq(saq).astype(F32), QPAD), sel,
                               _pad_axis1(per_seq(sak), LANES), _pad_axis1(per_seq(sav), LANES),
                               cache_k, cache_v, db, n_pages, PAGES_PER_STEP)
    h2_s = tail(_pad_axis1(x_sample, SAMPLE_TPAD), ret_y_s, _pad_axis1(att_s, SAMPLE_TPAD),
                cache_mem_k.reshape(db, MEM_TOKENS, d), cache_mem_v.reshape(db, MEM_TOKENS, d), SAMPLE_TPAD)
    y_sample = peer(h2_s[:, :ds].reshape(ns, d)).reshape(db, ds, d)

    heads = lambda a, b_, t_: a.reshape(b_, t_, ATT_HEADS, ATT_HD)
    mem_heads = lambda a: a.reshape(nb, MEM_TOKENS, MEM_HEADS, MEM_HD)
    return (y_prompt, y_sample, p_state, heads(ak, nb, t_len), heads(av, nb, t_len),
            ikw[:, :IDX_DIM].reshape(nb, t_len, IDX_DIM), mem_heads(p_mem_k), mem_heads(p_mem_v),
            s_state, heads(sak, db, ds), heads(sav, db, ds), sikw[:, :IDX_DIM].reshape(db, ds, IDX_DIM))
```

```python
import functools

import jax
import jax.numpy as jnp
from jax import lax
from jax.experimental import pallas as pl
from jax.experimental.pallas import tpu as pltpu

F32 = jnp.float32
BF16 = jnp.bfloat16
I32 = jnp.int32

D_MODEL = 1024
DEPTH = 1
PAST_LEN = 8192
PAGE_SIZE = 128
RET_HEADS = 4
RET_DK = 128
RET_DV = 128
RET_W = RET_HEADS * RET_DV
RET_CHUNK = 128
ATT_HEADS = 4
ATT_HD = 128
ATT_W = ATT_HEADS * ATT_HD
IDX_HEADS = 8
IDX_DIM = 64
IDX_TOPK_MAX = 256
MEM_TOKENS = 256
MEM_HEADS = 4
MEM_HD = D_MODEL // MEM_HEADS
PEER_HEADS = 8
PEER_NKEYS = 128
PEER_EXPERTS = PEER_NKEYS * PEER_NKEYS
PEER_QDIM = 256
PEER_HALF = PEER_QDIM // 2
PEER_TOPK = 16
ROPE_THETA = 10000.0
LN_EPS = 1e-5
GN_EPS = 1e-5
DN_ALPHA = (2 * DEPTH) ** 0.25

LANES = 128
SUBLANES = 8
GROUP_W = 512
N_GROUPS = 8
IN_COLS_PAD = N_GROUPS * GROUP_W + LANES
VMEM_LIMIT = 56 * 1024 * 1024
INT_MIN = -(2 ** 31)
NEG_INF = float("-inf")

_NT = (((1,), (1,)), ((), ()))


def _cparams(sem):
    return pltpu.CompilerParams(dimension_semantics=sem, vmem_limit_bytes=VMEM_LIMIT)


def _rope_tables(pos):
    def tab(d):
        half = d // 2
        inv = ROPE_THETA ** (-jnp.arange(half, dtype=F32) * (2.0 / d))
        ang = pos.astype(F32)[:, None] * inv[None, :]
        cos, sin = jnp.cos(ang), jnp.sin(ang)
        reps = LANES // d
        return (jnp.tile(jnp.concatenate([cos, cos], axis=1), (1, reps)),
                jnp.tile(jnp.concatenate([-sin, sin], axis=1), (1, reps)))
    c128, s128 = tab(ATT_HD)
    c64, s64 = tab(IDX_DIM)
    return c128, s128, c64, s64


def _proj_kernel(x_ref, w_ref, c128_ref, s128_ref, c64_ref, s64_ref,
                 rq_ref, rk_ref, rv_ref, rg_ref, aq_ref, ak_ref, akb_ref, av_ref, avb_ref,
                 iq_ref, ikw_ref, ik2_ref):
    xb = x_ref[...].astype(BF16)
    c128 = c128_ref[...]
    s128 = s128_ref[...]
    c64 = c64_ref[...]
    s64 = s64_ref[...]
    tm = xb.shape[0]
    lane = lax.broadcasted_iota(I32, (tm, LANES), 1)
    first_half64 = (lane & (IDX_DIM // 2)) == 0

    def group(g, width=GROUP_W):
        return jnp.dot(xb, w_ref[:, g * GROUP_W:g * GROUP_W + width], preferred_element_type=F32)

    def rope128(p):
        return p * c128 + pltpu.roll(p, ATT_HD // 2, 1) * s128

    def rope64(p):
        rot = jnp.where(first_half64, pltpu.roll(p, LANES - IDX_DIM // 2, 1), pltpu.roll(p, IDX_DIM // 2, 1))
        return p * c64 + rot * s64

    p = group(0)
    for h in range(RET_HEADS):
        sl = slice(h * LANES, (h + 1) * LANES)
        rq_ref[:, sl] = rope128(p[:, sl]).astype(BF16)
    p = group(1)
    for h in range(RET_HEADS):
        sl = slice(h * LANES, (h + 1) * LANES)
        rk_ref[:, sl] = (rope128(p[:, sl]) * (RET_DK ** -0.5)).astype(BF16)
    rv_ref[...] = group(2).astype(BF16)
    rg_ref[...] = group(3)
    p = group(4)
    for h in range(ATT_HEADS):
        sl = slice(h * LANES, (h + 1) * LANES)
        aq_ref[:, sl] = rope128(p[:, sl]).astype(BF16)
    p = group(5)
    for h in range(ATT_HEADS):
        sl = slice(h * LANES, (h + 1) * LANES)
        r = rope128(p[:, sl])
        ak_ref[:, sl] = r
        akb_ref[:, sl] = r.astype(BF16)
    p = group(6)
    av_ref[...] = p
    avb_ref[...] = p.astype(BF16)
    p = group(7)
    for j in range(GROUP_W // LANES):
        sl = slice(j * LANES, (j + 1) * LANES)
        iq_ref[:, sl] = rope64(p[:, sl]).astype(BF16)
    p = group(8, LANES)
    is_key = lane < IDX_DIM
    r = jnp.where(is_key, rope64(p), p)
    ikw_ref[...] = r
    ik2_ref[...] = jnp.where(is_key, r, pltpu.roll(r, IDX_DIM, 1)).astype(BF16)


def _project(x2d, w_pad, tables, period_tiles, tm):
    n = x2d.shape[0]
    nt = n // tm
    row = lambda i: (i, 0)
    tab = lambda i: (i % period_tiles, 0)
    bspec = lambda w: pl.BlockSpec((tm, w), row)
    outs = [
        (GROUP_W, BF16), (GROUP_W, BF16), (GROUP_W, BF16), (GROUP_W, F32),
        (GROUP_W, BF16), (GROUP_W, F32), (GROUP_W, BF16), (GROUP_W, F32), (GROUP_W, BF16),
        (GROUP_W, BF16), (LANES, F32), (LANES, BF16),
    ]
    return pl.pallas_call(
        _proj_kernel,
        grid=(nt,),
        in_specs=[pl.BlockSpec((tm, D_MODEL), row),
                  pl.BlockSpec((D_MODEL, IN_COLS_PAD), lambda i: (0, 0))]
                 + [pl.BlockSpec((tm, LANES), tab)] * 4,
        out_specs=[bspec(w) for w, _ in outs],
        out_shape=[jax.ShapeDtypeStruct((n, w), dt) for w, dt in outs],
        compiler_params=_cparams(("parallel",)),
        name="proj_rope",
    )(x2d, w_pad, *tables)


def _pad_w_in(w_in):
    return jnp.pad(w_in.astype(BF16), ((0, 0), (0, IN_COLS_PAD - w_in.shape[1])))


def _retention_log_decay():
    return jnp.log(1.0 - 2.0 ** (-5.0 - jnp.arange(RET_HEADS, dtype=F32)))


def _retention_tables(chunk_len):
    log_g = _retention_log_decay()
    j = jnp.arange(RET_CHUNK, dtype=F32)
    diff = j[:, None] - j[None, :]
    dmat = jnp.where(diff >= 0, jnp.exp(log_g[:, None, None] * jnp.maximum(diff, 0.0)), 0.0)
    q_dec = jnp.exp(log_g[None, :] * (j[:, None] + 1.0))
    k_dec = jnp.exp(log_g[None, :] * (chunk_len - 1.0 - j[:, None]))
    k_dec = jnp.where(j[:, None] < chunk_len, k_dec, 0.0)
    s_dec = jnp.exp(log_g * chunk_len)
    widen = lambda t: jnp.repeat(t, LANES, axis=1)
    return dmat, widen(q_dec), widen(k_dec), s_dec


def _retention_kernel(sdec_ref, q_ref, k_ref, v_ref, g_ref, s0_ref, dmat_ref, qdec_ref, kdec_ref,
                      gng_ref, gnb_ref, y_ref, sout_ref, state_ref):
    c = pl.program_id(1)

    @pl.when(c == 0)
    def _():
        state_ref[...] = s0_ref[0]

    for h in range(RET_HEADS):
        sl = slice(h * LANES, (h + 1) * LANES)
        q = q_ref[:, sl]
        k = k_ref[:, sl]
        v = v_ref[:, sl]
        s_prev = state_ref[h]
        scores = lax.dot_general(q, k, _NT, preferred_element_type=F32) * dmat_ref[h]
        o = jnp.dot(scores.astype(BF16), v, preferred_element_type=F32)
        o = o + jnp.dot(q, s_prev.astype(BF16), preferred_element_type=F32) * qdec_ref[:, sl]
        kd_t = (k.astype(F32) * kdec_ref[:, sl]).T.astype(BF16)
        state_ref[h] = s_prev * sdec_ref[h] + jnp.dot(kd_t, v, preferred_element_type=F32)
        mu = jnp.mean(o, axis=-1, keepdims=True)
        var = jnp.mean(jnp.square(o - mu), axis=-1, keepdims=True)
        n = (o - mu) * lax.rsqrt(var + GN_EPS) * gng_ref[:, sl] + gnb_ref[:, sl]
        g = g_ref[:, sl]
        y_ref[:, sl] = (g * (1.0 / (1.0 + jnp.exp(-g))) * n).astype(BF16)

    @pl.when(c == pl.num_programs(1) - 1)
    def _():
        sout_ref[0] = state_ref[...]


def _retention(q, k, v, g, s0, tables, gn_g, gn_b, nb, nc):
    dmat, qdec, kdec, sdec = tables
    n = q.shape[0]
    row = lambda b, c: (b * nc + c, 0)
    const2 = lambda b, c: (0, 0)
    blk = pl.BlockSpec((RET_CHUNK, RET_W), row)
    return pl.pallas_call(
        _retention_kernel,
        grid=(nb, nc),
        in_specs=[pl.BlockSpec(memory_space=pltpu.SMEM), blk, blk, blk, blk,
                  pl.BlockSpec((1, RET_HEADS, RET_DK, RET_DV), lambda b, c: (b, 0, 0, 0)),
                  pl.BlockSpec((RET_HEADS, RET_CHUNK, RET_CHUNK), lambda b, c: (0, 0, 0)),
                  pl.BlockSpec((RET_CHUNK, RET_W), const2), pl.BlockSpec((RET_CHUNK, RET_W), const2),
                  pl.BlockSpec((1, RET_W), const2), pl.BlockSpec((1, RET_W), const2)],
        out_specs=[blk, pl.BlockSpec((1, RET_HEADS, RET_DK, RET_DV), lambda b, c: (b, 0, 0, 0))],
        out_shape=[jax.ShapeDtypeStruct((n, RET_W), BF16),
                   jax.ShapeDtypeStruct((nb, RET_HEADS, RET_DK, RET_DV), F32)],
        scratch_shapes=[pltpu.VMEM((RET_HEADS, RET_DK, RET_DV), F32)],
        compiler_params=_cparams(("parallel", "arbitrary")),
        name="retention",
    )(sdec, q, k, v, g, s0, dmat, qdec, kdec, gn_g.reshape(1, RET_W), gn_b.reshape(1, RET_W))


def _sort_key(x):
    bits = pltpu.bitcast(x, I32)
    return bits ^ ((bits >> 31) & 0x7FFFFFFF)


def _index_scores(iq, w_all, ik2, lane):
    acc = None
    for h in range(IDX_HEADS):
        pair = iq[:, (h // 2) * LANES:(h // 2 + 1) * LANES]
        keep = (lane < IDX_DIM) if h % 2 == 0 else (lane >= IDX_DIM)
        qh = jnp.where(keep, pair, jnp.zeros_like(pair))
        s = lax.dot_general(qh, ik2, _NT, preferred_element_type=F32)
        term = jnp.maximum(s, 0.0) * w_all[:, IDX_DIM + h:IDX_DIM + h + 1]
        acc = term if acc is None else acc + term
    return acc


def _kth_largest_key(count_ge, rows, topk):
    def bit_body(bi, t):
        cand = t + lax.shift_left(jnp.int32(1), 31 - bi)
        return jnp.where(count_ge(cand) >= topk, cand, t)
    return lax.fori_loop(0, 32, bit_body, jnp.full((rows, LANES), INT_MIN, I32))


def _tie_limit(count_eq_below, need, rows, pos_bits):
    def bit_body(bi, p):
        cand = p + lax.shift_left(jnp.int32(1), pos_bits - 1 - bi)
        return jnp.where(count_eq_below(cand) <= need, cand, p)
    return lax.fori_loop(0, pos_bits, bit_body, jnp.zeros((rows, LANES), I32))


def _dsa_prompt_kernel(aq_ref, iq_ref, ikw_ref, ik2_ref, ak_ref, av_ref, o_ref, keys_ref, *, tq, kc, topk, pos_bits):
    i = pl.program_id(1)
    q0 = i * tq
    nk = (q0 + tq + kc - 1) // kc
    lane = lax.broadcasted_iota(I32, (tq, LANES), 1)
    qpos = q0 + lax.broadcasted_iota(I32, (tq, kc), 0)
    col = lax.broadcasted_iota(I32, (tq, kc), 1)
    iq = iq_ref[...]
    w_all = ikw_ref[...]

    def score_chunk(c, carry):
        k0 = pl.multiple_of(c * kc, kc)
        s = _index_scores(iq, w_all, ik2_ref[pl.ds(k0, kc), :], lane)
        keys_ref[c] = jnp.where(k0 + col <= qpos, _sort_key(s), INT_MIN)
        return carry
    lax.fori_loop(0, nk, score_chunk, 0)

    def lane_total(a):
        return jnp.broadcast_to(jnp.sum(a, axis=1, keepdims=True), (tq, LANES))

    def count_ge(cand):
        def body(c, a):
            key = keys_ref[c]
            for g in range(kc // LANES):
                a = a + jnp.where(key[:, g * LANES:(g + 1) * LANES] >= cand, 1, 0)
            return a
        return lane_total(lax.fori_loop(0, nk, body, jnp.zeros((tq, LANES), I32)))

    t = _kth_largest_key(count_ge, tq, topk)
    n_ge = count_ge(t)
    need = topk - count_ge(t + 1)

    def count_eq_below(p):
        def body(c, a):
            key = keys_ref[c]
            for g in range(kc // LANES):
                kpos = c * kc + g * LANES + lane
                hit = jnp.where(key[:, g * LANES:(g + 1) * LANES] == t, 1, 0)
                a = a + jnp.where(kpos < p, hit, 0)
            return a
        return lane_total(lax.fori_loop(0, nk, body, jnp.zeros((tq, LANES), I32)))

    has_excess = jnp.max(jnp.where(n_ge > topk, 1, 0)) > 0
    plim = lax.cond(has_excess,
                    lambda: _tie_limit(count_eq_below, need, tq, pos_bits),
                    lambda: jnp.full((tq, LANES), 2 ** pos_bits, I32))
    t_row = t[:, :1]
    p_row = plim[:, :1]

    aq = aq_ref[...]
    scale = ATT_HD ** -0.5

    def attend_chunk(c, carry):
        k0 = pl.multiple_of(c * kc, kc)
        key = keys_ref[c]
        sel = (key > t_row) | ((key == t_row) & (k0 + col < p_row))
        bias = jnp.where(sel & (key != INT_MIN), 0.0, NEG_INF)
        kch = ak_ref[pl.ds(k0, kc), :]
        vch = av_ref[pl.ds(k0, kc), :]
        out = []
        for h in range(ATT_HEADS):
            m, l, acc = carry[h]
            sl = slice(h * LANES, (h + 1) * LANES)
            s = lax.dot_general(aq[:, sl], kch[:, sl], _NT, preferred_element_type=F32) * scale + bias
            m_new = jnp.maximum(m, jnp.max(s, axis=1, keepdims=True))
            m_safe = jnp.where(m_new == NEG_INF, 0.0, m_new)
            p = jnp.exp(s - m_safe)
            alpha = jnp.exp(m - m_safe)
            l_new = alpha * l + jnp.sum(p, axis=1, keepdims=True)
            acc_new = alpha * acc + jnp.dot(p.astype(BF16), vch[:, sl], preferred_element_type=F32)
            out.append((m_new, l_new, acc_new))
        return tuple(out)

    init = tuple((jnp.full((tq, 1), NEG_INF, F32), jnp.zeros((tq, 1), F32), jnp.zeros((tq, LANES), F32))
                 for _ in range(ATT_HEADS))
    fin = lax.fori_loop(0, nk, attend_chunk, init)
    for h in range(ATT_HEADS):
        m, l, acc = fin[h]
        o_ref[:, h * LANES:(h + 1) * LANES] = (acc / l).astype(BF16)


def _dsa_prompt(aq, iq, ikw, ik2, akb, avb, nb, t_len, topk, tq=128, kc=512):
    n = aq.shape[0]
    nq = t_len // tq
    kc = min(kc, t_len)
    qrow = lambda b, i: (b * nq + i, 0)
    brow = lambda b, i: (b, 0)
    kern = functools.partial(_dsa_prompt_kernel, tq=tq, kc=kc, topk=topk, pos_bits=(t_len - 1).bit_length() + 1)
    return pl.pallas_call(
        kern,
        grid=(nb, nq),
        in_specs=[pl.BlockSpec((tq, ATT_W), qrow), pl.BlockSpec((tq, GROUP_W), qrow), pl.BlockSpec((tq, LANES), qrow),
                  pl.BlockSpec((t_len, LANES), brow), pl.BlockSpec((t_len, ATT_W), brow),
                  pl.BlockSpec((t_len, ATT_W), brow)],
        out_specs=pl.BlockSpec((tq, ATT_W), qrow),
        out_shape=jax.ShapeDtypeStruct((n, ATT_W), BF16),
        scratch_shapes=[pltpu.VMEM((t_len // kc, tq, kc), I32)],
        compiler_params=_cparams(("parallel", "arbitrary")),
        name="dsa_prompt",
    )(aq, iq, ikw, ik2, akb, avb)


QPAD = SUBLANES


def _dsa_sample_select_kernel(pt_ref, iq_ref, w_ref, iknew_ref, expand_ref, *rest, pg, n_pages, n_new, topk,
                              pos_bits):
    idx_refs = rest[:pg]
    sel_ref = rest[pg]
    keys_ref = rest[pg + 1]
    p = pl.program_id(1)
    iq = iq_ref[0]
    w = w_ref[0]
    lane = lax.broadcasted_iota(I32, (QPAD, LANES), 1)
    row = lax.broadcasted_iota(I32, (QPAD, LANES), 0)

    def scores(ik):
        s = lax.dot_general(iq, ik.astype(BF16), _NT, preferred_element_type=F32)
        r = jnp.maximum(s, 0.0) * w
        return jnp.sum(r.reshape(QPAD, IDX_HEADS, r.shape[1]), axis=1)

    for j in range(pg):
        keys_ref[p * pg + j] = _sort_key(scores(idx_refs[j][0]))

    @pl.when(p == pl.num_programs(1) - 1)
    def _():
        s_new = scores(iknew_ref[0])
        keys_ref[n_pages] = jnp.where((lane <= row) & (lane < n_new), _sort_key(s_new), INT_MIN)
        nblk = n_pages + 1

        def lane_total(a):
            return jnp.broadcast_to(jnp.sum(a, axis=1, keepdims=True), (QPAD, LANES))

        def count_ge(cand):
            body = lambda c, a: a + jnp.where(keys_ref[c] >= cand, 1, 0)
            return lane_total(lax.fori_loop(0, nblk, body, jnp.zeros((QPAD, LANES), I32)))

        t = _kth_largest_key(count_ge, QPAD, topk)
        need = topk - count_ge(t + 1)

        def count_eq_below(lim):
            def body(c, a):
                hit = jnp.where(keys_ref[c] == t, 1, 0)
                return a + jnp.where(c * LANES + lane < lim, hit, 0)
            return lane_total(lax.fori_loop(0, nblk, body, jnp.zeros((QPAD, LANES), I32)))

        has_excess = jnp.max(jnp.where(count_ge(t) > topk, 1, 0)) > 0
        plim = lax.cond(has_excess,
                        lambda: _tie_limit(count_eq_below, need, QPAD, pos_bits),
                        lambda: jnp.full((QPAD, LANES), 2 ** pos_bits, I32))

        expand = expand_ref[...]

        def emit(c, carry):
            key = keys_ref[c]
            sel = (key > t) | ((key == t) & (c * LANES + lane < plim))
            sel = jnp.where(sel & (key != INT_MIN), 1.0, 0.0).astype(BF16)
            sel_ref[0, c] = jnp.dot(sel, expand, preferred_element_type=F32)
            return carry
        lax.fori_loop(0, nblk, emit, 0)


def _dsa_sample_select(pt_flat, iq_rows, w_rows, ik_new, cache_idx_k, nb, n_pages, n_new, topk, pg):
    kern = functools.partial(_dsa_sample_select_kernel, pg=pg, n_pages=n_pages, n_new=n_new, topk=topk,
                             pos_bits=((n_pages + 1) * LANES).bit_length())
    per_b = lambda shape: pl.BlockSpec((1,) + shape, lambda b, p, pt: (b,) + (0,) * len(shape))
    page = lambda j: pl.BlockSpec((1, PAGE_SIZE, IDX_DIM), lambda b, p, pt: (pt[b * n_pages + p * pg + j], 0, 0))
    cols = PAGE_SIZE * ATT_HEADS
    expand = (jnp.arange(cols)[None, :] // ATT_HEADS == jnp.arange(PAGE_SIZE)[:, None]).astype(BF16)
    return pl.pallas_call(
        kern,
        grid_spec=pltpu.PrefetchScalarGridSpec(
            num_scalar_prefetch=1,
            grid=(nb, n_pages // pg),
            in_specs=[per_b((QPAD * IDX_HEADS, IDX_DIM)), per_b((QPAD * IDX_HEADS, 1)), per_b((LANES, IDX_DIM)),
                      pl.BlockSpec((PAGE_SIZE, cols), lambda b, p, pt: (0, 0))]
                     + [page(j) for j in range(pg)],
            out_specs=per_b((n_pages + 1, QPAD, cols)),
            scratch_shapes=[pltpu.VMEM((n_pages + 1, QPAD, LANES), I32)]),
        out_shape=jax.ShapeDtypeStruct((nb, n_pages + 1, QPAD, cols), F32),
        compiler_params=_cparams(("parallel", "arbitrary")),
        name="dsa_sample_select",
    )(pt_flat, iq_rows, w_rows, ik_new, expand, *([cache_idx_k] * pg))


def _dsa_sample_attend_kernel(pt_ref, q_ref, sel_ref, knew_ref, vnew_ref, *rest, pg, n_pages):
    k_refs = rest[:pg]
    v_refs = rest[pg:2 * pg]
    o_ref = rest[2 * pg]
    m_ref, l_ref, acc_ref = rest[2 * pg + 1:]
    p = pl.program_id(1)
    scale = ATT_HD ** -0.5
    rows = ATT_HEADS * QPAD
    cols = PAGE_SIZE * ATT_HEADS

    @pl.when(p == 0)
    def _():
        m_ref[...] = jnp.full(m_ref.shape, NEG_INF, F32)
        l_ref[...] = jnp.zeros(l_ref.shape, F32)
        acc_ref[...] = jnp.zeros(acc_ref.shape, F32)

    q = q_ref[0].astype(BF16)
    own_head = ((lax.broadcasted_iota(I32, (rows, cols), 1) & (ATT_HEADS - 1))
                == (lax.broadcasted_iota(I32, (rows, cols), 0) >> (QPAD.bit_length() - 1)))

    def update(blocks):
        scores = []
        for flags, kp, _ in blocks:
            s = lax.dot_general(q, kp.astype(BF16), _NT, preferred_element_type=F32) * scale
            mask = own_head & (jnp.concatenate([flags] * ATT_HEADS, axis=0) > 0.5)
            scores.append(jnp.where(mask, s, NEG_INF))
        mc = scores[0]
        for s in scores[1:]:
            mc = jnp.maximum(mc, s)
        m = m_ref[...]
        m_new = jnp.maximum(m, jnp.max(mc, axis=1, keepdims=True))
        m_safe = jnp.where(m_new == NEG_INF, 0.0, m_new)
        m_row = m_safe[:, :1]
        psum = jnp.zeros((rows, cols), F32)
        acc = jnp.exp(m - m_safe) * acc_ref[...]
        for (_, _, vp), s in zip(blocks, scores):
            pr = jnp.exp(s - m_row)
            psum = psum + pr
            acc = acc + jnp.dot(pr.astype(BF16), vp.astype(BF16), preferred_element_type=F32)
        l_ref[...] = jnp.exp(m - m_safe) * l_ref[...] + jnp.sum(psum, axis=1, keepdims=True)
        acc_ref[...] = acc
        m_ref[...] = m_new

    update([(sel_ref[0, p * pg + j], k_refs[j][0], v_refs[j][0]) for j in range(pg)])

    @pl.when(p == pl.num_programs(1) - 1)
    def _():
        update([(sel_ref[0, n_pages], knew_ref[0], vnew_ref[0])])
        o_ref[0] = acc_ref[...] / l_ref[...]


def _dsa_sample_attend(pt_flat, q_rows, sel, k_new, v_new, cache_k, cache_v, nb, n_pages, pg):
    kern = functools.partial(_dsa_sample_attend_kernel, pg=pg, n_pages=n_pages)
    rows = ATT_HEADS * QPAD
    cols = PAGE_SIZE * ATT_HEADS
    per_b = lambda shape: pl.BlockSpec((1,) + shape, lambda b, p, pt: (b,) + (0,) * len(shape))
    page = lambda j: pl.BlockSpec((1, cols, ATT_HD), lambda b, p, pt: (pt[b * n_pages + p * pg + j], 0, 0))
    return pl.pallas_call(
        kern,
        grid_spec=pltpu.PrefetchScalarGridSpec(
            num_scalar_prefetch=1,
            grid=(nb, n_pages // pg),
            in_specs=[per_b((rows, ATT_HD)), per_b((n_pages + 1, QPAD, cols)), per_b((cols, ATT_HD)),
                      per_b((cols, ATT_HD))] + [page(j) for j in range(pg)] * 2,
            out_specs=per_b((rows, ATT_HD)),
            scratch_shapes=[pltpu.VMEM((rows, LANES), F32)] * 3),
        out_shape=jax.ShapeDtypeStruct((nb, rows, ATT_HD), F32),
        compiler_params=_cparams(("parallel", "arbitrary")),
        name="dsa_sample_attend",
    )(pt_flat, q_rows, sel, k_new, v_new, *([cache_k] * pg), *([cache_v] * pg))


def _layer_norm(x, g, b):
    mu = jnp.mean(x, axis=-1, keepdims=True)
    var = jnp.mean(jnp.square(x - mu), axis=-1, keepdims=True)
    return (x - mu) * lax.rsqrt(var + LN_EPS) * g + b


def _matmul_kernel(x_ref, w_ref, o_ref):
    o_ref[...] = jnp.dot(x_ref[...].astype(BF16), w_ref[...], preferred_element_type=F32)


def _matmul(x, w_bf16, tm):
    n, kdim = x.shape
    m = w_bf16.shape[1]
    return pl.pallas_call(
        _matmul_kernel,
        grid=(n // tm,),
        in_specs=[pl.BlockSpec((tm, kdim), lambda i: (i, 0)), pl.BlockSpec((kdim, m), lambda i: (0, 0))],
        out_specs=pl.BlockSpec((tm, m), lambda i: (i, 0)),
        out_shape=jax.ShapeDtypeStruct((n, m), F32),
        compiler_params=_cparams(("parallel",)),
        name="mem_proj",
    )(x, w_bf16)


def _tail1_kernel(x_ref, ry_ref, ao_ref, mk_ref, mv_ref, wout_ref, wcq_ref, wco_ref,
                  ln1g_ref, ln1b_ref, ln2g_ref, ln2b_ref, h2_ref):
    x = x_ref[0]
    mix = jnp.dot(ry_ref[0].astype(BF16), wout_ref[:RET_W, :], preferred_element_type=F32)
    mix = mix + jnp.dot(ao_ref[0].astype(BF16), wout_ref[RET_W:, :], preferred_element_type=F32)
    h1 = _layer_norm(DN_ALPHA * x + mix, ln1g_ref[...], ln1b_ref[...])
    q = jnp.dot(h1.astype(BF16), wcq_ref[...], preferred_element_type=F32).astype(BF16)
    heads = []
    for h in range(MEM_HEADS):
        sl = slice(h * MEM_HD, (h + 1) * MEM_HD)
        mk = mk_ref[0, :, sl].astype(BF16)
        mv = mv_ref[0, :, sl].astype(BF16)
        s = lax.dot_general(q[:, sl], mk, _NT, preferred_element_type=F32) * (MEM_HD ** -0.5)
        e = jnp.exp(s - jnp.max(s, axis=-1, keepdims=True))
        p = e / jnp.sum(e, axis=-1, keepdims=True)
        heads.append(jnp.dot(p.astype(BF16), mv, preferred_element_type=F32).astype(BF16))
    ca = jnp.concatenate(heads, axis=1)
    h2 = DN_ALPHA * h1 + jnp.dot(ca, wco_ref[...], preferred_element_type=F32)
    h2_ref[0] = _layer_norm(h2, ln2g_ref[...], ln2b_ref[...])


def _tail1(x, ret_y, att_o, mem_k, mem_v, w_out, w_cq, w_co, ln1, ln2, tm):
    g, tg, d = x.shape
    tok = lambda w: pl.BlockSpec((1, tm, w), lambda b, i: (b, i, 0))
    mem = pl.BlockSpec((1, MEM_TOKENS, d), lambda b, i: (b, 0, 0))
    wsp = lambda r, c: pl.BlockSpec((r, c), lambda b, i: (0, 0))
    vec = wsp(1, d)
    return pl.pallas_call(
        _tail1_kernel,
        grid=(g, tg // tm),
        in_specs=[tok(d), tok(RET_W), tok(ATT_W), mem, mem, wsp(d, d), wsp(d, d), wsp(d, d), vec, vec, vec, vec],
        out_specs=tok(d),
        out_shape=jax.ShapeDtypeStruct((g, tg, d), F32),
        compiler_params=_cparams(("parallel", "arbitrary")),
        name="tail1",
    )(x, ret_y, att_o, mem_k, mem_v, w_out, w_cq, w_co,
      ln1[0].reshape(1, d), ln1[1].reshape(1, d), ln2[0].reshape(1, d), ln2[1].reshape(1, d))


HALF_EXPERTS = PEER_EXPERTS // 2
HALF_BIT = HALF_EXPERTS.bit_length() - 1
ROW_TILE = (SUBLANES, LANES)
SLOTS = PEER_HEADS * PEER_TOPK
HIGH_HALF = -65536


def _pack_expert_table(w):
    bits = lax.bitcast_convert_type(w.astype(BF16), jnp.uint16).astype(jnp.uint32)
    packed = bits[:HALF_EXPERTS] | (bits[HALF_EXPERTS:] << 16)
    return lax.bitcast_convert_type(packed, I32).reshape(HALF_EXPERTS * SUBLANES, LANES)


def _top_rows(s, k, payload=None, order=None):
    rows = s.shape[0]
    rid = lax.broadcasted_iota(I32, s.shape, 0)
    order = rid if order is None else order
    big = jnp.int32(2 ** 30)
    vals, pays = [], []
    for _ in range(k):
        m = jnp.max(s, axis=0, keepdims=True)
        win = jnp.min(jnp.where(s == m, order, big), axis=0, keepdims=True)
        hit = order == win
        vals.append(m)
        pays.append(win if payload is None else jnp.max(jnp.where(hit, payload, -1), axis=0, keepdims=True))
        s = jnp.where(hit, NEG_INF, s)
    return jnp.concatenate(vals, axis=0), jnp.concatenate(pays, axis=0)


def _peer_route_kernel(h_ref, wpq_ref, ka_ref, kb_ref, row_ref, shift_ref, gate_ref):
    hb = h_ref[...].astype(BF16)
    tm = hb.shape[0]
    ka = ka_ref[...]
    kb = kb_ref[...]
    a8 = lax.broadcasted_iota(I32, (SUBLANES, tm), 0)
    b16 = lax.broadcasted_iota(I32, (PEER_TOPK, tm), 0)
    for hd in range(PEER_HEADS):
        q = jnp.dot(hb, wpq_ref[:, hd * PEER_QDIM:(hd + 1) * PEER_QDIM], preferred_element_type=F32).astype(BF16)
        s1 = lax.dot_general(ka, q[:, :PEER_HALF], _NT, preferred_element_type=F32)
        s2 = lax.dot_general(kb, q[:, PEER_HALF:], _NT, preferred_element_type=F32)
        v1, i1 = _top_rows(s1, PEER_TOPK)
        v2, i2 = _top_rows(s2, PEER_TOPK)
        cand, cid, fid = [], [], []

        def block(vals, ids, flat, valid):
            cand.append(jnp.where(valid, vals, NEG_INF) if valid is not None else vals)
            cid.append(ids)
            fid.append(flat)

        block(v1[0:1] + v2, i1[0:1] * PEER_NKEYS + i2, b16, None)
        block(v1[1:2] + v2[:8], i1[1:2] * PEER_NKEYS + i2[:8], PEER_TOPK + a8, None)
        block(v1[8:] + v2[0:1], i1[8:] * PEER_NKEYS + i2[0:1], (a8 + 8) * PEER_TOPK, None)
        for b in range(5):
            lim = PEER_TOPK // (b + 1)
            block(v1[:8] + v2[b:b + 1], i1[:8] * PEER_NKEYS + i2[b:b + 1], a8 * PEER_TOPK + b,
                  (a8 >= 2) & (a8 < lim))
        sc, eid = _top_rows(jnp.concatenate(cand, axis=0), PEER_TOPK,
                            payload=jnp.concatenate(cid, axis=0), order=jnp.concatenate(fid, axis=0))
        e = jnp.exp(sc - sc[0:1])
        rows = slice(hd * PEER_TOPK, (hd + 1) * PEER_TOPK)
        gate_ref[rows, :] = e / jnp.sum(e, axis=0, keepdims=True)
        row_ref[rows, :] = (eid & (HALF_EXPERTS - 1)) * SUBLANES
        shift_ref[rows, :] = (16 - ((eid >> HALF_BIT) << 4)).astype(F32)


def _peer_route(h2d, w_pq, keys_a, keys_b, tm):
    n, d = h2d.shape
    slot_blk = pl.BlockSpec((SLOTS, tm), lambda i: (0, i))
    return pl.pallas_call(
        _peer_route_kernel,
        grid=(n // tm,),
        in_specs=[pl.BlockSpec((tm, d), lambda i: (i, 0)),
                  pl.BlockSpec((d, PEER_HEADS * PEER_QDIM), lambda i: (0, 0)),
                  pl.BlockSpec((PEER_NKEYS, PEER_HALF), lambda i: (0, 0)),
                  pl.BlockSpec((PEER_NKEYS, PEER_HALF), lambda i: (0, 0))],
        out_specs=[slot_blk, slot_blk, slot_blk],
        out_shape=[jax.ShapeDtypeStruct((SLOTS, n), I32), jax.ShapeDtypeStruct((SLOTS, n), F32),
                   jax.ShapeDtypeStruct((SLOTS, n), F32)],
        compiler_params=_cparams(("parallel",)),
        name="peer_route",
    )(h2d, w_pq, keys_a, keys_b)


def _token_column(ref, n, lane):
    col = jnp.sum(jnp.where(lane == n, ref[...], 0.0), axis=1, keepdims=True)
    return jnp.broadcast_to(col, (SLOTS, LANES))


def _expert_row(tab_ref, row, shift_row):
    word = tab_ref[pl.ds(pl.multiple_of(row, SUBLANES), SUBLANES), :]
    return pltpu.bitcast(lax.shift_left(word, jnp.broadcast_to(shift_row, ROW_TILE)) & jnp.int32(HIGH_HALF), F32)


def _fold_sublanes(tiles, sub):
    step = SUBLANES // 2
    while len(tiles) > 1:
        half = len(tiles) // 2
        low = (sub & step) == 0
        nxt = []
        for i in range(half):
            a, b = tiles[i], tiles[i + half]
            if 2 * step == SUBLANES:
                folded = jnp.where(low, a, b) + pltpu.roll(jnp.where(low, b, a), step, 0)
            else:
                folded = jnp.where(low, a + pltpu.roll(a, SUBLANES - step, 0), b + pltpu.roll(b, step, 0))
            nxt.append(folded)
        tiles = nxt
        step //= 2
    return tiles[0]


def _peer_hidden_kernel(row_ref, x_ref, shift_ref, gate_ref, tab_ref, coef_ref, shv_ref, part_ref, hid_ref):
    tm = x_ref.shape[0]
    lane = lax.broadcasted_iota(I32, (SLOTS, tm), 1)
    sub = lax.broadcasted_iota(I32, ROW_TILE, 0)

    def shifts(n):
        return _token_column(shift_ref, n, lane).astype(I32)

    def gather(n, buf):
        x = x_ref[n]
        base = n * SLOTS
        for g in range(SLOTS // SUBLANES):
            prods = []
            for r in range(SUBLANES):
                j = g * SUBLANES + r
                prods.append(_expert_row(tab_ref, row_ref[base + j], shv_ref[buf, j:j + 1, :]) * x)
            part_ref[n, g * SUBLANES:(g + 1) * SUBLANES, :] = _fold_sublanes(prods, sub)

    shv_ref[0] = shifts(0)

    def token_pair(i, carry):
        n = 2 * i
        nxt = shifts(n + 1)
        gather(n, 0)
        shv_ref[1] = nxt
        nxt = shifts(n + 2)
        gather(n + 1, 1)
        shv_ref[0] = nxt
        return carry
    lax.fori_loop(0, tm // 2, token_pair, 0)

    hid_ref[...] = jnp.zeros((SLOTS, tm), F32)

    def reduce_tokens(i, carry):
        hid = hid_ref[...]
        for r in range(SUBLANES):
            n = i * SUBLANES + r
            hid = jnp.where(lane == n, jnp.sum(part_ref[n], axis=1, keepdims=True), hid)
        hid_ref[...] = hid
        return carry
    lax.fori_loop(0, tm // SUBLANES, reduce_tokens, 0)
    hid = hid_ref[...]
    gelu = 0.5 * hid * (1.0 + lax.erf(hid * (2.0 ** -0.5)))
    coef_ref[...] = gate_ref[...] * gelu


def _peer_hidden(rows_flat, x_tiles, shift, gate, table, tm):
    n = x_tiles.shape[0]
    slot_blk = pl.BlockSpec((SLOTS, tm), lambda i: (0, i))
    return pl.pallas_call(
        _peer_hidden_kernel,
        grid=(n // tm,),
        in_specs=[pl.BlockSpec((tm * SLOTS,), lambda i: (i,), memory_space=pltpu.SMEM),
                  pl.BlockSpec((tm, *ROW_TILE), lambda i: (i, 0, 0)),
                  slot_blk, slot_blk,
                  pl.BlockSpec(memory_space=pltpu.VMEM)],
        out_specs=slot_blk,
        out_shape=jax.ShapeDtypeStruct((SLOTS, n), F32),
        scratch_shapes=[pltpu.VMEM((2, SLOTS, LANES), I32), pltpu.VMEM((tm, SLOTS, LANES), F32),
                        pltpu.VMEM((SLOTS, tm), F32)],
        compiler_params=_cparams(("arbitrary",)),
        name="peer_hidden",
    )(rows_flat, x_tiles, shift, gate, table)


N_ACC = 4


def _peer_combine_kernel(row_ref, shift_ref, coef_ref, tab_ref, o_ref, shv_ref, cv_ref):
    tm = o_ref.shape[0]
    lane = lax.broadcasted_iota(I32, (SLOTS, tm), 1)

    def columns(n):
        return _token_column(shift_ref, n, lane).astype(I32), _token_column(coef_ref, n, lane)

    def combine(n, buf):
        base = n * SLOTS
        accs = [None] * N_ACC
        for j in range(SLOTS):
            term = jnp.broadcast_to(cv_ref[buf, j:j + 1, :], ROW_TILE) * _expert_row(
                tab_ref, row_ref[base + j], shv_ref[buf, j:j + 1, :])
            accs[j % N_ACC] = term if accs[j % N_ACC] is None else accs[j % N_ACC] + term
        o_ref[n] = (accs[0] + accs[1]) + (accs[2] + accs[3])

    shv_ref[0], cv_ref[0] = columns(0)

    def token_pair(i, carry):
        n = 2 * i
        nxt = columns(n + 1)
        combine(n, 0)
        shv_ref[1], cv_ref[1] = nxt
        nxt = columns(n + 2)
        combine(n + 1, 1)
        shv_ref[0], cv_ref[0] = nxt
        return carry
    lax.fori_loop(0, tm // 2, token_pair, 0)


def _peer_combine(rows_flat, shift, coef, table, n, tm):
    slot_blk = pl.BlockSpec((SLOTS, tm), lambda i: (0, i))
    return pl.pallas_call(
        _peer_combine_kernel,
        grid=(n // tm,),
        in_specs=[pl.BlockSpec((tm * SLOTS,), lambda i: (i,), memory_space=pltpu.SMEM), slot_blk, slot_blk,
                  pl.BlockSpec(memory_space=pltpu.VMEM)],
        out_specs=pl.BlockSpec((tm, *ROW_TILE), lambda i: (i, 0, 0)),
        out_shape=jax.ShapeDtypeStruct((n, *ROW_TILE), F32),
        scratch_shapes=[pltpu.VMEM((2, SLOTS, LANES), I32), pltpu.VMEM((2, SLOTS, LANES), F32)],
        compiler_params=_cparams(("arbitrary",)),
        name="peer_combine",
    )(rows_flat, shift, coef, table)


def _ln3_kernel(h_ref, p_ref, g_ref, b_ref, y_ref):
    y_ref[...] = _layer_norm(DN_ALPHA * h_ref[...] + p_ref[...], g_ref[...], b_ref[...])


def _ln3(h2d, peer2d, g, b, tm):
    n, d = h2d.shape
    tok = pl.BlockSpec((tm, d), lambda i: (i, 0))
    vec = pl.BlockSpec((1, d), lambda i: (0, 0))
    return pl.pallas_call(
        _ln3_kernel,
        grid=(n // tm,),
        in_specs=[tok, tok, vec, vec],
        out_specs=tok,
        out_shape=jax.ShapeDtypeStruct((n, d), F32),
        compiler_params=_cparams(("parallel",)),
        name="ln3",
    )(h2d, peer2d, g.reshape(1, d), b.reshape(1, d))


def _peer_ffn_ln(h2d, w_pq, keys_a, keys_b, tab_u, tab_v, ln3, tm_route, tm_gather):
    n, d = h2d.shape
    rows, shift, gate = _peer_route(h2d, w_pq, keys_a, keys_b, tm_route)
    rows_flat = rows.T.reshape(-1)
    coef = _peer_hidden(rows_flat, h2d.reshape(n, *ROW_TILE), shift, gate, tab_u, tm_gather)
    out = _peer_combine(rows_flat, shift, coef, tab_v, n, tm_gather)
    return _ln3(h2d, out.reshape(n, d), ln3[0], ln3[1], tm_route)


TOKEN_TILE = 512
ROUTE_TILE = 256
GATHER_TILE = 128
SAMPLE_TPAD = 16
PAGES_PER_STEP = 8


def _pad_axis1(a, rows):
    return jnp.pad(a, ((0, 0), (0, rows - a.shape[1])) + ((0, 0),) * (a.ndim - 2))


def kernel(x_prompt, x_sample, mem_prompt, state_ret, cache_k, cache_v, cache_idx_k, cache_mem_k, cache_mem_v,
           page_table, w_in, ret_gn_g, ret_gn_b, w_out, ln1_g, ln1_b, w_cq, w_ck, w_cv, w_co, ln2_g, ln2_b,
           w_pq, peer_keys_a, peer_keys_b, peer_u, peer_v, ln3_g, ln3_b):
    nb, t_len, d = x_prompt.shape
    db, ds, _ = x_sample.shape
    n_pages = page_table.shape[1]
    past = n_pages * PAGE_SIZE
    bf = lambda w: w.astype(BF16)

    w_in_p = _pad_w_in(w_in)
    w_out_b, w_cq_b, w_co_b, w_pq_b = bf(w_out), bf(w_cq), bf(w_co), bf(w_pq)
    keys_a, keys_b = bf(peer_keys_a), bf(peer_keys_b)
    tab_u, tab_v = _pack_expert_table(peer_u), _pack_expert_table(peer_v)
    ln1, ln2, ln3 = (ln1_g, ln1_b), (ln2_g, ln2_b), (ln3_g, ln3_b)

    def tail(x, ret_y, att_o, mem_k, mem_v, tm):
        h2 = _tail1(x, ret_y, att_o, mem_k, mem_v, w_out_b, w_cq_b, w_co_b, ln1, ln2, tm)
        return h2

    def peer(h2d):
        return _peer_ffn_ln(h2d, w_pq_b, keys_a, keys_b, tab_u, tab_v, ln3, ROUTE_TILE, GATHER_TILE)

    n = nb * t_len
    tabs = _rope_tables(jnp.arange(t_len, dtype=I32))
    rq, rk, rv, rg, aq, ak, akb, av, avb, iq, ikw, ik2 = _project(
        x_prompt.reshape(n, d), w_in_p, tabs, t_len // TOKEN_TILE, TOKEN_TILE)
    ret_y, p_state = _retention(rq, rk, rv, rg, jnp.zeros((nb, RET_HEADS, RET_DK, RET_DV), F32),
                                _retention_tables(RET_CHUNK), ret_gn_g, ret_gn_b, nb, t_len // RET_CHUNK)
    att_o = _dsa_prompt(aq, iq, ikw, ik2, akb, avb, nb, t_len, min(IDX_TOPK_MAX, t_len // 4))
    mem_kv = _matmul(mem_prompt.reshape(nb * MEM_TOKENS, d), jnp.concatenate([bf(w_ck), bf(w_cv)], axis=1), MEM_TOKENS)
    p_mem_k = mem_kv[:, :d].reshape(nb, MEM_TOKENS, d)
    p_mem_v = mem_kv[:, d:].reshape(nb, MEM_TOKENS, d)
    h2 = tail(x_prompt, ret_y.reshape(nb, t_len, RET_W), att_o.reshape(nb, t_len, ATT_W), p_mem_k, p_mem_v, TOKEN_TILE)
    y_prompt = peer(h2.reshape(n, d)).reshape(nb, t_len, d)

    ns = db * ds
    pos_s = jnp.tile(past + jnp.arange(ds, dtype=I32), db)
    sq, sk, sv, sg, saq, sak, _, sav, _, siq, sikw, _ = _project(
        x_sample.reshape(ns, d), w_in_p, _rope_tables(pos_s), 1, ns)
    per_seq = lambda a: a.reshape(db, ds, a.shape[-1])
    chunk = lambda a: _pad_axis1(per_seq(a), RET_CHUNK).reshape(db * RET_CHUNK, a.shape[-1])
    ret_y_s, s_state = _retention(chunk(sq), chunk(sk), chunk(sv), chunk(sg), state_ret,
                                  _retention_tables(ds), ret_gn_g, ret_gn_b, db, 1)
    ret_y_s = ret_y_s.reshape(db, RET_CHUNK, RET_W)[:, :SAMPLE_TPAD]

    pt_flat = page_table.reshape(-1)
    iq_rows = _pad_axis1(per_seq(siq), QPAD).reshape(db, QPAD * IDX_HEADS, IDX_DIM)
    w_rows = _pad_axis1(per_seq(sikw[:, IDX_DIM:IDX_DIM + IDX_HEADS]), QPAD).reshape(db, QPAD * IDX_HEADS, 1)
    ik_new = _pad_axis1(per_seq(sikw[:, :IDX_DIM]), LANES)
    sel = _dsa_sample_select(pt_flat, iq_rows, w_rows, ik_new, cache_idx_k, db, n_pages, ds,
                             min(IDX_TOPK_MAX, (past + ds) // 4), PAGES_PER_STEP)
    page_rows = lambda a: a.reshape(a.shape[0], PAGE_SIZE * ATT_HEADS, ATT_HD)
    new_rows = lambda a: page_rows(_pad_axis1(per_seq(a), PAGE_SIZE))
    q_rows = _pad_axis1(per_seq(saq).astype(F32), QPAD).reshape(db, QPAD, ATT_HEADS, ATT_HD)
    q_rows = q_rows.transpose(0, 2, 1, 3).reshape(db, ATT_HEADS * QPAD, ATT_HD)
    att_s = _dsa_sample_attend(pt_flat, q_rows, sel, new_rows(sak), new_rows(sav),
                               page_rows(cache_k), page_rows(cache_v), db, n_pages, PAGES_PER_STEP)
    att_s = att_s.reshape(db, ATT_HEADS, QPAD, ATT_HD).transpose(0, 2, 1, 3).reshape(db, QPAD, ATT_W)
    h2_s = tail(_pad_axis1(x_sample, SAMPLE_TPAD), ret_y_s, _pad_axis1(att_s, SAMPLE_TPAD),
                cache_mem_k.reshape(db, MEM_TOKENS, d), cache_mem_v.reshape(db, MEM_TOKENS, d), SAMPLE_TPAD)
    y_sample = peer(h2_s[:, :ds].reshape(ns, d)).reshape(db, ds, d)

    heads = lambda a, b_, t_: a.reshape(b_, t_, ATT_HEADS, ATT_HD)
    mem_heads = lambda a: a.reshape(nb, MEM_TOKENS, MEM_HEADS, MEM_HD)
    return (y_prompt, y_sample, p_state, heads(ak, nb, t_len), heads(av, nb, t_len),
            ikw[:, :IDX_DIM].reshape(nb, t_len, IDX_DIM), mem_heads(p_mem_k), mem_heads(p_mem_v),
            s_state, heads(sak, db, ds), heads(sav, db, ds), sikw[:, :IDX_DIM].reshape(db, ds, IDX_DIM))
```

```python
import functools

import jax
import jax.numpy as jnp
from jax import lax
from jax.experimental import pallas as pl
from jax.experimental.pallas import tpu as pltpu

F32 = jnp.float32
BF16 = jnp.bfloat16
I32 = jnp.int32

D_MODEL = 1024
DEPTH = 1
PAST_LEN = 8192
PAGE_SIZE = 128
RET_HEADS = 4
RET_DK = 128
RET_DV = 128
RET_W = RET_HEADS * RET_DV
RET_CHUNK = 128
ATT_HEADS = 4
ATT_HD = 128
ATT_W = ATT_HEADS * ATT_HD
IDX_HEADS = 8
IDX_DIM = 64
IDX_TOPK_MAX = 256
MEM_TOKENS = 256
MEM_HEADS = 4
MEM_HD = D_MODEL // MEM_HEADS
PEER_HEADS = 8
PEER_NKEYS = 128
PEER_EXPERTS = PEER_NKEYS * PEER_NKEYS
PEER_QDIM = 256
PEER_HALF = PEER_QDIM // 2
PEER_TOPK = 16
ROPE_THETA = 10000.0
LN_EPS = 1e-5
GN_EPS = 1e-5
DN_ALPHA = (2 * DEPTH) ** 0.25

LANES = 128
SUBLANES = 8
GROUP_W = 512
N_GROUPS = 8
IN_COLS_PAD = N_GROUPS * GROUP_W + LANES
VMEM_LIMIT = 56 * 1024 * 1024
INT_MIN = -(2 ** 31)
NEG_INF = float("-inf")

_NT = (((1,), (1,)), ((), ()))


def _cparams(sem):
    return pltpu.CompilerParams(dimension_semantics=sem, vmem_limit_bytes=VMEM_LIMIT)


def _rope_tables(pos):
    def tab(d):
        half = d // 2
        inv = ROPE_THETA ** (-jnp.arange(half, dtype=F32) * (2.0 / d))
        ang = pos.astype(F32)[:, None] * inv[None, :]
        cos, sin = jnp.cos(ang), jnp.sin(ang)
        reps = LANES // d
        return (jnp.tile(jnp.concatenate([cos, cos], axis=1), (1, reps)),
                jnp.tile(jnp.concatenate([-sin, sin], axis=1), (1, reps)))
    c128, s128 = tab(ATT_HD)
    c64, s64 = tab(IDX_DIM)
    return c128, s128, c64, s64


def _proj_kernel(x_ref, w_ref, wvt_ref, wwt_ref, c128_ref, s128_ref, c64_ref, s64_ref,
                 rq_ref, rk_ref, rv_ref, rg_ref, aq_ref, ak_ref, akb_ref, av_ref, avt_ref,
                 iq_ref, ikw_ref, ik2_ref, iwt_ref):
    xb = x_ref[...].astype(BF16)
    c128 = c128_ref[...]
    s128 = s128_ref[...]
    c64 = c64_ref[...]
    s64 = s64_ref[...]
    tm = xb.shape[0]
    lane = lax.broadcasted_iota(I32, (tm, LANES), 1)
    first_half64 = (lane & (IDX_DIM // 2)) == 0

    def group(g, width=GROUP_W):
        return jnp.dot(xb, w_ref[:, g * GROUP_W:g * GROUP_W + width], preferred_element_type=F32)

    def rope128(p):
        return p * c128 + pltpu.roll(p, ATT_HD // 2, 1) * s128

    def rope64(p):
        rot = jnp.where(first_half64, pltpu.roll(p, LANES - IDX_DIM // 2, 1), pltpu.roll(p, IDX_DIM // 2, 1))
        return p * c64 + rot * s64

    p = group(0)
    for h in range(RET_HEADS):
        sl = slice(h * LANES, (h + 1) * LANES)
        rq_ref[:, sl] = rope128(p[:, sl]).astype(BF16)
    p = group(1)
    for h in range(RET_HEADS):
        sl = slice(h * LANES, (h + 1) * LANES)
        rk_ref[:, sl] = (rope128(p[:, sl]) * (RET_DK ** -0.5)).astype(BF16)
    rv_ref[...] = group(2).astype(BF16)
    rg_ref[...] = group(3)
    p = group(4)
    for h in range(ATT_HEADS):
        sl = slice(h * LANES, (h + 1) * LANES)
        aq_ref[:, sl] = rope128(p[:, sl]).astype(BF16)
    p = group(5)
    for h in range(ATT_HEADS):
        sl = slice(h * LANES, (h + 1) * LANES)
        r = rope128(p[:, sl])
        ak_ref[:, sl] = r
        akb_ref[:, sl] = r.astype(BF16)
    p = group(6)
    av_ref[...] = p
    avt_ref[0] = lax.dot_general(wvt_ref[...], xb, _NT, preferred_element_type=F32).astype(BF16)
    iwt_ref[...] = lax.dot_general(wwt_ref[...], xb, _NT, preferred_element_type=F32)[:IDX_HEADS]
    p = group(7)
    for j in range(GROUP_W // LANES):
        sl = slice(j * LANES, (j + 1) * LANES)
        iq_ref[:, sl] = rope64(p[:, sl]).astype(BF16)
    p = group(8, LANES)
    is_key = lane < IDX_DIM
    r = jnp.where(is_key, rope64(p), p)
    ikw_ref[...] = r
    ik2_ref[...] = jnp.where(is_key, r, pltpu.roll(r, IDX_DIM, 1)).astype(BF16)


def _project(x2d, w_pad, tables, period_tiles, tm):
    n = x2d.shape[0]
    nt = n // tm
    row = lambda i: (i, 0)
    tab = lambda i: (i % period_tiles, 0)
    const = lambda i: (0, 0)
    flat = lambda w, dt: (pl.BlockSpec((tm, w), row), jax.ShapeDtypeStruct((n, w), dt))
    outs = [
        flat(GROUP_W, BF16), flat(GROUP_W, BF16), flat(GROUP_W, BF16), flat(GROUP_W, F32),
        flat(GROUP_W, BF16), flat(GROUP_W, F32), flat(GROUP_W, BF16), flat(GROUP_W, F32),
        (pl.BlockSpec((1, ATT_W, tm), lambda i: (i, 0, 0)), jax.ShapeDtypeStruct((nt, ATT_W, tm), BF16)),
        flat(GROUP_W, BF16), flat(LANES, F32), flat(LANES, BF16),
        (pl.BlockSpec((IDX_HEADS, tm), lambda i: (0, i)), jax.ShapeDtypeStruct((IDX_HEADS, n), F32)),
    ]
    v_cols = slice(6 * GROUP_W, 7 * GROUP_W)
    w_cols = slice(N_GROUPS * GROUP_W + IDX_DIM, N_GROUPS * GROUP_W + IDX_DIM + 2 * IDX_HEADS)
    return pl.pallas_call(
        _proj_kernel,
        grid=(nt,),
        in_specs=[pl.BlockSpec((tm, D_MODEL), row),
                  pl.BlockSpec((D_MODEL, IN_COLS_PAD), const),
                  pl.BlockSpec((GROUP_W, D_MODEL), const), pl.BlockSpec((2 * IDX_HEADS, D_MODEL), const)]
                 + [pl.BlockSpec((tm, LANES), tab)] * 4,
        out_specs=[spec for spec, _ in outs],
        out_shape=[shape for _, shape in outs],
        compiler_params=_cparams(("parallel",)),
        name="proj_rope",
    )(x2d, w_pad, w_pad[:, v_cols].T, w_pad[:, w_cols].T, *tables)


def _pad_w_in(w_in):
    return jnp.pad(w_in.astype(BF16), ((0, 0), (0, IN_COLS_PAD - w_in.shape[1])))


def _retention_log_decay():
    return jnp.log(1.0 - 2.0 ** (-5.0 - jnp.arange(RET_HEADS, dtype=F32)))


def _retention_tables(chunk_len):
    log_g = _retention_log_decay()
    j = jnp.arange(RET_CHUNK, dtype=F32)
    diff = j[:, None] - j[None, :]
    dmat = jnp.where(diff >= 0, jnp.exp(log_g[:, None, None] * jnp.maximum(diff, 0.0)), 0.0)
    q_dec = jnp.exp(log_g[None, :] * (j[:, None] + 1.0))
    k_dec = jnp.exp(log_g[None, :] * (chunk_len - 1.0 - j[:, None]))
    k_dec = jnp.where(j[:, None] < chunk_len, k_dec, 0.0)
    s_dec = jnp.exp(log_g * chunk_len)
    widen = lambda t: jnp.repeat(t, LANES, axis=1)
    return dmat, widen(q_dec), widen(k_dec), s_dec


def _retention_kernel(sdec_ref, q_ref, k_ref, v_ref, g_ref, s0_ref, dmat_ref, qdec_ref, kdec_ref,
                      gng_ref, gnb_ref, y_ref, sout_ref, state_ref):
    c = pl.program_id(1)

    @pl.when(c == 0)
    def _():
        state_ref[...] = s0_ref[0]

    for h in range(RET_HEADS):
        sl = slice(h * LANES, (h + 1) * LANES)
        q = q_ref[:, sl]
        k = k_ref[:, sl]
        v = v_ref[:, sl]
        s_prev = state_ref[h]
        scores = lax.dot_general(q, k, _NT, preferred_element_type=F32) * dmat_ref[h]
        o = jnp.dot(scores.astype(BF16), v, preferred_element_type=F32)
        o = o + jnp.dot(q, s_prev.astype(BF16), preferred_element_type=F32) * qdec_ref[:, sl]
        kd_t = (k.astype(F32) * kdec_ref[:, sl]).T.astype(BF16)
        state_ref[h] = s_prev * sdec_ref[h] + jnp.dot(kd_t, v, preferred_element_type=F32)
        mu = jnp.mean(o, axis=-1, keepdims=True)
        var = jnp.mean(jnp.square(o - mu), axis=-1, keepdims=True)
        n = (o - mu) * lax.rsqrt(var + GN_EPS) * gng_ref[:, sl] + gnb_ref[:, sl]
        g = g_ref[:, sl]
        y_ref[:, sl] = (g * (1.0 / (1.0 + jnp.exp(-g))) * n).astype(BF16)

    @pl.when(c == pl.num_programs(1) - 1)
    def _():
        sout_ref[0] = state_ref[...]


def _retention(q, k, v, g, s0, tables, gn_g, gn_b, nb, nc):
    dmat, qdec, kdec, sdec = tables
    n = q.shape[0]
    row = lambda b, c: (b * nc + c, 0)
    const2 = lambda b, c: (0, 0)
    blk = pl.BlockSpec((RET_CHUNK, RET_W), row)
    return pl.pallas_call(
        _retention_kernel,
        grid=(nb, nc),
        in_specs=[pl.BlockSpec(memory_space=pltpu.SMEM), blk, blk, blk, blk,
                  pl.BlockSpec((1, RET_HEADS, RET_DK, RET_DV), lambda b, c: (b, 0, 0, 0)),
                  pl.BlockSpec((RET_HEADS, RET_CHUNK, RET_CHUNK), lambda b, c: (0, 0, 0)),
                  pl.BlockSpec((RET_CHUNK, RET_W), const2), pl.BlockSpec((RET_CHUNK, RET_W), const2),
                  pl.BlockSpec((1, RET_W), const2), pl.BlockSpec((1, RET_W), const2)],
        out_specs=[blk, pl.BlockSpec((1, RET_HEADS, RET_DK, RET_DV), lambda b, c: (b, 0, 0, 0))],
        out_shape=[jax.ShapeDtypeStruct((n, RET_W), BF16),
                   jax.ShapeDtypeStruct((nb, RET_HEADS, RET_DK, RET_DV), F32)],
        scratch_shapes=[pltpu.VMEM((RET_HEADS, RET_DK, RET_DV), F32)],
        compiler_params=_cparams(("parallel", "arbitrary")),
        name="retention",
    )(sdec, q, k, v, g, s0, dmat, qdec, kdec, gn_g.reshape(1, RET_W), gn_b.reshape(1, RET_W))


def _sort_key(x):
    bits = pltpu.bitcast(x, I32)
    return bits ^ ((bits >> 31) & 0x7FFFFFFF)


def _kth_largest_key(count_ge, shape, topk):
    def bit_body(bi, t):
        cand = t + lax.shift_left(jnp.int32(1), 31 - bi)
        return jnp.where(count_ge(cand) >= topk, cand, t)
    return lax.fori_loop(0, 32, bit_body, jnp.full(shape, INT_MIN, I32))


def _tie_limit(count_eq_below, need, shape, pos_bits):
    def bit_body(bi, p):
        cand = p + lax.shift_left(jnp.int32(1), pos_bits - 1 - bi)
        return jnp.where(count_eq_below(cand) <= need, cand, p)
    return lax.fori_loop(0, pos_bits, bit_body, jnp.zeros(shape, I32))


def _dsa_prompt_kernel(aq_ref, iq_ref, iwt_ref, ik2_ref, ak_ref, avt_ref, o_ref, keys_ref, bias_ref, *,
                       tq, kc, topk, pos_bits):
    i = pl.program_id(1)
    q0 = i * tq
    nk = (q0 + tq + kc - 1) // kc
    groups = kc // SUBLANES
    state = (SUBLANES, tq)
    lane = lax.broadcasted_iota(I32, (tq, LANES), 1)
    key_in_chunk = (lax.broadcasted_iota(I32, (groups, SUBLANES, tq), 0) * SUBLANES
                    + lax.broadcasted_iota(I32, (groups, SUBLANES, tq), 1))
    qpos = q0 + lax.broadcasted_iota(I32, (groups, SUBLANES, tq), 2)
    iq = iq_ref[...]
    iwt = iwt_ref[...]

    def score_chunk(c, carry):
        k0 = pl.multiple_of(c * kc, kc)
        ikc = ik2_ref[pl.ds(k0, kc), :]
        acc = None
        for h in range(IDX_HEADS):
            pair = iq[:, (h // 2) * LANES:(h // 2 + 1) * LANES]
            keep = (lane < IDX_DIM) if h % 2 == 0 else (lane >= IDX_DIM)
            qh = jnp.where(keep, pair, jnp.zeros_like(pair))
            s = lax.dot_general(ikc, qh, _NT, preferred_element_type=F32)
            term = jnp.maximum(s, 0.0) * iwt[h:h + 1, :]
            acc = term if acc is None else acc + term
        key = _sort_key(acc).reshape(groups, SUBLANES, tq)
        keys_ref[c] = jnp.where(k0 + key_in_chunk <= qpos, key, INT_MIN)
        return carry
    lax.fori_loop(0, nk, score_chunk, 0)

    def total(a):
        return jnp.broadcast_to(jnp.sum(a, axis=0, keepdims=True), state)

    def count_ge(cand):
        body = lambda c, a: a + jnp.sum(jnp.where(keys_ref[c] >= cand[None], 1, 0), axis=0)
        return total(lax.fori_loop(0, nk, body, jnp.zeros(state, I32)))

    t = _kth_largest_key(count_ge, state, topk)
    n_ge = count_ge(t)
    need = topk - count_ge(t + 1)

    def count_eq_below(p):
        def body(c, a):
            hit = (keys_ref[c] == t[None]) & (c * kc + key_in_chunk < p[None])
            return a + jnp.sum(jnp.where(hit, 1, 0), axis=0)
        return total(lax.fori_loop(0, nk, body, jnp.zeros(state, I32)))

    has_excess = jnp.max(jnp.where(n_ge > topk, 1, 0)) > 0
    plim = lax.cond(has_excess,
                    lambda: _tie_limit(count_eq_below, need, state, pos_bits),
                    lambda: jnp.full(state, 2 ** pos_bits, I32))

    def bias_chunk(c, carry):
        key = keys_ref[c]
        sel = (key > t[None]) | ((key == t[None]) & (c * kc + key_in_chunk < plim[None]))
        bias_ref[c] = jnp.where(sel & (key != INT_MIN), 0.0, NEG_INF)
        return carry
    lax.fori_loop(0, nk, bias_chunk, 0)

    scale = ATT_HD ** -0.5
    head_cols = [slice(h * LANES, (h + 1) * LANES) for h in range(ATT_HEADS)]
    queries = [aq_ref[:, sl] for sl in head_cols]

    def attend_chunk(c, carry):
        k0 = pl.multiple_of(c * kc, kc)
        bias = bias_ref[c].reshape(kc, tq)
        out = []
        for h, sl in enumerate(head_cols):
            m, l, acc = carry[h]
            s = lax.dot_general(ak_ref[pl.ds(k0, kc), sl], queries[h], _NT, preferred_element_type=F32)
            s = s * scale + bias
            m_new = jnp.maximum(m, jnp.max(s, axis=0, keepdims=True))
            m_safe = jnp.where(m_new == NEG_INF, 0.0, m_new)
            p = jnp.exp(s - m_safe)
            alpha = jnp.exp(m - m_safe)
            l_new = alpha * l + jnp.sum(p, axis=0, keepdims=True)
            acc_new = alpha * acc + jnp.dot(avt_ref[c, sl, :], p.astype(BF16), preferred_element_type=F32)
            out.append((m_new, l_new, acc_new))
        return tuple(out)

    init = tuple((jnp.full((1, tq), NEG_INF, F32), jnp.zeros((1, tq), F32), jnp.zeros((ATT_HD, tq), F32))
                 for _ in head_cols)
    fin = lax.fori_loop(0, nk, attend_chunk, init)
    for (m, l, acc), sl in zip(fin, head_cols):
        o_ref[:, sl] = (acc / l).T.astype(BF16)


def _dsa_prompt(aq, iq, iwt, ik2, akb, avt, nb, t_len, topk, tq=LANES, kc=512):
    n = aq.shape[0]
    nq = t_len // tq
    qrow = lambda b, i: (b * nq + i, 0)
    brow = lambda b, i: (b, 0)
    kern = functools.partial(_dsa_prompt_kernel, tq=tq, kc=kc, topk=topk, pos_bits=(t_len - 1).bit_length() + 1)
    chunked = (t_len // kc, kc // SUBLANES, SUBLANES, tq)
    return pl.pallas_call(
        kern,
        grid=(nb, nq),
        in_specs=[pl.BlockSpec((tq, ATT_W), qrow), pl.BlockSpec((tq, GROUP_W), qrow),
                  pl.BlockSpec((IDX_HEADS, tq), lambda b, i: (0, b * nq + i)),
                  pl.BlockSpec((t_len, LANES), brow), pl.BlockSpec((t_len, ATT_W), brow),
                  pl.BlockSpec((t_len // kc, ATT_W, kc), lambda b, i: (b, 0, 0))],
        out_specs=pl.BlockSpec((tq, ATT_W), qrow),
        out_shape=jax.ShapeDtypeStruct((n, ATT_W), BF16),
        scratch_shapes=[pltpu.VMEM(chunked, I32), pltpu.VMEM(chunked, F32)],
        compiler_params=_cparams(("parallel", "arbitrary")),
        name="dsa_prompt",
    )(aq, iq, iwt, ik2, akb, avt)


QPAD = SUBLANES


def _dsa_sample_select_kernel(pt_ref, iq_ref, w_ref, iknew_ref, expand_ref, *rest, pg, n_pages, n_new, topk,
                              pos_bits):
    idx_refs = rest[:pg]
    sel_ref = rest[pg]
    keys_ref = rest[pg + 1]
    p = pl.program_id(1)
    iq = iq_ref[0]
    w = w_ref[0]
    lane = lax.broadcasted_iota(I32, (QPAD, LANES), 1)
    row = lax.broadcasted_iota(I32, (QPAD, LANES), 0)
    key_in_block = 2 * (lane & (IDX_DIM - 1)) + (lane >> (IDX_DIM.bit_length() - 1))
    half_lane = lax.broadcasted_iota(I32, (PAGE_SIZE // 2, LANES), 1) < IDX_DIM

    def scores(pairs):
        pb = pairs.astype(BF16)
        zero = jnp.zeros_like(pb)
        keys = jnp.concatenate([jnp.where(half_lane, pb, zero), jnp.where(half_lane, zero, pb)], axis=0)
        s = lax.dot_general(iq, keys, _NT, preferred_element_type=F32)
        r = jnp.maximum(s, 0.0) * w
        return jnp.sum(r.reshape(QPAD, IDX_HEADS, LANES), axis=1)

    for j in range(pg):
        keys_ref[p * pg + j] = _sort_key(scores(idx_refs[j][0]))

    @pl.when(p == pl.num_programs(1) - 1)
    def _():
        s_new = scores(iknew_ref[0])
        keys_ref[n_pages] = jnp.where((key_in_block <= row) & (key_in_block < n_new), _sort_key(s_new), INT_MIN)
        keys_ref[n_pages + 1] = jnp.full((QPAD, LANES), INT_MIN, I32)
        keys = keys_ref[...]
        pos = lax.broadcasted_iota(I32, keys.shape, 0) * LANES + key_in_block[None]

        def lane_total(a):
            return jnp.broadcast_to(jnp.sum(a, axis=1, keepdims=True), (QPAD, LANES))

        def count_ge(cand):
            return lane_total(jnp.sum(jnp.where(keys >= cand[None], 1, 0), axis=0))

        t = _kth_largest_key(count_ge, (QPAD, LANES), topk)
        need = topk - count_ge(t + 1)

        def count_eq_below(lim):
            return lane_total(jnp.sum(jnp.where((keys == t[None]) & (pos < lim[None]), 1, 0), axis=0))

        has_excess = jnp.max(jnp.where(count_ge(t) > topk, 1, 0)) > 0
        plim = lax.cond(has_excess,
                        lambda: _tie_limit(count_eq_below, need, (QPAD, LANES), pos_bits),
                        lambda: jnp.full((QPAD, LANES), 2 ** pos_bits, I32))

        sel = ((keys > t[None]) | ((keys == t[None]) & (pos < plim[None]))) & (keys != INT_MIN)
        flags = jnp.where(sel, 1.0, 0.0).reshape(keys.shape[0] * QPAD, LANES).astype(BF16)
        out = jnp.dot(flags, expand_ref[...], preferred_element_type=F32)
        sel_ref[0] = out.reshape(keys.shape[0], QPAD, out.shape[1])


def _dsa_sample_select(pt_flat, iq_rows, w_rows, ik_new, cache_idx_k, nb, n_pages, n_new, topk, pg):
    kern = functools.partial(_dsa_sample_select_kernel, pg=pg, n_pages=n_pages, n_new=n_new, topk=topk,
                             pos_bits=((n_pages + 1) * LANES).bit_length())
    per_b = lambda shape: pl.BlockSpec((1,) + shape, lambda b, p, pt: (b,) + (0,) * len(shape))
    pair_rows = lambda a: a.reshape(a.shape[0], PAGE_SIZE // 2, 2 * IDX_DIM)
    page = lambda j: pl.BlockSpec((1, PAGE_SIZE // 2, LANES), lambda b, p, pt: (pt[b * n_pages + p * pg + j], 0, 0))
    cols = PAGE_SIZE * ATT_HEADS
    lane = jnp.arange(LANES)
    key_of_lane = 2 * (lane % IDX_DIM) + lane // IDX_DIM
    expand = (jnp.arange(cols)[None, :] // ATT_HEADS == key_of_lane[:, None]).astype(BF16)
    nblk = n_pages + 2
    return pl.pallas_call(
        kern,
        grid_spec=pltpu.PrefetchScalarGridSpec(
            num_scalar_prefetch=1,
            grid=(nb, n_pages // pg),
            in_specs=[per_b((QPAD * IDX_HEADS, LANES)), per_b((QPAD * IDX_HEADS, 1)), per_b((PAGE_SIZE // 2, LANES)),
                      pl.BlockSpec((LANES, cols), lambda b, p, pt: (0, 0))]
                     + [page(j) for j in range(pg)],
            out_specs=per_b((nblk, QPAD, cols)),
            scratch_shapes=[pltpu.VMEM((nblk, QPAD, LANES), I32)]),
        out_shape=jax.ShapeDtypeStruct((nb, nblk, QPAD, cols), F32),
        compiler_params=_cparams(("parallel", "arbitrary")),
        name="dsa_sample_select",
    )(pt_flat, jnp.concatenate([iq_rows, iq_rows], axis=-1), w_rows, pair_rows(ik_new), expand,
      *([pair_rows(cache_idx_k)] * pg))


def _dsa_sample_attend_kernel(pt_ref, q_ref, sel_ref, knew_ref, vnew_ref, *rest, pg, n_pages):
    k_refs = rest[:pg]
    v_refs = rest[pg:2 * pg]
    o_ref = rest[2 * pg]
    m_ref, l_ref, acc_ref = rest[2 * pg + 1:]
    p = pl.program_id(1)
    scale = ATT_HD ** -0.5
    rows = ATT_HEADS * QPAD
    cols = PAGE_SIZE * ATT_HEADS

    @pl.when(p == 0)
    def _():
        m_ref[...] = jnp.full(m_ref.shape, NEG_INF, F32)
        l_ref[...] = jnp.zeros(l_ref.shape, F32)
        acc_ref[...] = jnp.zeros(acc_ref.shape, F32)

    q = q_ref[0].astype(BF16)
    own_head = ((lax.broadcasted_iota(I32, (rows, cols), 1) & (ATT_HEADS - 1))
                == (lax.broadcasted_iota(I32, (rows, cols), 0) >> (QPAD.bit_length() - 1)))

    def update(blocks):
        scores = []
        for flags, kp, _ in blocks:
            s = lax.dot_general(q, kp.astype(BF16), _NT, preferred_element_type=F32) * scale
            mask = own_head & (jnp.concatenate([flags] * ATT_HEADS, axis=0) > 0.5)
            scores.append(jnp.where(mask, s, NEG_INF))
        mc = scores[0]
        for s in scores[1:]:
            mc = jnp.maximum(mc, s)
        m = m_ref[...]
        m_new = jnp.maximum(m, jnp.max(mc, axis=1, keepdims=True))
        m_safe = jnp.where(m_new == NEG_INF, 0.0, m_new)
        m_row = m_safe[:, :1]
        psum = jnp.zeros((rows, cols), F32)
        acc = jnp.exp(m - m_safe) * acc_ref[...]
        for (_, _, vp), s in zip(blocks, scores):
            pr = jnp.exp(s - m_row)
            psum = psum + pr
            acc = acc + jnp.dot(pr.astype(BF16), vp.astype(BF16), preferred_element_type=F32)
        l_ref[...] = jnp.exp(m - m_safe) * l_ref[...] + jnp.sum(psum, axis=1, keepdims=True)
        acc_ref[...] = acc
        m_ref[...] = m_new

    update([(sel_ref[0, p * pg + j], k_refs[j][0], v_refs[j][0]) for j in range(pg)])

    @pl.when(p == pl.num_programs(1) - 1)
    def _():
        update([(sel_ref[0, n_pages], knew_ref[0], vnew_ref[0])])
        o_ref[0] = acc_ref[...] / l_ref[...]


def _dsa_sample_attend(pt_flat, q_rows, sel, k_new, v_new, cache_k, cache_v, nb, n_pages, pg):
    kern = functools.partial(_dsa_sample_attend_kernel, pg=pg, n_pages=n_pages)
    rows = ATT_HEADS * QPAD
    cols = PAGE_SIZE * ATT_HEADS
    per_b = lambda shape: pl.BlockSpec((1,) + shape, lambda b, p, pt: (b,) + (0,) * len(shape))
    page = lambda j: pl.BlockSpec((1, cols, ATT_HD), lambda b, p, pt: (pt[b * n_pages + p * pg + j], 0, 0))
    return pl.pallas_call(
        kern,
        grid_spec=pltpu.PrefetchScalarGridSpec(
            num_scalar_prefetch=1,
            grid=(nb, n_pages // pg),
            in_specs=[per_b((rows, ATT_HD)), per_b((n_pages + 2, QPAD, cols)), per_b((cols, ATT_HD)),
                      per_b((cols, ATT_HD))] + [page(j) for j in range(pg)] * 2,
            out_specs=per_b((rows, ATT_HD)),
            scratch_shapes=[pltpu.VMEM((rows, LANES), F32)] * 3),
        out_shape=jax.ShapeDtypeStruct((nb, rows, ATT_HD), F32),
        compiler_params=_cparams(("parallel", "arbitrary")),
        name="dsa_sample_attend",
    )(pt_flat, q_rows, sel, k_new, v_new, *([cache_k] * pg), *([cache_v] * pg))


def _layer_norm(x, g, b):
    mu = jnp.mean(x, axis=-1, keepdims=True)
    var = jnp.mean(jnp.square(x - mu), axis=-1, keepdims=True)
    return (x - mu) * lax.rsqrt(var + LN_EPS) * g + b


def _matmul_kernel(x_ref, w_ref, o_ref):
    o_ref[...] = jnp.dot(x_ref[...].astype(BF16), w_ref[...], preferred_element_type=F32)


def _matmul(x, w_bf16, tm):
    n, kdim = x.shape
    m = w_bf16.shape[1]
    return pl.pallas_call(
        _matmul_kernel,
        grid=(n // tm,),
        in_specs=[pl.BlockSpec((tm, kdim), lambda i: (i, 0)), pl.BlockSpec((kdim, m), lambda i: (0, 0))],
        out_specs=pl.BlockSpec((tm, m), lambda i: (i, 0)),
        out_shape=jax.ShapeDtypeStruct((n, m), F32),
        compiler_params=_cparams(("parallel",)),
        name="mem_proj",
    )(x, w_bf16)


def _tail1_kernel(x_ref, ry_ref, ao_ref, mk_ref, mv_ref, wout_ref, wcq_ref, wco_ref,
                  ln1g_ref, ln1b_ref, ln2g_ref, ln2b_ref, h2_ref):
    x = x_ref[0]
    mix = jnp.dot(ry_ref[0].astype(BF16), wout_ref[:RET_W, :], preferred_element_type=F32)
    mix = mix + jnp.dot(ao_ref[0].astype(BF16), wout_ref[RET_W:, :], preferred_element_type=F32)
    h1 = _layer_norm(DN_ALPHA * x + mix, ln1g_ref[...], ln1b_ref[...])
    q = jnp.dot(h1.astype(BF16), wcq_ref[...], preferred_element_type=F32).astype(BF16)
    heads = []
    for h in range(MEM_HEADS):
        sl = slice(h * MEM_HD, (h + 1) * MEM_HD)
        mk = mk_ref[0, :, sl].astype(BF16)
        mv = mv_ref[0, :, sl].astype(BF16)
        s = lax.dot_general(q[:, sl], mk, _NT, preferred_element_type=F32) * (MEM_HD ** -0.5)
        e = jnp.exp(s - jnp.max(s, axis=-1, keepdims=True))
        p = e / jnp.sum(e, axis=-1, keepdims=True)
        heads.append(jnp.dot(p.astype(BF16), mv, preferred_element_type=F32).astype(BF16))
    ca = jnp.concatenate(heads, axis=1)
    h2 = DN_ALPHA * h1 + jnp.dot(ca, wco_ref[...], preferred_element_type=F32)
    h2_ref[0] = _layer_norm(h2, ln2g_ref[...], ln2b_ref[...])


def _tail1(x, ret_y, att_o, mem_k, mem_v, w_out, w_cq, w_co, ln1, ln2, tm):
    g, tg, d = x.shape
    tok = lambda w: pl.BlockSpec((1, tm, w), lambda b, i: (b, i, 0))
    mem = pl.BlockSpec((1, MEM_TOKENS, d), lambda b, i: (b, 0, 0))
    wsp = lambda r, c: pl.BlockSpec((r, c), lambda b, i: (0, 0))
    vec = wsp(1, d)
    return pl.pallas_call(
        _tail1_kernel,
        grid=(g, tg // tm),
        in_specs=[tok(d), tok(RET_W), tok(ATT_W), mem, mem, wsp(d, d), wsp(d, d), wsp(d, d), vec, vec, vec, vec],
        out_specs=tok(d),
        out_shape=jax.ShapeDtypeStruct((g, tg, d), F32),
        compiler_params=_cparams(("parallel", "arbitrary")),
        name="tail1",
    )(x, ret_y, att_o, mem_k, mem_v, w_out, w_cq, w_co,
      ln1[0].reshape(1, d), ln1[1].reshape(1, d), ln2[0].reshape(1, d), ln2[1].reshape(1, d))


HALF_EXPERTS = PEER_EXPERTS // 2
HALF_BIT = HALF_EXPERTS.bit_length() - 1
ROW_TILE = (SUBLANES, LANES)
SLOTS = PEER_HEADS * PEER_TOPK
HIGH_HALF = -65536


def _pack_expert_table(w):
    bits = lax.bitcast_convert_type(w.astype(BF16), jnp.uint16).astype(jnp.uint32)
    packed = bits[:HALF_EXPERTS] | (bits[HALF_EXPERTS:] << 16)
    return lax.bitcast_convert_type(packed, I32).reshape(HALF_EXPERTS * SUBLANES, LANES)


def _top_rows(s, k, payload=None, order=None):
    rows = s.shape[0]
    rid = lax.broadcasted_iota(I32, s.shape, 0)
    order = rid if order is None else order
    big = jnp.int32(2 ** 30)
    vals, pays = [], []
    for _ in range(k):
        m = jnp.max(s, axis=0, keepdims=True)
        win = jnp.min(jnp.where(s == m, order, big), axis=0, keepdims=True)
        hit = order == win
        vals.append(m)
        pays.append(win if payload is None else jnp.max(jnp.where(hit, payload, -1), axis=0, keepdims=True))
        s = jnp.where(hit, NEG_INF, s)
    return jnp.concatenate(vals, axis=0), jnp.concatenate(pays, axis=0)


def _peer_route_kernel(h_ref, wpq_ref, ka_ref, kb_ref, row_ref, shift_ref, gate_ref):
    hb = h_ref[...].astype(BF16)
    tm = hb.shape[0]
    ka = ka_ref[...]
    kb = kb_ref[...]
    a8 = lax.broadcasted_iota(I32, (SUBLANES, tm), 0)
    b16 = lax.broadcasted_iota(I32, (PEER_TOPK, tm), 0)
    for hd in range(PEER_HEADS):
        q = jnp.dot(hb, wpq_ref[:, hd * PEER_QDIM:(hd + 1) * PEER_QDIM], preferred_element_type=F32).astype(BF16)
        s1 = lax.dot_general(ka, q[:, :PEER_HALF], _NT, preferred_element_type=F32)
        s2 = lax.dot_general(kb, q[:, PEER_HALF:], _NT, preferred_element_type=F32)
        v1, i1 = _top_rows(s1, PEER_TOPK)
        v2, i2 = _top_rows(s2, PEER_TOPK)
        cand, cid, fid = [], [], []

        def block(vals, ids, flat, valid):
            cand.append(jnp.where(valid, vals, NEG_INF) if valid is not None else vals)
            cid.append(ids)
            fid.append(flat)

        block(v1[0:1] + v2, i1[0:1] * PEER_NKEYS + i2, b16, None)
        block(v1[1:2] + v2[:8], i1[1:2] * PEER_NKEYS + i2[:8], PEER_TOPK + a8, None)
        block(v1[8:] + v2[0:1], i1[8:] * PEER_NKEYS + i2[0:1], (a8 + 8) * PEER_TOPK, None)
        for b in range(5):
            lim = PEER_TOPK // (b + 1)
            block(v1[:8] + v2[b:b + 1], i1[:8] * PEER_NKEYS + i2[b:b + 1], a8 * PEER_TOPK + b,
                  (a8 >= 2) & (a8 < lim))
        sc, eid = _top_rows(jnp.concatenate(cand, axis=0), PEER_TOPK,
                            payload=jnp.concatenate(cid, axis=0), order=jnp.concatenate(fid, axis=0))
        e = jnp.exp(sc - sc[0:1])
        rows = slice(hd * PEER_TOPK, (hd + 1) * PEER_TOPK)
        gate_ref[rows, :] = e / jnp.sum(e, axis=0, keepdims=True)
        row_ref[rows, :] = (eid & (HALF_EXPERTS - 1)) * SUBLANES
        shift_ref[rows, :] = (16 - ((eid >> HALF_BIT) << 4)).astype(F32)


def _peer_route(h2d, w_pq, keys_a, keys_b, tm):
    n, d = h2d.shape
    slot_blk = pl.BlockSpec((SLOTS, tm), lambda i: (0, i))
    return pl.pallas_call(
        _peer_route_kernel,
        grid=(n // tm,),
        in_specs=[pl.BlockSpec((tm, d), lambda i: (i, 0)),
                  pl.BlockSpec((d, PEER_HEADS * PEER_QDIM), lambda i: (0, 0)),
                  pl.BlockSpec((PEER_NKEYS, PEER_HALF), lambda i: (0, 0)),
                  pl.BlockSpec((PEER_NKEYS, PEER_HALF), lambda i: (0, 0))],
        out_specs=[slot_blk, slot_blk, slot_blk],
        out_shape=[jax.ShapeDtypeStruct((SLOTS, n), I32), jax.ShapeDtypeStruct((SLOTS, n), F32),
                   jax.ShapeDtypeStruct((SLOTS, n), F32)],
        compiler_params=_cparams(("parallel",)),
        name="peer_route",
    )(h2d, w_pq, keys_a, keys_b)


def _token_column(ref, n, lane):
    col = jnp.sum(jnp.where(lane == n, ref[...], 0.0), axis=1, keepdims=True)
    return jnp.broadcast_to(col, (SLOTS, LANES))


def _expert_row(tab_ref, row, shift_row):
    word = tab_ref[pl.ds(pl.multiple_of(row, SUBLANES), SUBLANES), :]
    return pltpu.bitcast(lax.shift_left(word, jnp.broadcast_to(shift_row, ROW_TILE)) & jnp.int32(HIGH_HALF), F32)


def _fold_sublanes(tiles, sub):
    step = SUBLANES // 2
    while len(tiles) > 1:
        half = len(tiles) // 2
        low = (sub & step) == 0
        nxt = []
        for i in range(half):
            a, b = tiles[i], tiles[i + half]
            if 2 * step == SUBLANES:
                folded = jnp.where(low, a, b) + pltpu.roll(jnp.where(low, b, a), step, 0)
            else:
                folded = jnp.where(low, a + pltpu.roll(a, SUBLANES - step, 0), b + pltpu.roll(b, step, 0))
            nxt.append(folded)
        tiles = nxt
        step //= 2
    return tiles[0]


def _peer_hidden_kernel(row_ref, x_ref, shift_ref, gate_ref, tab_ref, coef_ref, shv_ref, part_ref, hid_ref):
    tm = x_ref.shape[0]
    lane = lax.broadcasted_iota(I32, (SLOTS, tm), 1)
    sub = lax.broadcasted_iota(I32, ROW_TILE, 0)

    def shifts(n):
        return _token_column(shift_ref, n, lane).astype(I32)

    def gather(n, buf):
        x = x_ref[n]
        base = n * SLOTS
        for g in range(SLOTS // SUBLANES):
            prods = []
            for r in range(SUBLANES):
                j = g * SUBLANES + r
                prods.append(_expert_row(tab_ref, row_ref[base + j], shv_ref[buf, j:j + 1, :]) * x)
            part_ref[n, g * SUBLANES:(g + 1) * SUBLANES, :] = _fold_sublanes(prods, sub)

    shv_ref[0] = shifts(0)

    def token_pair(i, carry):
        n = 2 * i
        nxt = shifts(n + 1)
        gather(n, 0)
        shv_ref[1] = nxt
        nxt = shifts(n + 2)
        gather(n + 1, 1)
        shv_ref[0] = nxt
        return carry
    lax.fori_loop(0, tm // 2, token_pair, 0)

    hid_ref[...] = jnp.zeros((SLOTS, tm), F32)

    def reduce_tokens(i, carry):
        hid = hid_ref[...]
        for r in range(SUBLANES):
            n = i * SUBLANES + r
            hid = jnp.where(lane == n, jnp.sum(part_ref[n], axis=1, keepdims=True), hid)
        hid_ref[...] = hid
        return carry
    lax.fori_loop(0, tm // SUBLANES, reduce_tokens, 0)
    hid = hid_ref[...]
    gelu = 0.5 * hid * (1.0 + lax.erf(hid * (2.0 ** -0.5)))
    coef_ref[...] = gate_ref[...] * gelu


def _peer_hidden(rows_flat, x_tiles, shift, gate, table, tm):
    n = x_tiles.shape[0]
    slot_blk = pl.BlockSpec((SLOTS, tm), lambda i: (0, i))
    return pl.pallas_call(
        _peer_hidden_kernel,
        grid=(n // tm,),
        in_specs=[pl.BlockSpec((tm * SLOTS,), lambda i: (i,), memory_space=pltpu.SMEM),
                  pl.BlockSpec((tm, *ROW_TILE), lambda i: (i, 0, 0)),
                  slot_blk, slot_blk,
                  pl.BlockSpec(memory_space=pltpu.VMEM)],
        out_specs=slot_blk,
        out_shape=jax.ShapeDtypeStruct((SLOTS, n), F32),
        scratch_shapes=[pltpu.VMEM((2, SLOTS, LANES), I32), pltpu.VMEM((tm, SLOTS, LANES), F32),
                        pltpu.VMEM((SLOTS, tm), F32)],
        compiler_params=_cparams(("arbitrary",)),
        name="peer_hidden",
    )(rows_flat, x_tiles, shift, gate, table)


N_ACC = 4


def _peer_combine_kernel(row_ref, shift_ref, coef_ref, tab_ref, o_ref, shv_ref, cv_ref):
    tm = o_ref.shape[0]
    lane = lax.broadcasted_iota(I32, (SLOTS, tm), 1)

    def columns(n):
        return _token_column(shift_ref, n, lane).astype(I32), _token_column(coef_ref, n, lane)

    def combine(n, buf):
        base = n * SLOTS
        accs = [None] * N_ACC
        for j in range(SLOTS):
            term = jnp.broadcast_to(cv_ref[buf, j:j + 1, :], ROW_TILE) * _expert_row(
                tab_ref, row_ref[base + j], shv_ref[buf, j:j + 1, :])
            accs[j % N_ACC] = term if accs[j % N_ACC] is None else accs[j % N_ACC] + term
        o_ref[n] = (accs[0] + accs[1]) + (accs[2] + accs[3])

    shv_ref[0], cv_ref[0] = columns(0)

    def token_pair(i, carry):
        n = 2 * i
        nxt = columns(n + 1)
        combine(n, 0)
        shv_ref[1], cv_ref[1] = nxt
        nxt = columns(n + 2)
        combine(n + 1, 1)
        shv_ref[0], cv_ref[0] = nxt
        return carry
    lax.fori_loop(0, tm // 2, token_pair, 0)


def _peer_combine(rows_flat, shift, coef, table, n, tm):
    slot_blk = pl.BlockSpec((SLOTS, tm), lambda i: (0, i))
    return pl.pallas_call(
        _peer_combine_kernel,
        grid=(n // tm,),
        in_specs=[pl.BlockSpec((tm * SLOTS,), lambda i: (i,), memory_space=pltpu.SMEM), slot_blk, slot_blk,
                  pl.BlockSpec(memory_space=pltpu.VMEM)],
        out_specs=pl.BlockSpec((tm, *ROW_TILE), lambda i: (i, 0, 0)),
        out_shape=jax.ShapeDtypeStruct((n, *ROW_TILE), F32),
        scratch_shapes=[pltpu.VMEM((2, SLOTS, LANES), I32), pltpu.VMEM((2, SLOTS, LANES), F32)],
        compiler_params=_cparams(("arbitrary",)),
        name="peer_combine",
    )(rows_flat, shift, coef, table)


def _ln3_kernel(h_ref, p_ref, g_ref, b_ref, y_ref):
    y_ref[...] = _layer_norm(DN_ALPHA * h_ref[...] + p_ref[...], g_ref[...], b_ref[...])


def _ln3(h2d, peer2d, g, b, tm):
    n, d = h2d.shape
    tok = pl.BlockSpec((tm, d), lambda i: (i, 0))
    vec = pl.BlockSpec((1, d), lambda i: (0, 0))
    return pl.pallas_call(
        _ln3_kernel,
        grid=(n // tm,),
        in_specs=[tok, tok, vec, vec],
        out_specs=tok,
        out_shape=jax.ShapeDtypeStruct((n, d), F32),
        compiler_params=_cparams(("parallel",)),
        name="ln3",
    )(h2d, peer2d, g.reshape(1, d), b.reshape(1, d))


def _peer_ffn_ln(h2d, w_pq, keys_a, keys_b, tab_u, tab_v, ln3, tm_route, tm_gather):
    n, d = h2d.shape
    rows, shift, gate = _peer_route(h2d, w_pq, keys_a, keys_b, tm_route)
    rows_flat = rows.T.reshape(-1)
    coef = _peer_hidden(rows_flat, h2d.reshape(n, *ROW_TILE), shift, gate, tab_u, tm_gather)
    out = _peer_combine(rows_flat, shift, coef, tab_v, n, tm_gather)
    return _ln3(h2d, out.reshape(n, d), ln3[0], ln3[1], tm_route)


TOKEN_TILE = 512
ROUTE_TILE = 256
GATHER_TILE = 128
SAMPLE_TPAD = 16
PAGES_PER_STEP = 8


def _pad_axis1(a, rows):
    return jnp.pad(a, ((0, 0), (0, rows - a.shape[1])) + ((0, 0),) * (a.ndim - 2))


def kernel(x_prompt, x_sample, mem_prompt, state_ret, cache_k, cache_v, cache_idx_k, cache_mem_k, cache_mem_v,
           page_table, w_in, ret_gn_g, ret_gn_b, w_out, ln1_g, ln1_b, w_cq, w_ck, w_cv, w_co, ln2_g, ln2_b,
           w_pq, peer_keys_a, peer_keys_b, peer_u, peer_v, ln3_g, ln3_b):
    nb, t_len, d = x_prompt.shape
    db, ds, _ = x_sample.shape
    n_pages = page_table.shape[1]
    past = n_pages * PAGE_SIZE
    bf = lambda w: w.astype(BF16)

    w_in_p = _pad_w_in(w_in)
    w_out_b, w_cq_b, w_co_b, w_pq_b = bf(w_out), bf(w_cq), bf(w_co), bf(w_pq)
    keys_a, keys_b = bf(peer_keys_a), bf(peer_keys_b)
    tab_u, tab_v = _pack_expert_table(peer_u), _pack_expert_table(peer_v)
    ln1, ln2, ln3 = (ln1_g, ln1_b), (ln2_g, ln2_b), (ln3_g, ln3_b)

    def tail(x, ret_y, att_o, mem_k, mem_v, tm):
        h2 = _tail1(x, ret_y, att_o, mem_k, mem_v, w_out_b, w_cq_b, w_co_b, ln1, ln2, tm)
        return h2

    def peer(h2d):
        return _peer_ffn_ln(h2d, w_pq_b, keys_a, keys_b, tab_u, tab_v, ln3, ROUTE_TILE, GATHER_TILE)

    n = nb * t_len
    tabs = _rope_tables(jnp.arange(t_len, dtype=I32))
    rq, rk, rv, rg, aq, ak, akb, av, avt, iq, ikw, ik2, iwt = _project(
        x_prompt.reshape(n, d), w_in_p, tabs, t_len // TOKEN_TILE, TOKEN_TILE)
    ret_y, p_state = _retention(rq, rk, rv, rg, jnp.zeros((nb, RET_HEADS, RET_DK, RET_DV), F32),
                                _retention_tables(RET_CHUNK), ret_gn_g, ret_gn_b, nb, t_len // RET_CHUNK)
    att_o = _dsa_prompt(aq, iq, iwt, ik2, akb, avt, nb, t_len, min(IDX_TOPK_MAX, t_len // 4), kc=TOKEN_TILE)
    mem_kv = _matmul(mem_prompt.reshape(nb * MEM_TOKENS, d), jnp.concatenate([bf(w_ck), bf(w_cv)], axis=1), MEM_TOKENS)
    p_mem_k = mem_kv[:, :d].reshape(nb, MEM_TOKENS, d)
    p_mem_v = mem_kv[:, d:].reshape(nb, MEM_TOKENS, d)
    h2 = tail(x_prompt, ret_y.reshape(nb, t_len, RET_W), att_o.reshape(nb, t_len, ATT_W), p_mem_k, p_mem_v, TOKEN_TILE)
    y_prompt = peer(h2.reshape(n, d)).reshape(nb, t_len, d)

    ns = db * ds
    pos_s = jnp.tile(past + jnp.arange(ds, dtype=I32), db)
    sq, sk, sv, sg, saq, sak, _, sav, _, siq, sikw, _, _ = _project(
        x_sample.reshape(ns, d), w_in_p, _rope_tables(pos_s), 1, ns)
    per_seq = lambda a: a.reshape(db, ds, a.shape[-1])
    chunk = lambda a: _pad_axis1(per_seq(a), RET_CHUNK).reshape(db * RET_CHUNK, a.shape[-1])
    ret_y_s, s_state = _retention(chunk(sq), chunk(sk), chunk(sv), chunk(sg), state_ret,
                                  _retention_tables(ds), ret_gn_g, ret_gn_b, db, 1)
    ret_y_s = ret_y_s.reshape(db, RET_CHUNK, RET_W)[:, :SAMPLE_TPAD]

    pt_flat = page_table.reshape(-1)
    iq_rows = _pad_axis1(per_seq(siq), QPAD).reshape(db, QPAD * IDX_HEADS, IDX_DIM)
    w_rows = _pad_axis1(per_seq(sikw[:, IDX_DIM:IDX_DIM + IDX_HEADS]), QPAD).reshape(db, QPAD * IDX_HEADS, 1)
    ik_new = _pad_axis1(per_seq(sikw[:, :IDX_DIM]), LANES)
    sel = _dsa_sample_select(pt_flat, iq_rows, w_rows, ik_new, cache_idx_k, db, n_pages, ds,
                             min(IDX_TOPK_MAX, (past + ds) // 4), PAGES_PER_STEP)
    page_rows = lambda a: a.reshape(a.shape[0], PAGE_SIZE * ATT_HEADS, ATT_HD)
    new_rows = lambda a: page_rows(_pad_axis1(per_seq(a), PAGE_SIZE))
    q_rows = _pad_axis1(per_seq(saq).astype(F32), QPAD).reshape(db, QPAD, ATT_HEADS, ATT_HD)
    q_rows = q_rows.transpose(0, 2, 1, 3).reshape(db, ATT_HEADS * QPAD, ATT_HD)
    att_s = _dsa_sample_attend(pt_flat, q_rows, sel, new_rows(sak), new_rows(sav),
                               page_rows(cache_k), page_rows(cache_v), db, n_pages, PAGES_PER_STEP)
    att_s = att_s.reshape(db, ATT_HEADS, QPAD, ATT_HD).transpose(0, 2, 1, 3).reshape(db, QPAD, ATT_W)
    h2_s = tail(_pad_axis1(x_sample, SAMPLE_TPAD), ret_y_s, _pad_axis1(att_s, SAMPLE_TPAD),
                cache_mem_k.reshape(db, MEM_TOKENS, d), cache_mem_v.reshape(db, MEM_TOKENS, d), SAMPLE_TPAD)
    y_sample = peer(h2_s[:, :ds].reshape(ns, d)).reshape(db, ds, d)

    heads = lambda a, b_, t_: a.reshape(b_, t_, ATT_HEADS, ATT_HD)
    mem_heads = lambda a: a.reshape(nb, MEM_TOKENS, MEM_HEADS, MEM_HD)
    return (y_prompt, y_sample, p_state, heads(ak, nb, t_len), heads(av, nb, t_len),
            ikw[:, :IDX_DIM].reshape(nb, t_len, IDX_DIM), mem_heads(p_mem_k), mem_heads(p_mem_v),
            s_state, heads(sak, db, ds), heads(sav, db, ds), sikw[:, :IDX_DIM].reshape(db, ds, IDX_DIM))
```

```python
import functools

import jax
import jax.numpy as jnp
from jax import lax
from jax.experimental import pallas as pl
from jax.experimental.pallas import tpu as pltpu

F32 = jnp.float32
BF16 = jnp.bfloat16
I32 = jnp.int32

D_MODEL = 1024
DEPTH = 1
PAST_LEN = 8192
PAGE_SIZE = 128
RET_HEADS = 4
RET_DK = 128
RET_DV = 128
RET_W = RET_HEADS * RET_DV
RET_CHUNK = 128
ATT_HEADS = 4
ATT_HD = 128
ATT_W = ATT_HEADS * ATT_HD
IDX_HEADS = 8
IDX_DIM = 64
IDX_TOPK_MAX = 256
MEM_TOKENS = 256
MEM_HEADS = 4
MEM_HD = D_MODEL // MEM_HEADS
PEER_HEADS = 8
PEER_NKEYS = 128
PEER_EXPERTS = PEER_NKEYS * PEER_NKEYS
PEER_QDIM = 256
PEER_HALF = PEER_QDIM // 2
PEER_TOPK = 16
ROPE_THETA = 10000.0
LN_EPS = 1e-5
GN_EPS = 1e-5
DN_ALPHA = (2 * DEPTH) ** 0.25

LANES = 128
SUBLANES = 8
GROUP_W = 512
N_GROUPS = 8
IN_COLS_PAD = N_GROUPS * GROUP_W + LANES
VMEM_LIMIT = 56 * 1024 * 1024
INT_MIN = -(2 ** 31)
NEG_INF = float("-inf")

_NT = (((1,), (1,)), ((), ()))


def _cparams(sem):
    return pltpu.CompilerParams(dimension_semantics=sem, vmem_limit_bytes=VMEM_LIMIT)


def _rope_tables(pos):
    def tab(d):
        half = d // 2
        inv = ROPE_THETA ** (-jnp.arange(half, dtype=F32) * (2.0 / d))
        ang = pos.astype(F32)[:, None] * inv[None, :]
        cos, sin = jnp.cos(ang), jnp.sin(ang)
        reps = LANES // d
        return (jnp.tile(jnp.concatenate([cos, cos], axis=1), (1, reps)),
                jnp.tile(jnp.concatenate([-sin, sin], axis=1), (1, reps)))
    c128, s128 = tab(ATT_HD)
    c64, s64 = tab(IDX_DIM)
    return c128, s128, c64, s64


def _proj_kernel(x_ref, w_ref, wvt_ref, wwt_ref, c128_ref, s128_ref, c64_ref, s64_ref,
                 rq_ref, rk_ref, rv_ref, rg_ref, aq_ref, ak_ref, akb_ref, av_ref, avt_ref,
                 iq_ref, ikw_ref, ik2_ref, iwt_ref):
    xb = x_ref[...].astype(BF16)
    c128 = c128_ref[...]
    s128 = s128_ref[...]
    c64 = c64_ref[...]
    s64 = s64_ref[...]
    tm = xb.shape[0]
    lane = lax.broadcasted_iota(I32, (tm, LANES), 1)
    first_half64 = (lane & (IDX_DIM // 2)) == 0

    def group(g, width=GROUP_W):
        return jnp.dot(xb, w_ref[:, g * GROUP_W:g * GROUP_W + width], preferred_element_type=F32)

    def rope128(p):
        return p * c128 + pltpu.roll(p, ATT_HD // 2, 1) * s128

    def rope64(p):
        rot = jnp.where(first_half64, pltpu.roll(p, LANES - IDX_DIM // 2, 1), pltpu.roll(p, IDX_DIM // 2, 1))
        return p * c64 + rot * s64

    p = group(0)
    for h in range(RET_HEADS):
        sl = slice(h * LANES, (h + 1) * LANES)
        rq_ref[:, sl] = rope128(p[:, sl]).astype(BF16)
    p = group(1)
    for h in range(RET_HEADS):
        sl = slice(h * LANES, (h + 1) * LANES)
        rk_ref[:, sl] = (rope128(p[:, sl]) * (RET_DK ** -0.5)).astype(BF16)
    rv_ref[...] = group(2).astype(BF16)
    rg_ref[...] = group(3)
    p = group(4)
    for h in range(ATT_HEADS):
        sl = slice(h * LANES, (h + 1) * LANES)
        aq_ref[:, sl] = rope128(p[:, sl]).astype(BF16)
    p = group(5)
    for h in range(ATT_HEADS):
        sl = slice(h * LANES, (h + 1) * LANES)
        r = rope128(p[:, sl])
        ak_ref[:, sl] = r
        akb_ref[:, sl] = r.astype(BF16)
    p = group(6)
    av_ref[...] = p
    avt_ref[0] = lax.dot_general(wvt_ref[...], xb, _NT, preferred_element_type=F32).astype(BF16)
    iwt_ref[...] = lax.dot_general(wwt_ref[...], xb, _NT, preferred_element_type=F32)[:IDX_HEADS]
    p = group(7)
    for j in range(GROUP_W // LANES):
        sl = slice(j * LANES, (j + 1) * LANES)
        iq_ref[:, sl] = rope64(p[:, sl]).astype(BF16)
    p = group(8, LANES)
    is_key = lane < IDX_DIM
    r = jnp.where(is_key, rope64(p), p)
    ikw_ref[...] = r
    ik2_ref[...] = jnp.where(is_key, r, pltpu.roll(r, IDX_DIM, 1)).astype(BF16)


def _project(x2d, w_pad, tables, period_tiles, tm):
    n = x2d.shape[0]
    nt = n // tm
    row = lambda i: (i, 0)
    tab = lambda i: (i % period_tiles, 0)
    const = lambda i: (0, 0)
    flat = lambda w, dt: (pl.BlockSpec((tm, w), row), jax.ShapeDtypeStruct((n, w), dt))
    outs = [
        flat(GROUP_W, BF16), flat(GROUP_W, BF16), flat(GROUP_W, BF16), flat(GROUP_W, F32),
        flat(GROUP_W, BF16), flat(GROUP_W, F32), flat(GROUP_W, BF16), flat(GROUP_W, F32),
        (pl.BlockSpec((1, ATT_W, tm), lambda i: (i, 0, 0)), jax.ShapeDtypeStruct((nt, ATT_W, tm), BF16)),
        flat(GROUP_W, BF16), flat(LANES, F32), flat(LANES, BF16),
        (pl.BlockSpec((IDX_HEADS, tm), lambda i: (0, i)), jax.ShapeDtypeStruct((IDX_HEADS, n), F32)),
    ]
    v_cols = slice(6 * GROUP_W, 7 * GROUP_W)
    w_cols = slice(N_GROUPS * GROUP_W + IDX_DIM, N_GROUPS * GROUP_W + IDX_DIM + 2 * IDX_HEADS)
    return pl.pallas_call(
        _proj_kernel,
        grid=(nt,),
        in_specs=[pl.BlockSpec((tm, D_MODEL), row),
                  pl.BlockSpec((D_MODEL, IN_COLS_PAD), const),
                  pl.BlockSpec((GROUP_W, D_MODEL), const), pl.BlockSpec((2 * IDX_HEADS, D_MODEL), const)]
                 + [pl.BlockSpec((tm, LANES), tab)] * 4,
        out_specs=[spec for spec, _ in outs],
        out_shape=[shape for _, shape in outs],
        compiler_params=_cparams(("parallel",)),
        name="proj_rope",
    )(x2d, w_pad, w_pad[:, v_cols].T, w_pad[:, w_cols].T, *tables)


def _pad_w_in(w_in):
    return jnp.pad(w_in.astype(BF16), ((0, 0), (0, IN_COLS_PAD - w_in.shape[1])))


def _retention_log_decay():
    return jnp.log(1.0 - 2.0 ** (-5.0 - jnp.arange(RET_HEADS, dtype=F32)))


def _retention_tables(chunk_len):
    log_g = _retention_log_decay()
    j = jnp.arange(RET_CHUNK, dtype=F32)
    diff = j[:, None] - j[None, :]
    dmat = jnp.where(diff >= 0, jnp.exp(log_g[:, None, None] * jnp.maximum(diff, 0.0)), 0.0)
    q_dec = jnp.exp(log_g[None, :] * (j[:, None] + 1.0))
    k_dec = jnp.exp(log_g[None, :] * (chunk_len - 1.0 - j[:, None]))
    k_dec = jnp.where(j[:, None] < chunk_len, k_dec, 0.0)
    s_dec = jnp.exp(log_g * chunk_len)
    widen = lambda t: jnp.repeat(t, LANES, axis=1)
    return dmat, widen(q_dec), widen(k_dec), s_dec


def _retention_kernel(sdec_ref, q_ref, k_ref, v_ref, g_ref, s0_ref, dmat_ref, qdec_ref, kdec_ref,
                      gng_ref, gnb_ref, y_ref, sout_ref, state_ref):
    c = pl.program_id(1)

    @pl.when(c == 0)
    def _():
        state_ref[...] = s0_ref[0]

    for h in range(RET_HEADS):
        sl = slice(h * LANES, (h + 1) * LANES)
        q = q_ref[:, sl]
        k = k_ref[:, sl]
        v = v_ref[:, sl]
        s_prev = state_ref[h]
        scores = lax.dot_general(q, k, _NT, preferred_element_type=F32) * dmat_ref[h]
        o = jnp.dot(scores.astype(BF16), v, preferred_element_type=F32)
        o = o + jnp.dot(q, s_prev.astype(BF16), preferred_element_type=F32) * qdec_ref[:, sl]
        kd_t = (k.astype(F32) * kdec_ref[:, sl]).T.astype(BF16)
        state_ref[h] = s_prev * sdec_ref[h] + jnp.dot(kd_t, v, preferred_element_type=F32)
        mu = jnp.mean(o, axis=-1, keepdims=True)
        var = jnp.mean(jnp.square(o - mu), axis=-1, keepdims=True)
        n = (o - mu) * lax.rsqrt(var + GN_EPS) * gng_ref[:, sl] + gnb_ref[:, sl]
        g = g_ref[:, sl]
        y_ref[:, sl] = (g * (1.0 / (1.0 + jnp.exp(-g))) * n).astype(BF16)

    @pl.when(c == pl.num_programs(1) - 1)
    def _():
        sout_ref[0] = state_ref[...]


def _retention(q, k, v, g, s0, tables, gn_g, gn_b, nb, nc):
    dmat, qdec, kdec, sdec = tables
    n = q.shape[0]
    row = lambda b, c: (b * nc + c, 0)
    const2 = lambda b, c: (0, 0)
    blk = pl.BlockSpec((RET_CHUNK, RET_W), row)
    return pl.pallas_call(
        _retention_kernel,
        grid=(nb, nc),
        in_specs=[pl.BlockSpec(memory_space=pltpu.SMEM), blk, blk, blk, blk,
                  pl.BlockSpec((1, RET_HEADS, RET_DK, RET_DV), lambda b, c: (b, 0, 0, 0)),
                  pl.BlockSpec((RET_HEADS, RET_CHUNK, RET_CHUNK), lambda b, c: (0, 0, 0)),
                  pl.BlockSpec((RET_CHUNK, RET_W), const2), pl.BlockSpec((RET_CHUNK, RET_W), const2),
                  pl.BlockSpec((1, RET_W), const2), pl.BlockSpec((1, RET_W), const2)],
        out_specs=[blk, pl.BlockSpec((1, RET_HEADS, RET_DK, RET_DV), lambda b, c: (b, 0, 0, 0))],
        out_shape=[jax.ShapeDtypeStruct((n, RET_W), BF16),
                   jax.ShapeDtypeStruct((nb, RET_HEADS, RET_DK, RET_DV), F32)],
        scratch_shapes=[pltpu.VMEM((RET_HEADS, RET_DK, RET_DV), F32)],
        compiler_params=_cparams(("parallel", "arbitrary")),
        name="retention",
    )(sdec, q, k, v, g, s0, dmat, qdec, kdec, gn_g.reshape(1, RET_W), gn_b.reshape(1, RET_W))


def _sort_key(x):
    bits = pltpu.bitcast(x, I32)
    return bits ^ ((bits >> 31) & 0x7FFFFFFF)


def _kth_largest_key(count_ge, shape, topk):
    def bit_body(bi, t):
        cand = t + lax.shift_left(jnp.int32(1), 31 - bi)
        return jnp.where(count_ge(cand) >= topk, cand, t)
    return lax.fori_loop(0, 32, bit_body, jnp.full(shape, INT_MIN, I32))


def _tie_limit(count_eq_below, need, shape, pos_bits):
    def bit_body(bi, p):
        cand = p + lax.shift_left(jnp.int32(1), pos_bits - 1 - bi)
        return jnp.where(count_eq_below(cand) <= need, cand, p)
    return lax.fori_loop(0, pos_bits, bit_body, jnp.zeros(shape, I32))


def _dsa_prompt_kernel(aq_ref, iq_ref, iwt_ref, ik2_ref, ak_ref, avt_ref, o_ref, keys_ref, bias_ref, *,
                       tq, kc, topk, pos_bits):
    i = pl.program_id(1)
    q0 = i * tq
    nk = (q0 + tq + kc - 1) // kc
    groups = kc // SUBLANES
    state = (SUBLANES, tq)
    lane = lax.broadcasted_iota(I32, (tq, LANES), 1)
    key_in_chunk = (lax.broadcasted_iota(I32, (groups, SUBLANES, tq), 0) * SUBLANES
                    + lax.broadcasted_iota(I32, (groups, SUBLANES, tq), 1))
    qpos = q0 + lax.broadcasted_iota(I32, (groups, SUBLANES, tq), 2)
    iq = iq_ref[...]
    iwt = iwt_ref[...]

    def score_chunk(c, carry):
        k0 = pl.multiple_of(c * kc, kc)
        ikc = ik2_ref[pl.ds(k0, kc), :]
        acc = None
        for h in range(IDX_HEADS):
            pair = iq[:, (h // 2) * LANES:(h // 2 + 1) * LANES]
            keep = (lane < IDX_DIM) if h % 2 == 0 else (lane >= IDX_DIM)
            qh = jnp.where(keep, pair, jnp.zeros_like(pair))
            s = lax.dot_general(ikc, qh, _NT, preferred_element_type=F32)
            term = jnp.maximum(s, 0.0) * iwt[h:h + 1, :]
            acc = term if acc is None else acc + term
        key = _sort_key(acc).reshape(groups, SUBLANES, tq)
        keys_ref[c] = jnp.where(k0 + key_in_chunk <= qpos, key, INT_MIN)
        return carry
    lax.fori_loop(0, nk, score_chunk, 0)

    def total(a):
        return jnp.broadcast_to(jnp.sum(a, axis=0, keepdims=True), state)

    def count_ge(cand):
        body = lambda c, a: a + jnp.sum(jnp.where(keys_ref[c] >= cand[None], 1, 0), axis=0)
        return total(lax.fori_loop(0, nk, body, jnp.zeros(state, I32)))

    t = _kth_largest_key(count_ge, state, topk)
    n_ge = count_ge(t)
    need = topk - count_ge(t + 1)

    def count_eq_below(p):
        def body(c, a):
            hit = (keys_ref[c] == t[None]) & (c * kc + key_in_chunk < p[None])
            return a + jnp.sum(jnp.where(hit, 1, 0), axis=0)
        return total(lax.fori_loop(0, nk, body, jnp.zeros(state, I32)))

    has_excess = jnp.max(jnp.where(n_ge > topk, 1, 0)) > 0
    plim = lax.cond(has_excess,
                    lambda: _tie_limit(count_eq_below, need, state, pos_bits),
                    lambda: jnp.full(state, 2 ** pos_bits, I32))

    def bias_chunk(c, carry):
        key = keys_ref[c]
        sel = (key > t[None]) | ((key == t[None]) & (c * kc + key_in_chunk < plim[None]))
        bias_ref[c] = jnp.where(sel & (key != INT_MIN), 0.0, NEG_INF)
        return carry
    lax.fori_loop(0, nk, bias_chunk, 0)

    scale = ATT_HD ** -0.5
    head_cols = [slice(h * LANES, (h + 1) * LANES) for h in range(ATT_HEADS)]
    queries = [aq_ref[:, sl] for sl in head_cols]

    def attend_chunk(c, carry):
        k0 = pl.multiple_of(c * kc, kc)
        bias = bias_ref[c].reshape(kc, tq)
        out = []
        for h, sl in enumerate(head_cols):
            m, l, acc = carry[h]
            s = lax.dot_general(ak_ref[pl.ds(k0, kc), sl], queries[h], _NT, preferred_element_type=F32)
            s = s * scale + bias
            m_new = jnp.maximum(m, jnp.max(s, axis=0, keepdims=True))
            m_safe = jnp.where(m_new == NEG_INF, 0.0, m_new)
            p = jnp.exp(s - m_safe)
            alpha = jnp.exp(m - m_safe)
            l_new = alpha * l + jnp.sum(p, axis=0, keepdims=True)
            acc_new = alpha * acc + jnp.dot(avt_ref[c, sl, :], p.astype(BF16), preferred_element_type=F32)
            out.append((m_new, l_new, acc_new))
        return tuple(out)

    init = tuple((jnp.full((1, tq), NEG_INF, F32), jnp.zeros((1, tq), F32), jnp.zeros((ATT_HD, tq), F32))
                 for _ in head_cols)
    fin = lax.fori_loop(0, nk, attend_chunk, init)
    for (m, l, acc), sl in zip(fin, head_cols):
        o_ref[:, sl] = (acc / l).T.astype(BF16)


def _dsa_prompt(aq, iq, iwt, ik2, akb, avt, nb, t_len, topk, tq=LANES, kc=512):
    n = aq.shape[0]
    nq = t_len // tq
    qrow = lambda b, i: (b * nq + i, 0)
    brow = lambda b, i: (b, 0)
    kern = functools.partial(_dsa_prompt_kernel, tq=tq, kc=kc, topk=topk, pos_bits=(t_len - 1).bit_length() + 1)
    chunked = (t_len // kc, kc // SUBLANES, SUBLANES, tq)
    return pl.pallas_call(
        kern,
        grid=(nb, nq),
        in_specs=[pl.BlockSpec((tq, ATT_W), qrow), pl.BlockSpec((tq, GROUP_W), qrow),
                  pl.BlockSpec((IDX_HEADS, tq), lambda b, i: (0, b * nq + i)),
                  pl.BlockSpec((t_len, LANES), brow), pl.BlockSpec((t_len, ATT_W), brow),
                  pl.BlockSpec((t_len // kc, ATT_W, kc), lambda b, i: (b, 0, 0))],
        out_specs=pl.BlockSpec((tq, ATT_W), qrow),
        out_shape=jax.ShapeDtypeStruct((n, ATT_W), BF16),
        scratch_shapes=[pltpu.VMEM(chunked, I32), pltpu.VMEM(chunked, F32)],
        compiler_params=_cparams(("parallel", "arbitrary")),
        name="dsa_prompt",
    )(aq, iq, iwt, ik2, akb, avt)


QPAD = SUBLANES


def _dsa_sample_select_kernel(pt_ref, iq_ref, w_ref, iknew_ref, expand_ref, *rest, pg, n_pages, n_new, topk,
                              pos_bits):
    idx_refs = rest[:pg]
    sel_ref = rest[pg]
    keys_ref = rest[pg + 1]
    p = pl.program_id(1)
    iq = iq_ref[0]
    w = w_ref[0]
    key_in_block = lax.broadcasted_iota(I32, (QPAD, LANES), 1)
    row = lax.broadcasted_iota(I32, (QPAD, LANES), 0)

    def scores(keys_t):
        s = jnp.dot(iq, keys_t.astype(BF16), preferred_element_type=F32)
        r = jnp.maximum(s, 0.0) * w
        return jnp.sum(r.reshape(QPAD, IDX_HEADS, LANES), axis=1)

    for j in range(pg):
        keys_ref[p * pg + j] = _sort_key(scores(idx_refs[j][0]))

    @pl.when(p == pl.num_programs(1) - 1)
    def _():
        s_new = scores(iknew_ref[0])
        keys_ref[n_pages] = jnp.where((key_in_block <= row) & (key_in_block < n_new), _sort_key(s_new), INT_MIN)
        keys_ref[n_pages + 1] = jnp.full((QPAD, LANES), INT_MIN, I32)
        keys = keys_ref[...]
        pos = lax.broadcasted_iota(I32, keys.shape, 0) * LANES + key_in_block[None]

        def lane_total(a):
            return jnp.broadcast_to(jnp.sum(a, axis=1, keepdims=True), (QPAD, LANES))

        def count_ge(cand):
            return lane_total(jnp.sum(jnp.where(keys >= cand[None], 1, 0), axis=0))

        t = _kth_largest_key(count_ge, (QPAD, LANES), topk)
        need = topk - count_ge(t + 1)

        def count_eq_below(lim):
            return lane_total(jnp.sum(jnp.where((keys == t[None]) & (pos < lim[None]), 1, 0), axis=0))

        has_excess = jnp.max(jnp.where(count_ge(t) > topk, 1, 0)) > 0
        plim = lax.cond(has_excess,
                        lambda: _tie_limit(count_eq_below, need, (QPAD, LANES), pos_bits),
                        lambda: jnp.full((QPAD, LANES), 2 ** pos_bits, I32))

        sel = ((keys > t[None]) | ((keys == t[None]) & (pos < plim[None]))) & (keys != INT_MIN)
        flags = jnp.where(sel, 1.0, 0.0).reshape(keys.shape[0] * QPAD, LANES).astype(BF16)
        out = jnp.dot(flags, expand_ref[...], preferred_element_type=F32)
        sel_ref[0] = out.reshape(keys.shape[0], QPAD, out.shape[1])


def _dsa_sample_select(pt_flat, iq_rows, w_rows, ik_new, cache_idx_k, nb, n_pages, n_new, topk, pg):
    kern = functools.partial(_dsa_sample_select_kernel, pg=pg, n_pages=n_pages, n_new=n_new, topk=topk,
                             pos_bits=((n_pages + 1) * LANES).bit_length())
    per_b = lambda shape: pl.BlockSpec((1,) + shape, lambda b, p, pt: (b,) + (0,) * len(shape))
    keys_minor = lambda a: a.transpose(0, 2, 1)
    page = lambda j: pl.BlockSpec((1, IDX_DIM, PAGE_SIZE), lambda b, p, pt: (pt[b * n_pages + p * pg + j], 0, 0))
    cols = PAGE_SIZE * ATT_HEADS
    expand = (jnp.arange(cols)[None, :] // ATT_HEADS == jnp.arange(PAGE_SIZE)[:, None]).astype(BF16)
    nblk = n_pages + 2
    return pl.pallas_call(
        kern,
        grid_spec=pltpu.PrefetchScalarGridSpec(
            num_scalar_prefetch=1,
            grid=(nb, n_pages // pg),
            in_specs=[per_b((QPAD * IDX_HEADS, IDX_DIM)), per_b((QPAD * IDX_HEADS, 1)), per_b((IDX_DIM, PAGE_SIZE)),
                      pl.BlockSpec((PAGE_SIZE, cols), lambda b, p, pt: (0, 0))]
                     + [page(j) for j in range(pg)],
            out_specs=per_b((nblk, QPAD, cols)),
            scratch_shapes=[pltpu.VMEM((nblk, QPAD, LANES), I32)]),
        out_shape=jax.ShapeDtypeStruct((nb, nblk, QPAD, cols), F32),
        compiler_params=_cparams(("parallel", "arbitrary")),
        name="dsa_sample_select",
    )(pt_flat, iq_rows, w_rows, keys_minor(ik_new), expand, *([keys_minor(cache_idx_k)] * pg))


def _dsa_sample_attend_kernel(pt_ref, q_ref, sel_ref, knew_ref, vnew_ref, *rest, pg, n_pages):
    k_refs = rest[:pg]
    v_refs = rest[pg:2 * pg]
    o_ref = rest[2 * pg]
    m_ref, l_ref, acc_ref = rest[2 * pg + 1:]
    p = pl.program_id(1)
    scale = ATT_HD ** -0.5
    rows = ATT_HEADS * QPAD
    cols = PAGE_SIZE * ATT_HEADS

    @pl.when(p == 0)
    def _():
        m_ref[...] = jnp.full(m_ref.shape, NEG_INF, F32)
        l_ref[...] = jnp.zeros(l_ref.shape, F32)
        acc_ref[...] = jnp.zeros(acc_ref.shape, F32)

    q = q_ref[0].astype(BF16)
    own_head = ((lax.broadcasted_iota(I32, (rows, cols), 1) & (ATT_HEADS - 1))
                == (lax.broadcasted_iota(I32, (rows, cols), 0) >> (QPAD.bit_length() - 1)))

    def update(blocks):
        scores = []
        for flags, kp, _ in blocks:
            s = lax.dot_general(q, kp.astype(BF16), _NT, preferred_element_type=F32) * scale
            mask = own_head & (jnp.concatenate([flags] * ATT_HEADS, axis=0) > 0.5)
            scores.append(jnp.where(mask, s, NEG_INF))
        mc = scores[0]
        for s in scores[1:]:
            mc = jnp.maximum(mc, s)
        m = m_ref[...]
        m_new = jnp.maximum(m, jnp.max(mc, axis=1, keepdims=True))
        m_safe = jnp.where(m_new == NEG_INF, 0.0, m_new)
        m_row = m_safe[:, :1]
        psum = jnp.zeros((rows, cols), F32)
        acc = jnp.exp(m - m_safe) * acc_ref[...]
        for (_, _, vp), s in zip(blocks, scores):
            pr = jnp.exp(s - m_row)
            psum = psum + pr
            acc = acc + jnp.dot(pr.astype(BF16), vp.astype(BF16), preferred_element_type=F32)
        l_ref[...] = jnp.exp(m - m_safe) * l_ref[...] + jnp.sum(psum, axis=1, keepdims=True)
        acc_ref[...] = acc
        m_ref[...] = m_new

    update([(sel_ref[0, p * pg + j], k_refs[j][0], v_refs[j][0]) for j in range(pg)])

    @pl.when(p == pl.num_programs(1) - 1)
    def _():
        update([(sel_ref[0, n_pages], knew_ref[0], vnew_ref[0])])
        o_ref[0] = acc_ref[...] / l_ref[...]


def _dsa_sample_attend(pt_flat, q_rows, sel, k_new, v_new, cache_k, cache_v, nb, n_pages, pg):
    kern = functools.partial(_dsa_sample_attend_kernel, pg=pg, n_pages=n_pages)
    rows = ATT_HEADS * QPAD
    cols = PAGE_SIZE * ATT_HEADS
    per_b = lambda shape: pl.BlockSpec((1,) + shape, lambda b, p, pt: (b,) + (0,) * len(shape))
    page = lambda j: pl.BlockSpec((1, cols, ATT_HD), lambda b, p, pt: (pt[b * n_pages + p * pg + j], 0, 0))
    return pl.pallas_call(
        kern,
        grid_spec=pltpu.PrefetchScalarGridSpec(
            num_scalar_prefetch=1,
            grid=(nb, n_pages // pg),
            in_specs=[per_b((rows, ATT_HD)), per_b((n_pages + 2, QPAD, cols)), per_b((cols, ATT_HD)),
                      per_b((cols, ATT_HD))] + [page(j) for j in range(pg)] * 2,
            out_specs=per_b((rows, ATT_HD)),
            scratch_shapes=[pltpu.VMEM((rows, LANES), F32)] * 3),
        out_shape=jax.ShapeDtypeStruct((nb, rows, ATT_HD), F32),
        compiler_params=_cparams(("parallel", "arbitrary")),
        name="dsa_sample_attend",
    )(pt_flat, q_rows, sel, k_new, v_new, *([cache_k] * pg), *([cache_v] * pg))


def _layer_norm(x, g, b):
    mu = jnp.mean(x, axis=-1, keepdims=True)
    var = jnp.mean(jnp.square(x - mu), axis=-1, keepdims=True)
    return (x - mu) * lax.rsqrt(var + LN_EPS) * g + b


def _matmul_kernel(x_ref, w_ref, o_ref):
    o_ref[...] = jnp.dot(x_ref[...].astype(BF16), w_ref[...], preferred_element_type=F32)


def _matmul(x, w_bf16, tm):
    n, kdim = x.shape
    m = w_bf16.shape[1]
    return pl.pallas_call(
        _matmul_kernel,
        grid=(n // tm,),
        in_specs=[pl.BlockSpec((tm, kdim), lambda i: (i, 0)), pl.BlockSpec((kdim, m), lambda i: (0, 0))],
        out_specs=pl.BlockSpec((tm, m), lambda i: (i, 0)),
        out_shape=jax.ShapeDtypeStruct((n, m), F32),
        compiler_params=_cparams(("parallel",)),
        name="mem_proj",
    )(x, w_bf16)


def _tail1_kernel(x_ref, ry_ref, ao_ref, mk_ref, mv_ref, wout_ref, wcq_ref, wco_ref,
                  ln1g_ref, ln1b_ref, ln2g_ref, ln2b_ref, h2_ref, *, head_rows):
    x = x_ref[0]
    tm = x.shape[0]
    mix = jnp.dot(ry_ref[0].astype(BF16), wout_ref[:RET_W, :], preferred_element_type=F32)
    mix = mix + jnp.dot(ao_ref[0].astype(BF16), wout_ref[RET_W:, :], preferred_element_type=F32)
    h1 = _layer_norm(DN_ALPHA * x + mix, ln1g_ref[...], ln1b_ref[...])
    q = jnp.dot(h1.astype(BF16), wcq_ref[...], preferred_element_type=F32).astype(BF16)
    head_cols = [slice(h * MEM_HD, (h + 1) * MEM_HD) for h in range(MEM_HEADS)]
    if head_rows:
        q_rows = jnp.concatenate([q[:, sl] for sl in head_cols], axis=0)
        s = lax.dot_general(q_rows, mk_ref[0].astype(BF16), _NT, preferred_element_type=F32) * (MEM_HD ** -0.5)
        assert tm & (tm - 1) == 0, "stacked-head rows are split by a shift"
        own_head = ((lax.broadcasted_iota(I32, s.shape, 1) & (MEM_HEADS - 1))
                    == lax.broadcasted_iota(I32, s.shape, 0) >> (tm.bit_length() - 1))
        s = jnp.where(own_head, s, NEG_INF)
        e = jnp.exp(s - jnp.max(s, axis=-1, keepdims=True))
        p = e / jnp.sum(e, axis=-1, keepdims=True)
        o = jnp.dot(p.astype(BF16), mv_ref[0].astype(BF16), preferred_element_type=F32).astype(BF16)
        heads = [o[h * tm:(h + 1) * tm] for h in range(MEM_HEADS)]
    else:
        heads = []
        for sl in head_cols:
            mk = mk_ref[0, :, sl].astype(BF16)
            mv = mv_ref[0, :, sl].astype(BF16)
            s = lax.dot_general(q[:, sl], mk, _NT, preferred_element_type=F32) * (MEM_HD ** -0.5)
            e = jnp.exp(s - jnp.max(s, axis=-1, keepdims=True))
            p = e / jnp.sum(e, axis=-1, keepdims=True)
            heads.append(jnp.dot(p.astype(BF16), mv, preferred_element_type=F32).astype(BF16))
    ca = jnp.concatenate(heads, axis=1)
    h2 = DN_ALPHA * h1 + jnp.dot(ca, wco_ref[...], preferred_element_type=F32)
    h2_ref[0] = _layer_norm(h2, ln2g_ref[...], ln2b_ref[...])


def _tail1(x, ret_y, att_o, mem_k, mem_v, w_out, w_cq, w_co, ln1, ln2, tm):
    g, tg, d = x.shape
    head_rows = mem_k.ndim == 4
    if head_rows:
        mem_k = mem_k.reshape(g, MEM_TOKENS * MEM_HEADS, MEM_HD)
        mem_v = mem_v.reshape(g, MEM_TOKENS * MEM_HEADS, MEM_HD)
    tok = lambda w: pl.BlockSpec((1, tm, w), lambda b, i: (b, i, 0))
    mem = pl.BlockSpec((1,) + mem_k.shape[1:], lambda b, i: (b, 0, 0))
    wsp = lambda r, c: pl.BlockSpec((r, c), lambda b, i: (0, 0))
    vec = wsp(1, d)
    return pl.pallas_call(
        functools.partial(_tail1_kernel, head_rows=head_rows),
        grid=(g, tg // tm),
        in_specs=[tok(d), tok(RET_W), tok(ATT_W), mem, mem, wsp(d, d), wsp(d, d), wsp(d, d), vec, vec, vec, vec],
        out_specs=tok(d),
        out_shape=jax.ShapeDtypeStruct((g, tg, d), F32),
        compiler_params=_cparams(("parallel", "arbitrary")),
        name="tail1",
    )(x, ret_y, att_o, mem_k, mem_v, w_out, w_cq, w_co,
      ln1[0].reshape(1, d), ln1[1].reshape(1, d), ln2[0].reshape(1, d), ln2[1].reshape(1, d))


HALF_EXPERTS = PEER_EXPERTS // 2
HALF_BIT = HALF_EXPERTS.bit_length() - 1
ROW_TILE = (SUBLANES, LANES)
SLOTS = PEER_HEADS * PEER_TOPK
HIGH_HALF = -65536


def _pack_expert_table(w):
    bits = lax.bitcast_convert_type(w.astype(BF16), jnp.uint16).astype(jnp.uint32)
    packed = bits[:HALF_EXPERTS] | (bits[HALF_EXPERTS:] << 16)
    return lax.bitcast_convert_type(packed, I32).reshape(HALF_EXPERTS * SUBLANES, LANES)


def _top_rows(s, k, payload=None, order=None):
    rows = s.shape[0]
    rid = lax.broadcasted_iota(I32, s.shape, 0)
    order = rid if order is None else order
    big = jnp.int32(2 ** 30)
    vals, pays = [], []
    for _ in range(k):
        m = jnp.max(s, axis=0, keepdims=True)
        win = jnp.min(jnp.where(s == m, order, big), axis=0, keepdims=True)
        hit = order == win
        vals.append(m)
        pays.append(win if payload is None else jnp.max(jnp.where(hit, payload, -1), axis=0, keepdims=True))
        s = jnp.where(hit, NEG_INF, s)
    return jnp.concatenate(vals, axis=0), jnp.concatenate(pays, axis=0)


def _peer_route_kernel(h_ref, wpq_ref, ka_ref, kb_ref, row_ref, shift_ref, gate_ref):
    hb = h_ref[...].astype(BF16)
    tm = hb.shape[0]
    ka = ka_ref[...]
    kb = kb_ref[...]
    a8 = lax.broadcasted_iota(I32, (SUBLANES, tm), 0)
    b16 = lax.broadcasted_iota(I32, (PEER_TOPK, tm), 0)
    for hd in range(PEER_HEADS):
        q = jnp.dot(hb, wpq_ref[:, hd * PEER_QDIM:(hd + 1) * PEER_QDIM], preferred_element_type=F32).astype(BF16)
        s1 = lax.dot_general(ka, q[:, :PEER_HALF], _NT, preferred_element_type=F32)
        s2 = lax.dot_general(kb, q[:, PEER_HALF:], _NT, preferred_element_type=F32)
        v1, i1 = _top_rows(s1, PEER_TOPK)
        v2, i2 = _top_rows(s2, PEER_TOPK)
        cand, cid, fid = [], [], []

        def block(vals, ids, flat, valid):
            cand.append(jnp.where(valid, vals, NEG_INF) if valid is not None else vals)
            cid.append(ids)
            fid.append(flat)

        block(v1[0:1] + v2, i1[0:1] * PEER_NKEYS + i2, b16, None)
        block(v1[1:2] + v2[:8], i1[1:2] * PEER_NKEYS + i2[:8], PEER_TOPK + a8, None)
        block(v1[8:] + v2[0:1], i1[8:] * PEER_NKEYS + i2[0:1], (a8 + 8) * PEER_TOPK, None)
        for b in range(5):
            lim = PEER_TOPK // (b + 1)
            block(v1[:8] + v2[b:b + 1], i1[:8] * PEER_NKEYS + i2[b:b + 1], a8 * PEER_TOPK + b,
                  (a8 >= 2) & (a8 < lim))
        sc, eid = _top_rows(jnp.concatenate(cand, axis=0), PEER_TOPK,
                            payload=jnp.concatenate(cid, axis=0), order=jnp.concatenate(fid, axis=0))
        e = jnp.exp(sc - sc[0:1])
        rows = slice(hd * PEER_TOPK, (hd + 1) * PEER_TOPK)
        gate_ref[rows, :] = e / jnp.sum(e, axis=0, keepdims=True)
        row_ref[rows, :] = (eid & (HALF_EXPERTS - 1)) * SUBLANES
        shift_ref[rows, :] = (16 - ((eid >> HALF_BIT) << 4)).astype(F32)


def _peer_route(h2d, w_pq, keys_a, keys_b, tm):
    n, d = h2d.shape
    slot_blk = pl.BlockSpec((SLOTS, tm), lambda i: (0, i))
    return pl.pallas_call(
        _peer_route_kernel,
        grid=(n // tm,),
        in_specs=[pl.BlockSpec((tm, d), lambda i: (i, 0)),
                  pl.BlockSpec((d, PEER_HEADS * PEER_QDIM), lambda i: (0, 0)),
                  pl.BlockSpec((PEER_NKEYS, PEER_HALF), lambda i: (0, 0)),
                  pl.BlockSpec((PEER_NKEYS, PEER_HALF), lambda i: (0, 0))],
        out_specs=[slot_blk, slot_blk, slot_blk],
        out_shape=[jax.ShapeDtypeStruct((SLOTS, n), I32), jax.ShapeDtypeStruct((SLOTS, n), F32),
                   jax.ShapeDtypeStruct((SLOTS, n), F32)],
        compiler_params=_cparams(("parallel",)),
        name="peer_route",
    )(h2d, w_pq, keys_a, keys_b)


def _token_column(ref, n, lane):
    col = jnp.sum(jnp.where(lane == n, ref[...], 0.0), axis=1, keepdims=True)
    return jnp.broadcast_to(col, (SLOTS, LANES))


def _expert_row(tab_ref, row, shift_row):
    word = tab_ref[pl.ds(pl.multiple_of(row, SUBLANES), SUBLANES), :]
    return pltpu.bitcast(lax.shift_left(word, jnp.broadcast_to(shift_row, ROW_TILE)) & jnp.int32(HIGH_HALF), F32)


def _fold_sublanes(tiles, sub):
    step = SUBLANES // 2
    while len(tiles) > 1:
        half = len(tiles) // 2
        low = (sub & step) == 0
        nxt = []
        for i in range(half):
            a, b = tiles[i], tiles[i + half]
            if 2 * step == SUBLANES:
                folded = jnp.where(low, a, b) + pltpu.roll(jnp.where(low, b, a), step, 0)
            else:
                folded = jnp.where(low, a + pltpu.roll(a, SUBLANES - step, 0), b + pltpu.roll(b, step, 0))
            nxt.append(folded)
        tiles = nxt
        step //= 2
    return tiles[0]


TOKENS_PER_STEP = 4


def _peer_hidden_kernel(row_ref, x_ref, shift_ref, gate_ref, tab_ref, coef_ref, shv_ref, part_ref, hid_ref):
    tm = x_ref.shape[0]
    lane = lax.broadcasted_iota(I32, (SLOTS, tm), 1)
    sub = lax.broadcasted_iota(I32, ROW_TILE, 0)

    def shifts(n):
        return _token_column(shift_ref, n, lane).astype(I32)

    def gather(n, buf):
        x = x_ref[n]
        base = n * SLOTS
        for g in range(SLOTS // SUBLANES):
            prods = []
            for r in range(SUBLANES):
                j = g * SUBLANES + r
                prods.append(_expert_row(tab_ref, row_ref[base + j], shv_ref[buf, j:j + 1, :]) * x)
            part_ref[n, g * SUBLANES:(g + 1) * SUBLANES, :] = _fold_sublanes(prods, sub)

    shv_ref[0] = shifts(0)

    def token_group(i, carry):
        for k in range(TOKENS_PER_STEP):
            n = TOKENS_PER_STEP * i + k
            nxt = shifts(n + 1)
            gather(n, k % 2)
            shv_ref[(k + 1) % 2] = nxt
        return carry
    lax.fori_loop(0, tm // TOKENS_PER_STEP, token_group, 0)

    hid_ref[...] = jnp.zeros((SLOTS, tm), F32)

    def reduce_tokens(i, carry):
        hid = hid_ref[...]
        for r in range(SUBLANES):
            n = i * SUBLANES + r
            hid = jnp.where(lane == n, jnp.sum(part_ref[n], axis=1, keepdims=True), hid)
        hid_ref[...] = hid
        return carry
    lax.fori_loop(0, tm // SUBLANES, reduce_tokens, 0)
    hid = hid_ref[...]
    gelu = 0.5 * hid * (1.0 + lax.erf(hid * (2.0 ** -0.5)))
    coef_ref[...] = gate_ref[...] * gelu


def _peer_hidden(rows_flat, x_tiles, shift, gate, table, tm):
    n = x_tiles.shape[0]
    slot_blk = pl.BlockSpec((SLOTS, tm), lambda i: (0, i))
    return pl.pallas_call(
        _peer_hidden_kernel,
        grid=(n // tm,),
        in_specs=[pl.BlockSpec((tm * SLOTS,), lambda i: (i,), memory_space=pltpu.SMEM),
                  pl.BlockSpec((tm, *ROW_TILE), lambda i: (i, 0, 0)),
                  slot_blk, slot_blk,
                  pl.BlockSpec(memory_space=pltpu.VMEM)],
        out_specs=slot_blk,
        out_shape=jax.ShapeDtypeStruct((SLOTS, n), F32),
        scratch_shapes=[pltpu.VMEM((2, SLOTS, LANES), I32), pltpu.VMEM((tm, SLOTS, LANES), F32),
                        pltpu.VMEM((SLOTS, tm), F32)],
        compiler_params=_cparams(("arbitrary",)),
        name="peer_hidden",
    )(rows_flat, x_tiles, shift, gate, table)


N_ACC = 4


def _peer_combine_kernel(row_ref, shift_ref, coef_ref, tab_ref, o_ref, shv_ref, cv_ref):
    tm = o_ref.shape[0]
    lane = lax.broadcasted_iota(I32, (SLOTS, tm), 1)

    def columns(n):
        return _token_column(shift_ref, n, lane).astype(I32), _token_column(coef_ref, n, lane)

    def combine(n, buf):
        base = n * SLOTS
        accs = [None] * N_ACC
        for j in range(SLOTS):
            term = jnp.broadcast_to(cv_ref[buf, j:j + 1, :], ROW_TILE) * _expert_row(
                tab_ref, row_ref[base + j], shv_ref[buf, j:j + 1, :])
            accs[j % N_ACC] = term if accs[j % N_ACC] is None else accs[j % N_ACC] + term
        o_ref[n] = (accs[0] + accs[1]) + (accs[2] + accs[3])

    shv_ref[0], cv_ref[0] = columns(0)

    def token_pair(i, carry):
        for k in range(2):
            n = 2 * i + k
            nxt = columns(n + 1)
            combine(n, k)
            shv_ref[1 - k], cv_ref[1 - k] = nxt
        return carry
    lax.fori_loop(0, tm // 2, token_pair, 0)


def _peer_combine(rows_flat, shift, coef, table, n, tm):
    slot_blk = pl.BlockSpec((SLOTS, tm), lambda i: (0, i))
    return pl.pallas_call(
        _peer_combine_kernel,
        grid=(n // tm,),
        in_specs=[pl.BlockSpec((tm * SLOTS,), lambda i: (i,), memory_space=pltpu.SMEM), slot_blk, slot_blk,
                  pl.BlockSpec(memory_space=pltpu.VMEM)],
        out_specs=pl.BlockSpec((tm, *ROW_TILE), lambda i: (i, 0, 0)),
        out_shape=jax.ShapeDtypeStruct((n, *ROW_TILE), F32),
        scratch_shapes=[pltpu.VMEM((2, SLOTS, LANES), I32), pltpu.VMEM((2, SLOTS, LANES), F32)],
        compiler_params=_cparams(("arbitrary",)),
        name="peer_combine",
    )(rows_flat, shift, coef, table)


def _ln3_kernel(h_ref, p_ref, g_ref, b_ref, y_ref):
    y_ref[...] = _layer_norm(DN_ALPHA * h_ref[...] + p_ref[...], g_ref[...], b_ref[...])


def _ln3(h2d, peer2d, g, b, tm):
    n, d = h2d.shape
    tok = pl.BlockSpec((tm, d), lambda i: (i, 0))
    vec = pl.BlockSpec((1, d), lambda i: (0, 0))
    return pl.pallas_call(
        _ln3_kernel,
        grid=(n // tm,),
        in_specs=[tok, tok, vec, vec],
        out_specs=tok,
        out_shape=jax.ShapeDtypeStruct((n, d), F32),
        compiler_params=_cparams(("parallel",)),
        name="ln3",
    )(h2d, peer2d, g.reshape(1, d), b.reshape(1, d))


def _peer_ffn_ln(h2d, w_pq, keys_a, keys_b, tab_u, tab_v, ln3, tm_route, tm_gather):
    n, d = h2d.shape
    rows, shift, gate = _peer_route(h2d, w_pq, keys_a, keys_b, tm_route)
    rows_flat = rows.T.reshape(-1)
    coef = _peer_hidden(rows_flat, h2d.reshape(n, *ROW_TILE), shift, gate, tab_u, tm_gather)
    out = _peer_combine(rows_flat, shift, coef, tab_v, n, tm_gather)
    return _ln3(h2d, out.reshape(n, d), ln3[0], ln3[1], tm_route)


TOKEN_TILE = 512
ROUTE_TILE = 256
GATHER_TILE = 128
SAMPLE_TPAD = 16
PAGES_PER_STEP = 8


def _pad_axis1(a, rows):
    return jnp.pad(a, ((0, 0), (0, rows - a.shape[1])) + ((0, 0),) * (a.ndim - 2))


def kernel(x_prompt, x_sample, mem_prompt, state_ret, cache_k, cache_v, cache_idx_k, cache_mem_k, cache_mem_v,
           page_table, w_in, ret_gn_g, ret_gn_b, w_out, ln1_g, ln1_b, w_cq, w_ck, w_cv, w_co, ln2_g, ln2_b,
           w_pq, peer_keys_a, peer_keys_b, peer_u, peer_v, ln3_g, ln3_b):
    nb, t_len, d = x_prompt.shape
    db, ds, _ = x_sample.shape
    n_pages = page_table.shape[1]
    past = n_pages * PAGE_SIZE
    bf = lambda w: w.astype(BF16)

    w_in_p = _pad_w_in(w_in)
    w_out_b, w_cq_b, w_co_b, w_pq_b = bf(w_out), bf(w_cq), bf(w_co), bf(w_pq)
    keys_a, keys_b = bf(peer_keys_a), bf(peer_keys_b)
    tab_u, tab_v = _pack_expert_table(peer_u), _pack_expert_table(peer_v)
    ln1, ln2, ln3 = (ln1_g, ln1_b), (ln2_g, ln2_b), (ln3_g, ln3_b)

    def tail(x, ret_y, att_o, mem_k, mem_v, tm):
        h2 = _tail1(x, ret_y, att_o, mem_k, mem_v, w_out_b, w_cq_b, w_co_b, ln1, ln2, tm)
        return h2

    def peer(h2d):
        return _peer_ffn_ln(h2d, w_pq_b, keys_a, keys_b, tab_u, tab_v, ln3, ROUTE_TILE, GATHER_TILE)

    n = nb * t_len
    tabs = _rope_tables(jnp.arange(t_len, dtype=I32))
    rq, rk, rv, rg, aq, ak, akb, av, avt, iq, ikw, ik2, iwt = _project(
        x_prompt.reshape(n, d), w_in_p, tabs, t_len // TOKEN_TILE, TOKEN_TILE)
    ret_y, p_state = _retention(rq, rk, rv, rg, jnp.zeros((nb, RET_HEADS, RET_DK, RET_DV), F32),
                                _retention_tables(RET_CHUNK), ret_gn_g, ret_gn_b, nb, t_len // RET_CHUNK)
    att_o = _dsa_prompt(aq, iq, iwt, ik2, akb, avt, nb, t_len, min(IDX_TOPK_MAX, t_len // 4), kc=TOKEN_TILE)
    mem_kv = _matmul(mem_prompt.reshape(nb * MEM_TOKENS, d), jnp.concatenate([bf(w_ck), bf(w_cv)], axis=1), MEM_TOKENS)
    p_mem_k = mem_kv[:, :d].reshape(nb, MEM_TOKENS, d)
    p_mem_v = mem_kv[:, d:].reshape(nb, MEM_TOKENS, d)
    h2 = tail(x_prompt, ret_y.reshape(nb, t_len, RET_W), att_o.reshape(nb, t_len, ATT_W), p_mem_k, p_mem_v, TOKEN_TILE)
    y_prompt = peer(h2.reshape(n, d)).reshape(nb, t_len, d)

    ns = db * ds
    pos_s = jnp.tile(past + jnp.arange(ds, dtype=I32), db)
    sq, sk, sv, sg, saq, sak, _, sav, _, siq, sikw, _, _ = _project(
        x_sample.reshape(ns, d), w_in_p, _rope_tables(pos_s), 1, ns)
    per_seq = lambda a: a.reshape(db, ds, a.shape[-1])
    chunk = lambda a: _pad_axis1(per_seq(a), RET_CHUNK).reshape(db * RET_CHUNK, a.shape[-1])
    ret_y_s, s_state = _retention(chunk(sq), chunk(sk), chunk(sv), chunk(sg), state_ret,
                                  _retention_tables(ds), ret_gn_g, ret_gn_b, db, 1)
    ret_y_s = ret_y_s.reshape(db, RET_CHUNK, RET_W)[:, :SAMPLE_TPAD]

    pt_flat = page_table.reshape(-1)
    iq_rows = _pad_axis1(per_seq(siq), QPAD).reshape(db, QPAD * IDX_HEADS, IDX_DIM)
    w_rows = _pad_axis1(per_seq(sikw[:, IDX_DIM:IDX_DIM + IDX_HEADS]), QPAD).reshape(db, QPAD * IDX_HEADS, 1)
    ik_new = _pad_axis1(per_seq(sikw[:, :IDX_DIM]), LANES)
    sel = _dsa_sample_select(pt_flat, iq_rows, w_rows, ik_new, cache_idx_k, db, n_pages, ds,
                             min(IDX_TOPK_MAX, (past + ds) // 4), PAGES_PER_STEP)
    page_rows = lambda a: a.reshape(a.shape[0], PAGE_SIZE * ATT_HEADS, ATT_HD)
    new_rows = lambda a: page_rows(_pad_axis1(per_seq(a), PAGE_SIZE))
    q_rows = _pad_axis1(per_seq(saq).astype(F32), QPAD).reshape(db, QPAD, ATT_HEADS, ATT_HD)
    q_rows = q_rows.transpose(0, 2, 1, 3).reshape(db, ATT_HEADS * QPAD, ATT_HD)
    att_s = _dsa_sample_attend(pt_flat, q_rows, sel, new_rows(sak), new_rows(sav),
                               page_rows(cache_k), page_rows(cache_v), db, n_pages, PAGES_PER_STEP)
    att_s = att_s.reshape(db, ATT_HEADS, QPAD, ATT_HD).transpose(0, 2, 1, 3).reshape(db, QPAD, ATT_W)
    h2_s = tail(_pad_axis1(x_sample, SAMPLE_TPAD), ret_y_s, _pad_axis1(att_s, SAMPLE_TPAD),
                cache_mem_k, cache_mem_v, SAMPLE_TPAD)
    y_sample = peer(h2_s[:, :ds].reshape(ns, d)).reshape(db, ds, d)

    heads = lambda a, b_, t_: a.reshape(b_, t_, ATT_HEADS, ATT_HD)
    mem_heads = lambda a: a.reshape(nb, MEM_TOKENS, MEM_HEADS, MEM_HD)
    return (y_prompt, y_sample, p_state, heads(ak, nb, t_len), heads(av, nb, t_len),
            ikw[:, :IDX_DIM].reshape(nb, t_len, IDX_DIM), mem_heads(p_mem_k), mem_heads(p_mem_v),
            s_state, heads(sak, db, ds), heads(sav, db, ds), sikw[:, :IDX_DIM].reshape(db, ds, IDX_DIM))
```

```python
import functools

import jax
import jax.numpy as jnp
from jax import lax
from jax.experimental import pallas as pl
from jax.experimental.pallas import tpu as pltpu

F32 = jnp.float32
BF16 = jnp.bfloat16
I32 = jnp.int32

D_MODEL = 1024
DEPTH = 1
PAST_LEN = 8192
PAGE_SIZE = 128
RET_HEADS = 4
RET_DK = 128
RET_DV = 128
RET_W = RET_HEADS * RET_DV
RET_CHUNK = 128
ATT_HEADS = 4
ATT_HD = 128
ATT_W = ATT_HEADS * ATT_HD
IDX_HEADS = 8
IDX_DIM = 64
IDX_TOPK_MAX = 256
MEM_TOKENS = 256
MEM_HEADS = 4
MEM_HD = D_MODEL // MEM_HEADS
PEER_HEADS = 8
PEER_NKEYS = 128
PEER_EXPERTS = PEER_NKEYS * PEER_NKEYS
PEER_QDIM = 256
PEER_HALF = PEER_QDIM // 2
PEER_TOPK = 16
ROPE_THETA = 10000.0
LN_EPS = 1e-5
GN_EPS = 1e-5
DN_ALPHA = (2 * DEPTH) ** 0.25

LANES = 128
SUBLANES = 8
GROUP_W = 512
N_GROUPS = 8
IN_COLS_PAD = N_GROUPS * GROUP_W + LANES
VMEM_LIMIT = 56 * 1024 * 1024
INT_MIN = -(2 ** 31)
NEG_INF = float("-inf")

_NT = (((1,), (1,)), ((), ()))


def _cparams(sem):
    return pltpu.CompilerParams(dimension_semantics=sem, vmem_limit_bytes=VMEM_LIMIT)


def _rope_tables(pos):
    def tab(d):
        half = d // 2
        inv = ROPE_THETA ** (-jnp.arange(half, dtype=F32) * (2.0 / d))
        ang = pos.astype(F32)[:, None] * inv[None, :]
        cos, sin = jnp.cos(ang), jnp.sin(ang)
        reps = LANES // d
        return (jnp.tile(jnp.concatenate([cos, cos], axis=1), (1, reps)),
                jnp.tile(jnp.concatenate([-sin, sin], axis=1), (1, reps)))
    c128, s128 = tab(ATT_HD)
    c64, s64 = tab(IDX_DIM)
    return c128, s128, c64, s64


def _proj_kernel(x_ref, w_ref, wvt_ref, wwt_ref, c128_ref, s128_ref, c64_ref, s64_ref,
                 rq_ref, rk_ref, rv_ref, rg_ref, aq_ref, ak_ref, akb_ref, av_ref, avt_ref,
                 iq_ref, ikw_ref, ik2_ref, iwt_ref):
    xb = x_ref[...].astype(BF16)
    c128 = c128_ref[...]
    s128 = s128_ref[...]
    c64 = c64_ref[...]
    s64 = s64_ref[...]
    tm = xb.shape[0]
    lane = lax.broadcasted_iota(I32, (tm, LANES), 1)
    first_half64 = (lane & (IDX_DIM // 2)) == 0

    def group(g, width=GROUP_W):
        return jnp.dot(xb, w_ref[:, g * GROUP_W:g * GROUP_W + width], preferred_element_type=F32)

    def rope128(p):
        return p * c128 + pltpu.roll(p, ATT_HD // 2, 1) * s128

    def rope64(p):
        rot = jnp.where(first_half64, pltpu.roll(p, LANES - IDX_DIM // 2, 1), pltpu.roll(p, IDX_DIM // 2, 1))
        return p * c64 + rot * s64

    p = group(0)
    for h in range(RET_HEADS):
        sl = slice(h * LANES, (h + 1) * LANES)
        rq_ref[:, sl] = rope128(p[:, sl]).astype(BF16)
    p = group(1)
    for h in range(RET_HEADS):
        sl = slice(h * LANES, (h + 1) * LANES)
        rk_ref[:, sl] = (rope128(p[:, sl]) * (RET_DK ** -0.5)).astype(BF16)
    rv_ref[...] = group(2).astype(BF16)
    rg_ref[...] = group(3)
    p = group(4)
    for h in range(ATT_HEADS):
        sl = slice(h * LANES, (h + 1) * LANES)
        aq_ref[:, sl] = rope128(p[:, sl]).astype(BF16)
    p = group(5)
    for h in range(ATT_HEADS):
        sl = slice(h * LANES, (h + 1) * LANES)
        r = rope128(p[:, sl])
        ak_ref[:, sl] = r
        akb_ref[:, sl] = r.astype(BF16)
    p = group(6)
    av_ref[...] = p
    avt_ref[0] = lax.dot_general(wvt_ref[...], xb, _NT, preferred_element_type=F32).astype(BF16)
    iwt_ref[...] = lax.dot_general(wwt_ref[...], xb, _NT, preferred_element_type=F32)[:IDX_HEADS]
    p = group(7)
    for j in range(GROUP_W // LANES):
        sl = slice(j * LANES, (j + 1) * LANES)
        iq_ref[:, sl] = rope64(p[:, sl]).astype(BF16)
    p = group(8, LANES)
    is_key = lane < IDX_DIM
    r = jnp.where(is_key, rope64(p), p)
    ikw_ref[...] = r
    ik2_ref[...] = jnp.where(is_key, r, pltpu.roll(r, IDX_DIM, 1)).astype(BF16)


def _project(x2d, w_pad, tables, period_tiles, tm):
    n = x2d.shape[0]
    nt = n // tm
    row = lambda i: (i, 0)
    tab = lambda i: (i % period_tiles, 0)
    const = lambda i: (0, 0)
    flat = lambda w, dt: (pl.BlockSpec((tm, w), row), jax.ShapeDtypeStruct((n, w), dt))
    outs = [
        flat(GROUP_W, BF16), flat(GROUP_W, BF16), flat(GROUP_W, BF16), flat(GROUP_W, F32),
        flat(GROUP_W, BF16), flat(GROUP_W, F32), flat(GROUP_W, BF16), flat(GROUP_W, F32),
        (pl.BlockSpec((1, ATT_W, tm), lambda i: (i, 0, 0)), jax.ShapeDtypeStruct((nt, ATT_W, tm), BF16)),
        flat(GROUP_W, BF16), flat(LANES, F32), flat(LANES, BF16),
        (pl.BlockSpec((IDX_HEADS, tm), lambda i: (0, i)), jax.ShapeDtypeStruct((IDX_HEADS, n), F32)),
    ]
    v_cols = slice(6 * GROUP_W, 7 * GROUP_W)
    w_cols = slice(N_GROUPS * GROUP_W + IDX_DIM, N_GROUPS * GROUP_W + IDX_DIM + 2 * IDX_HEADS)
    return pl.pallas_call(
        _proj_kernel,
        grid=(nt,),
        in_specs=[pl.BlockSpec((tm, D_MODEL), row),
                  pl.BlockSpec((D_MODEL, IN_COLS_PAD), const),
                  pl.BlockSpec((GROUP_W, D_MODEL), const), pl.BlockSpec((2 * IDX_HEADS, D_MODEL), const)]
                 + [pl.BlockSpec((tm, LANES), tab)] * 4,
        out_specs=[spec for spec, _ in outs],
        out_shape=[shape for _, shape in outs],
        compiler_params=_cparams(("parallel",)),
        name="proj_rope",
    )(x2d, w_pad, w_pad[:, v_cols].T, w_pad[:, w_cols].T, *tables)


def _pad_w_in(w_in):
    return jnp.pad(w_in.astype(BF16), ((0, 0), (0, IN_COLS_PAD - w_in.shape[1])))


def _retention_log_decay():
    return jnp.log(1.0 - 2.0 ** (-5.0 - jnp.arange(RET_HEADS, dtype=F32)))


def _retention_tables(chunk_len):
    log_g = _retention_log_decay()
    j = jnp.arange(RET_CHUNK, dtype=F32)
    diff = j[:, None] - j[None, :]
    dmat = jnp.where(diff >= 0, jnp.exp(log_g[:, None, None] * jnp.maximum(diff, 0.0)), 0.0)
    q_dec = jnp.exp(log_g[None, :] * (j[:, None] + 1.0))
    k_dec = jnp.exp(log_g[None, :] * (chunk_len - 1.0 - j[:, None]))
    k_dec = jnp.where(j[:, None] < chunk_len, k_dec, 0.0)
    s_dec = jnp.exp(log_g * chunk_len)
    widen = lambda t: jnp.repeat(t, LANES, axis=1)
    return dmat, widen(q_dec), widen(k_dec), s_dec


def _retention_kernel(sdec_ref, q_ref, k_ref, v_ref, g_ref, s0_ref, dmat_ref, qdec_ref, kdec_ref,
                      gng_ref, gnb_ref, y_ref, sout_ref, state_ref):
    c = pl.program_id(1)

    @pl.when(c == 0)
    def _():
        state_ref[...] = s0_ref[0]

    for h in range(RET_HEADS):
        sl = slice(h * LANES, (h + 1) * LANES)
        q = q_ref[:, sl]
        k = k_ref[:, sl]
        v = v_ref[:, sl]
        s_prev = state_ref[h]
        scores = lax.dot_general(q, k, _NT, preferred_element_type=F32) * dmat_ref[h]
        o = jnp.dot(scores.astype(BF16), v, preferred_element_type=F32)
        o = o + jnp.dot(q, s_prev.astype(BF16), preferred_element_type=F32) * qdec_ref[:, sl]
        kd_t = (k.astype(F32) * kdec_ref[:, sl]).T.astype(BF16)
        state_ref[h] = s_prev * sdec_ref[h] + jnp.dot(kd_t, v, preferred_element_type=F32)
        mu = jnp.mean(o, axis=-1, keepdims=True)
        var = jnp.mean(jnp.square(o - mu), axis=-1, keepdims=True)
        n = (o - mu) * lax.rsqrt(var + GN_EPS) * gng_ref[:, sl] + gnb_ref[:, sl]
        g = g_ref[:, sl]
        y_ref[:, sl] = (g * (1.0 / (1.0 + jnp.exp(-g))) * n).astype(BF16)

    @pl.when(c == pl.num_programs(1) - 1)
    def _():
        sout_ref[0] = state_ref[...]


def _retention(q, k, v, g, s0, tables, gn_g, gn_b, nb, nc):
    dmat, qdec, kdec, sdec = tables
    n = q.shape[0]
    row = lambda b, c: (b * nc + c, 0)
    const2 = lambda b, c: (0, 0)
    blk = pl.BlockSpec((RET_CHUNK, RET_W), row)
    return pl.pallas_call(
        _retention_kernel,
        grid=(nb, nc),
        in_specs=[pl.BlockSpec(memory_space=pltpu.SMEM), blk, blk, blk, blk,
                  pl.BlockSpec((1, RET_HEADS, RET_DK, RET_DV), lambda b, c: (b, 0, 0, 0)),
                  pl.BlockSpec((RET_HEADS, RET_CHUNK, RET_CHUNK), lambda b, c: (0, 0, 0)),
                  pl.BlockSpec((RET_CHUNK, RET_W), const2), pl.BlockSpec((RET_CHUNK, RET_W), const2),
                  pl.BlockSpec((1, RET_W), const2), pl.BlockSpec((1, RET_W), const2)],
        out_specs=[blk, pl.BlockSpec((1, RET_HEADS, RET_DK, RET_DV), lambda b, c: (b, 0, 0, 0))],
        out_shape=[jax.ShapeDtypeStruct((n, RET_W), BF16),
                   jax.ShapeDtypeStruct((nb, RET_HEADS, RET_DK, RET_DV), F32)],
        scratch_shapes=[pltpu.VMEM((RET_HEADS, RET_DK, RET_DV), F32)],
        compiler_params=_cparams(("parallel", "arbitrary")),
        name="retention",
    )(sdec, q, k, v, g, s0, dmat, qdec, kdec, gn_g.reshape(1, RET_W), gn_b.reshape(1, RET_W))


def _sort_key(x):
    bits = pltpu.bitcast(x, I32)
    return bits ^ ((bits >> 31) & 0x7FFFFFFF)


def _kth_largest_key(count_ge, shape, topk):
    def bit_body(bi, t):
        cand = t + lax.shift_left(jnp.int32(1), 31 - bi)
        return jnp.where(count_ge(cand) >= topk, cand, t)
    return lax.fori_loop(0, 32, bit_body, jnp.full(shape, INT_MIN, I32))


def _tie_limit(count_eq_below, need, shape, pos_bits):
    def bit_body(bi, p):
        cand = p + lax.shift_left(jnp.int32(1), pos_bits - 1 - bi)
        return jnp.where(count_eq_below(cand) <= need, cand, p)
    return lax.fori_loop(0, pos_bits, bit_body, jnp.zeros(shape, I32))


def _dsa_prompt_kernel(aq_ref, iq_ref, iwt_ref, ik2_ref, ak_ref, avt_ref, o_ref, keys_ref, bias_ref, *,
                       tq, kc, topk, pos_bits):
    i = pl.program_id(1)
    q0 = i * tq
    nk = (q0 + tq + kc - 1) // kc
    groups = kc // SUBLANES
    state = (SUBLANES, tq)
    lane = lax.broadcasted_iota(I32, (tq, LANES), 1)
    key_in_chunk = (lax.broadcasted_iota(I32, (groups, SUBLANES, tq), 0) * SUBLANES
                    + lax.broadcasted_iota(I32, (groups, SUBLANES, tq), 1))
    qpos = q0 + lax.broadcasted_iota(I32, (groups, SUBLANES, tq), 2)
    iq = iq_ref[...]
    iwt = iwt_ref[...]

    def score_chunk(c, carry):
        k0 = pl.multiple_of(c * kc, kc)
        ikc = ik2_ref[pl.ds(k0, kc), :]
        acc = None
        for h in range(IDX_HEADS):
            pair = iq[:, (h // 2) * LANES:(h // 2 + 1) * LANES]
            keep = (lane < IDX_DIM) if h % 2 == 0 else (lane >= IDX_DIM)
            qh = jnp.where(keep, pair, jnp.zeros_like(pair))
            s = lax.dot_general(ikc, qh, _NT, preferred_element_type=F32)
            term = jnp.maximum(s, 0.0) * iwt[h:h + 1, :]
            acc = term if acc is None else acc + term
        key = _sort_key(acc).reshape(groups, SUBLANES, tq)
        keys_ref[c] = jnp.where(k0 + key_in_chunk <= qpos, key, INT_MIN)
        return carry
    lax.fori_loop(0, nk, score_chunk, 0)

    def total(a):
        return jnp.broadcast_to(jnp.sum(a, axis=0, keepdims=True), state)

    def count_ge(cand):
        body = lambda c, a: a + jnp.sum(jnp.where(keys_ref[c] >= cand[None], 1, 0), axis=0)
        return total(lax.fori_loop(0, nk, body, jnp.zeros(state, I32)))

    t = _kth_largest_key(count_ge, state, topk)
    n_ge = count_ge(t)
    need = topk - count_ge(t + 1)

    def count_eq_below(p):
        def body(c, a):
            hit = (keys_ref[c] == t[None]) & (c * kc + key_in_chunk < p[None])
            return a + jnp.sum(jnp.where(hit, 1, 0), axis=0)
        return total(lax.fori_loop(0, nk, body, jnp.zeros(state, I32)))

    has_excess = jnp.max(jnp.where(n_ge > topk, 1, 0)) > 0
    plim = lax.cond(has_excess,
                    lambda: _tie_limit(count_eq_below, need, state, pos_bits),
                    lambda: jnp.full(state, 2 ** pos_bits, I32))

    def bias_chunk(c, carry):
        key = keys_ref[c]
        sel = (key > t[None]) | ((key == t[None]) & (c * kc + key_in_chunk < plim[None]))
        bias_ref[c] = jnp.where(sel & (key != INT_MIN), 0.0, NEG_INF)
        return carry
    lax.fori_loop(0, nk, bias_chunk, 0)

    scale = ATT_HD ** -0.5
    head_cols = [slice(h * LANES, (h + 1) * LANES) for h in range(ATT_HEADS)]
    queries = [aq_ref[:, sl] for sl in head_cols]

    def attend_chunk(c, carry):
        k0 = pl.multiple_of(c * kc, kc)
        bias = bias_ref[c].reshape(kc, tq)
        out = []
        for h, sl in enumerate(head_cols):
            m, l, acc = carry[h]
            s = lax.dot_general(ak_ref[pl.ds(k0, kc), sl], queries[h], _NT, preferred_element_type=F32)
            s = s * scale + bias
            m_new = jnp.maximum(m, jnp.max(s, axis=0, keepdims=True))
            m_safe = jnp.where(m_new == NEG_INF, 0.0, m_new)
            p = jnp.exp(s - m_safe)
            alpha = jnp.exp(m - m_safe)
            l_new = alpha * l + jnp.sum(p, axis=0, keepdims=True)
            acc_new = alpha * acc + jnp.dot(avt_ref[c, sl, :], p.astype(BF16), preferred_element_type=F32)
            out.append((m_new, l_new, acc_new))
        return tuple(out)

    init = tuple((jnp.full((1, tq), NEG_INF, F32), jnp.zeros((1, tq), F32), jnp.zeros((ATT_HD, tq), F32))
                 for _ in head_cols)
    fin = lax.fori_loop(0, nk, attend_chunk, init)
    for (m, l, acc), sl in zip(fin, head_cols):
        o_ref[:, sl] = (acc / l).T.astype(BF16)


def _dsa_prompt(aq, iq, iwt, ik2, akb, avt, nb, t_len, topk, tq=LANES, kc=512):
    n = aq.shape[0]
    nq = t_len // tq
    qrow = lambda b, i: (b * nq + i, 0)
    brow = lambda b, i: (b, 0)
    kern = functools.partial(_dsa_prompt_kernel, tq=tq, kc=kc, topk=topk, pos_bits=(t_len - 1).bit_length() + 1)
    chunked = (t_len // kc, kc // SUBLANES, SUBLANES, tq)
    return pl.pallas_call(
        kern,
        grid=(nb, nq),
        in_specs=[pl.BlockSpec((tq, ATT_W), qrow), pl.BlockSpec((tq, GROUP_W), qrow),
                  pl.BlockSpec((IDX_HEADS, tq), lambda b, i: (0, b * nq + i)),
                  pl.BlockSpec((t_len, LANES), brow), pl.BlockSpec((t_len, ATT_W), brow),
                  pl.BlockSpec((t_len // kc, ATT_W, kc), lambda b, i: (b, 0, 0))],
        out_specs=pl.BlockSpec((tq, ATT_W), qrow),
        out_shape=jax.ShapeDtypeStruct((n, ATT_W), BF16),
        scratch_shapes=[pltpu.VMEM(chunked, I32), pltpu.VMEM(chunked, F32)],
        compiler_params=_cparams(("parallel", "arbitrary")),
        name="dsa_prompt",
    )(aq, iq, iwt, ik2, akb, avt)


QPAD = SUBLANES


def _dsa_sample_select_kernel(pt_ref, iq_ref, w_ref, iknew_ref, expand_ref, *rest, pg, n_pages, n_new, topk,
                              pos_bits):
    idx_refs = rest[:pg]
    sel_ref = rest[pg]
    keys_ref = rest[pg + 1]
    p = pl.program_id(1)
    iq = iq_ref[0]
    w = w_ref[0]
    key_in_block = lax.broadcasted_iota(I32, (QPAD, LANES), 1)
    row = lax.broadcasted_iota(I32, (QPAD, LANES), 0)

    def scores(keys_t):
        s = jnp.dot(iq, keys_t.astype(BF16), preferred_element_type=F32)
        r = jnp.maximum(s, 0.0) * w
        return jnp.sum(r.reshape(QPAD, IDX_HEADS, LANES), axis=1)

    for j in range(pg):
        keys_ref[p * pg + j] = _sort_key(scores(idx_refs[j][0]))

    @pl.when(p == pl.num_programs(1) - 1)
    def _():
        s_new = scores(iknew_ref[0])
        keys_ref[n_pages] = jnp.where((key_in_block <= row) & (key_in_block < n_new), _sort_key(s_new), INT_MIN)
        keys_ref[n_pages + 1] = jnp.full((QPAD, LANES), INT_MIN, I32)
        keys = keys_ref[...]
        pos = lax.broadcasted_iota(I32, keys.shape, 0) * LANES + key_in_block[None]

        def lane_total(a):
            return jnp.broadcast_to(jnp.sum(a, axis=1, keepdims=True), (QPAD, LANES))

        def count_ge(cand):
            return lane_total(jnp.sum(jnp.where(keys >= cand[None], 1, 0), axis=0))

        t = _kth_largest_key(count_ge, (QPAD, LANES), topk)
        need = topk - count_ge(t + 1)

        def count_eq_below(lim):
            return lane_total(jnp.sum(jnp.where((keys == t[None]) & (pos < lim[None]), 1, 0), axis=0))

        has_excess = jnp.max(jnp.where(count_ge(t) > topk, 1, 0)) > 0
        plim = lax.cond(has_excess,
                        lambda: _tie_limit(count_eq_below, need, (QPAD, LANES), pos_bits),
                        lambda: jnp.full((QPAD, LANES), 2 ** pos_bits, I32))

        sel = ((keys > t[None]) | ((keys == t[None]) & (pos < plim[None]))) & (keys != INT_MIN)
        flags = jnp.where(sel, 1.0, 0.0).reshape(keys.shape[0] * QPAD, LANES).astype(BF16)
        out = jnp.dot(flags, expand_ref[...], preferred_element_type=F32)
        sel_ref[0] = out.reshape(keys.shape[0], QPAD, out.shape[1])


def _dsa_sample_select(pt_flat, iq_rows, w_rows, ik_new, cache_idx_k, nb, n_pages, n_new, topk, pg):
    kern = functools.partial(_dsa_sample_select_kernel, pg=pg, n_pages=n_pages, n_new=n_new, topk=topk,
                             pos_bits=((n_pages + 1) * LANES).bit_length())
    per_b = lambda shape: pl.BlockSpec((1,) + shape, lambda b, p, pt: (b,) + (0,) * len(shape))
    keys_minor = lambda a: a.transpose(0, 2, 1)
    page = lambda j: pl.BlockSpec((1, IDX_DIM, PAGE_SIZE), lambda b, p, pt: (pt[b * n_pages + p * pg + j], 0, 0))
    cols = PAGE_SIZE * ATT_HEADS
    expand = (jnp.arange(cols)[None, :] // ATT_HEADS == jnp.arange(PAGE_SIZE)[:, None]).astype(BF16)
    nblk = n_pages + 2
    return pl.pallas_call(
        kern,
        grid_spec=pltpu.PrefetchScalarGridSpec(
            num_scalar_prefetch=1,
            grid=(nb, n_pages // pg),
            in_specs=[per_b((QPAD * IDX_HEADS, IDX_DIM)), per_b((QPAD * IDX_HEADS, 1)), per_b((IDX_DIM, PAGE_SIZE)),
                      pl.BlockSpec((PAGE_SIZE, cols), lambda b, p, pt: (0, 0))]
                     + [page(j) for j in range(pg)],
            out_specs=per_b((nblk, QPAD, cols)),
            scratch_shapes=[pltpu.VMEM((nblk, QPAD, LANES), I32)]),
        out_shape=jax.ShapeDtypeStruct((nb, nblk, QPAD, cols), F32),
        compiler_params=_cparams(("parallel", "arbitrary")),
        name="dsa_sample_select",
    )(pt_flat, iq_rows, w_rows, keys_minor(ik_new), expand, *([keys_minor(cache_idx_k)] * pg))


def _dsa_sample_attend_kernel(pt_ref, q_ref, sel_ref, knew_ref, vnew_ref, *rest, pg, n_pages):
    k_refs = rest[:pg]
    v_refs = rest[pg:2 * pg]
    o_ref = rest[2 * pg]
    m_ref, l_ref, acc_ref = rest[2 * pg + 1:]
    p = pl.program_id(1)
    scale = ATT_HD ** -0.5
    rows = ATT_HEADS * QPAD
    cols = PAGE_SIZE * ATT_HEADS

    @pl.when(p == 0)
    def _():
        m_ref[...] = jnp.full(m_ref.shape, NEG_INF, F32)
        l_ref[...] = jnp.zeros(l_ref.shape, F32)
        acc_ref[...] = jnp.zeros(acc_ref.shape, F32)

    q = q_ref[0].astype(BF16)
    own_head = ((lax.broadcasted_iota(I32, (rows, cols), 1) & (ATT_HEADS - 1))
                == (lax.broadcasted_iota(I32, (rows, cols), 0) >> (QPAD.bit_length() - 1)))

    def update(blocks):
        scores = []
        for flags, kp, _ in blocks:
            s = lax.dot_general(q, kp.astype(BF16), _NT, preferred_element_type=F32) * scale
            mask = own_head & (jnp.concatenate([flags] * ATT_HEADS, axis=0) > 0.5)
            scores.append(jnp.where(mask, s, NEG_INF))
        mc = scores[0]
        for s in scores[1:]:
            mc = jnp.maximum(mc, s)
        m = m_ref[...]
        m_new = jnp.maximum(m, jnp.max(mc, axis=1, keepdims=True))
        m_safe = jnp.where(m_new == NEG_INF, 0.0, m_new)
        m_row = m_safe[:, :1]
        psum = jnp.zeros((rows, cols), F32)
        acc = jnp.exp(m - m_safe) * acc_ref[...]
        for (_, _, vp), s in zip(blocks, scores):
            pr = jnp.exp(s - m_row)
            psum = psum + pr
            acc = acc + jnp.dot(pr.astype(BF16), vp.astype(BF16), preferred_element_type=F32)
        l_ref[...] = jnp.exp(m - m_safe) * l_ref[...] + jnp.sum(psum, axis=1, keepdims=True)
        acc_ref[...] = acc
        m_ref[...] = m_new

    update([(sel_ref[0, p * pg + j], k_refs[j][0], v_refs[j][0]) for j in range(pg)])

    @pl.when(p == pl.num_programs(1) - 1)
    def _():
        update([(sel_ref[0, n_pages], knew_ref[0], vnew_ref[0])])
        o_ref[0] = acc_ref[...] / l_ref[...]


def _dsa_sample_attend(pt_flat, q_rows, sel, k_new, v_new, cache_k, cache_v, nb, n_pages, pg):
    kern = functools.partial(_dsa_sample_attend_kernel, pg=pg, n_pages=n_pages)
    rows = ATT_HEADS * QPAD
    cols = PAGE_SIZE * ATT_HEADS
    per_b = lambda shape: pl.BlockSpec((1,) + shape, lambda b, p, pt: (b,) + (0,) * len(shape))
    page = lambda j: pl.BlockSpec((1, cols, ATT_HD), lambda b, p, pt: (pt[b * n_pages + p * pg + j], 0, 0))
    return pl.pallas_call(
        kern,
        grid_spec=pltpu.PrefetchScalarGridSpec(
            num_scalar_prefetch=1,
            grid=(nb, n_pages // pg),
            in_specs=[per_b((rows, ATT_HD)), per_b((n_pages + 2, QPAD, cols)), per_b((cols, ATT_HD)),
                      per_b((cols, ATT_HD))] + [page(j) for j in range(pg)] * 2,
            out_specs=per_b((rows, ATT_HD)),
            scratch_shapes=[pltpu.VMEM((rows, LANES), F32)] * 3),
        out_shape=jax.ShapeDtypeStruct((nb, rows, ATT_HD), F32),
        compiler_params=_cparams(("parallel", "arbitrary")),
        name="dsa_sample_attend",
    )(pt_flat, q_rows, sel, k_new, v_new, *([cache_k] * pg), *([cache_v] * pg))


def _layer_norm(x, g, b):
    mu = jnp.mean(x, axis=-1, keepdims=True)
    var = jnp.mean(jnp.square(x - mu), axis=-1, keepdims=True)
    return (x - mu) * lax.rsqrt(var + LN_EPS) * g + b


def _matmul_kernel(x_ref, w_ref, o_ref):
    o_ref[...] = jnp.dot(x_ref[...].astype(BF16), w_ref[...], preferred_element_type=F32)


def _matmul(x, w_bf16, tm):
    n, kdim = x.shape
    m = w_bf16.shape[1]
    return pl.pallas_call(
        _matmul_kernel,
        grid=(n // tm,),
        in_specs=[pl.BlockSpec((tm, kdim), lambda i: (i, 0)), pl.BlockSpec((kdim, m), lambda i: (0, 0))],
        out_specs=pl.BlockSpec((tm, m), lambda i: (i, 0)),
        out_shape=jax.ShapeDtypeStruct((n, m), F32),
        compiler_params=_cparams(("parallel",)),
        name="mem_proj",
    )(x, w_bf16)


def _tail1_kernel(x_ref, ry_ref, ao_ref, mk_ref, mv_ref, wout_ref, wcq_ref, wco_ref,
                  ln1g_ref, ln1b_ref, ln2g_ref, ln2b_ref, h2_ref, *, head_rows):
    x = x_ref[0]
    tm = x.shape[0]
    mix = jnp.dot(ry_ref[0].astype(BF16), wout_ref[:RET_W, :], preferred_element_type=F32)
    mix = mix + jnp.dot(ao_ref[0].astype(BF16), wout_ref[RET_W:, :], preferred_element_type=F32)
    h1 = _layer_norm(DN_ALPHA * x + mix, ln1g_ref[...], ln1b_ref[...])
    q = jnp.dot(h1.astype(BF16), wcq_ref[...], preferred_element_type=F32).astype(BF16)
    head_cols = [slice(h * MEM_HD, (h + 1) * MEM_HD) for h in range(MEM_HEADS)]
    if head_rows:
        q_rows = jnp.concatenate([q[:, sl] for sl in head_cols], axis=0)
        s = lax.dot_general(q_rows, mk_ref[0].astype(BF16), _NT, preferred_element_type=F32) * (MEM_HD ** -0.5)
        assert tm & (tm - 1) == 0, "stacked-head rows are split by a shift"
        own_head = ((lax.broadcasted_iota(I32, s.shape, 1) & (MEM_HEADS - 1))
                    == lax.broadcasted_iota(I32, s.shape, 0) >> (tm.bit_length() - 1))
        s = jnp.where(own_head, s, NEG_INF)
        e = jnp.exp(s - jnp.max(s, axis=-1, keepdims=True))
        p = e / jnp.sum(e, axis=-1, keepdims=True)
        o = jnp.dot(p.astype(BF16), mv_ref[0].astype(BF16), preferred_element_type=F32).astype(BF16)
        heads = [o[h * tm:(h + 1) * tm] for h in range(MEM_HEADS)]
    else:
        heads = []
        for sl in head_cols:
            mk = mk_ref[0, :, sl].astype(BF16)
            mv = mv_ref[0, :, sl].astype(BF16)
            s = lax.dot_general(q[:, sl], mk, _NT, preferred_element_type=F32) * (MEM_HD ** -0.5)
            e = jnp.exp(s - jnp.max(s, axis=-1, keepdims=True))
            p = e / jnp.sum(e, axis=-1, keepdims=True)
            heads.append(jnp.dot(p.astype(BF16), mv, preferred_element_type=F32).astype(BF16))
    ca = jnp.concatenate(heads, axis=1)
    h2 = DN_ALPHA * h1 + jnp.dot(ca, wco_ref[...], preferred_element_type=F32)
    h2_ref[0] = _layer_norm(h2, ln2g_ref[...], ln2b_ref[...])


def _tail1(x, ret_y, att_o, mem_k, mem_v, w_out, w_cq, w_co, ln1, ln2, tm):
    g, tg, d = x.shape
    head_rows = mem_k.ndim == 4
    if head_rows:
        mem_k = mem_k.reshape(g, MEM_TOKENS * MEM_HEADS, MEM_HD)
        mem_v = mem_v.reshape(g, MEM_TOKENS * MEM_HEADS, MEM_HD)
    tok = lambda w: pl.BlockSpec((1, tm, w), lambda b, i: (b, i, 0))
    mem = pl.BlockSpec((1,) + mem_k.shape[1:], lambda b, i: (b, 0, 0))
    wsp = lambda r, c: pl.BlockSpec((r, c), lambda b, i: (0, 0))
    vec = wsp(1, d)
    return pl.pallas_call(
        functools.partial(_tail1_kernel, head_rows=head_rows),
        grid=(g, tg // tm),
        in_specs=[tok(d), tok(RET_W), tok(ATT_W), mem, mem, wsp(d, d), wsp(d, d), wsp(d, d), vec, vec, vec, vec],
        out_specs=tok(d),
        out_shape=jax.ShapeDtypeStruct((g, tg, d), F32),
        compiler_params=_cparams(("parallel", "arbitrary")),
        name="tail1",
    )(x, ret_y, att_o, mem_k, mem_v, w_out, w_cq, w_co,
      ln1[0].reshape(1, d), ln1[1].reshape(1, d), ln2[0].reshape(1, d), ln2[1].reshape(1, d))


HALF_EXPERTS = PEER_EXPERTS // 2
HALF_BIT = HALF_EXPERTS.bit_length() - 1
ROW_TILE = (SUBLANES, LANES)
SLOTS = PEER_HEADS * PEER_TOPK
HIGH_HALF = -65536


PACK_PAIRS = 64


def _pack_kernel(lo_ref, hi_ref, o_ref):
    def bf16_bits(x):
        return pltpu.bitcast(x.astype(BF16).astype(F32), I32)

    for g in range(PACK_PAIRS // SUBLANES):
        rows = slice(g * SUBLANES, (g + 1) * SUBLANES)
        word = lax.shift_right_logical(bf16_bits(lo_ref[rows, :]), 16) | bf16_bits(hi_ref[rows, :])
        for s in range(SUBLANES):
            o_ref[pl.ds(g * SUBLANES * SUBLANES + s, SUBLANES, stride=SUBLANES), :] = word[:, s * LANES:(s + 1) * LANES]


def _pack_expert_table(w):
    d = w.shape[1]
    steps = HALF_EXPERTS // PACK_PAIRS
    return pl.pallas_call(
        _pack_kernel,
        grid=(steps,),
        in_specs=[pl.BlockSpec((PACK_PAIRS, d), lambda i: (i, 0)),
                  pl.BlockSpec((PACK_PAIRS, d), lambda i: (i + steps, 0))],
        out_specs=pl.BlockSpec((PACK_PAIRS * SUBLANES, LANES), lambda i: (i, 0)),
        out_shape=jax.ShapeDtypeStruct((HALF_EXPERTS * SUBLANES, LANES), I32),
        compiler_params=_cparams(("parallel",)),
        name="pack_table",
    )(w, w)


def _top_rows(s, k, payload=None, order=None):
    rows = s.shape[0]
    rid = lax.broadcasted_iota(I32, s.shape, 0)
    order = rid if order is None else order
    big = jnp.int32(2 ** 30)
    vals, pays = [], []
    for _ in range(k):
        m = jnp.max(s, axis=0, keepdims=True)
        win = jnp.min(jnp.where(s == m, order, big), axis=0, keepdims=True)
        hit = order == win
        vals.append(m)
        pays.append(win if payload is None else jnp.max(jnp.where(hit, payload, -1), axis=0, keepdims=True))
        s = jnp.where(hit, NEG_INF, s)
    return jnp.concatenate(vals, axis=0), jnp.concatenate(pays, axis=0)


def _peer_route_kernel(h_ref, wpq_ref, ka_ref, kb_ref, row_ref, shift_ref, gate_ref):
    hb = h_ref[...].astype(BF16)
    tm = hb.shape[0]
    ka = ka_ref[...]
    kb = kb_ref[...]
    a8 = lax.broadcasted_iota(I32, (SUBLANES, tm), 0)
    b16 = lax.broadcasted_iota(I32, (PEER_TOPK, tm), 0)
    for hd in range(PEER_HEADS):
        q = jnp.dot(hb, wpq_ref[:, hd * PEER_QDIM:(hd + 1) * PEER_QDIM], preferred_element_type=F32).astype(BF16)
        s1 = lax.dot_general(ka, q[:, :PEER_HALF], _NT, preferred_element_type=F32)
        s2 = lax.dot_general(kb, q[:, PEER_HALF:], _NT, preferred_element_type=F32)
        v1, i1 = _top_rows(s1, PEER_TOPK)
        v2, i2 = _top_rows(s2, PEER_TOPK)
        cand, cid, fid = [], [], []

        def block(vals, ids, flat, valid):
            cand.append(jnp.where(valid, vals, NEG_INF) if valid is not None else vals)
            cid.append(ids)
            fid.append(flat)

        block(v1[0:1] + v2, i1[0:1] * PEER_NKEYS + i2, b16, None)
        block(v1[1:2] + v2[:8], i1[1:2] * PEER_NKEYS + i2[:8], PEER_TOPK + a8, None)
        block(v1[8:] + v2[0:1], i1[8:] * PEER_NKEYS + i2[0:1], (a8 + 8) * PEER_TOPK, None)
        for b in range(5):
            lim = PEER_TOPK // (b + 1)
            block(v1[:8] + v2[b:b + 1], i1[:8] * PEER_NKEYS + i2[b:b + 1], a8 * PEER_TOPK + b,
                  (a8 >= 2) & (a8 < lim))
        sc, eid = _top_rows(jnp.concatenate(cand, axis=0), PEER_TOPK,
                            payload=jnp.concatenate(cid, axis=0), order=jnp.concatenate(fid, axis=0))
        e = jnp.exp(sc - sc[0:1])
        rows = slice(hd * PEER_TOPK, (hd + 1) * PEER_TOPK)
        gate_ref[rows, :] = e / jnp.sum(e, axis=0, keepdims=True)
        row_ref[rows, :] = (eid & (HALF_EXPERTS - 1)) * SUBLANES
        shift_ref[rows, :] = (16 - ((eid >> HALF_BIT) << 4)).astype(F32)


def _peer_route(h2d, w_pq, keys_a, keys_b, tm):
    n, d = h2d.shape
    slot_blk = pl.BlockSpec((SLOTS, tm), lambda i: (0, i))
    return pl.pallas_call(
        _peer_route_kernel,
        grid=(n // tm,),
        in_specs=[pl.BlockSpec((tm, d), lambda i: (i, 0)),
                  pl.BlockSpec((d, PEER_HEADS * PEER_QDIM), lambda i: (0, 0)),
                  pl.BlockSpec((PEER_NKEYS, PEER_HALF), lambda i: (0, 0)),
                  pl.BlockSpec((PEER_NKEYS, PEER_HALF), lambda i: (0, 0))],
        out_specs=[slot_blk, slot_blk, slot_blk],
        out_shape=[jax.ShapeDtypeStruct((SLOTS, n), I32), jax.ShapeDtypeStruct((SLOTS, n), F32),
                   jax.ShapeDtypeStruct((SLOTS, n), F32)],
        compiler_params=_cparams(("parallel",)),
        name="peer_route",
    )(h2d, w_pq, keys_a, keys_b)


def _token_column(ref, n, lane):
    col = jnp.sum(jnp.where(lane == n, ref[...], 0.0), axis=1, keepdims=True)
    return jnp.broadcast_to(col, (SLOTS, LANES))


def _expert_row(tab_ref, row, shift_row):
    word = tab_ref[pl.ds(pl.multiple_of(row, SUBLANES), SUBLANES), :]
    return pltpu.bitcast(lax.shift_left(word, jnp.broadcast_to(shift_row, ROW_TILE)) & jnp.int32(HIGH_HALF), F32)


FOLD_PITCH = 12
FOLD_ROWS = -(-((SUBLANES - 1) * (FOLD_PITCH + 1) + 1) // SUBLANES) * SUBLANES
TOKENS_PER_STEP = 4


def _peer_hidden_kernel(row_ref, x_ref, shift_ref, gate_ref, tab_ref, coef_ref, shv_ref, part_ref, hid_ref,
                        fold_ref):
    tm = x_ref.shape[0]
    lane = lax.broadcasted_iota(I32, (SLOTS, tm), 1)

    def shifts(n):
        return _token_column(shift_ref, n, lane).astype(I32)

    def gather(n, buf):
        x = x_ref[n]
        base = n * SLOTS
        for g in range(SLOTS // SUBLANES):
            slab = fold_ref.at[g % 2]
            for r in range(SUBLANES):
                j = g * SUBLANES + r
                prod = _expert_row(tab_ref, row_ref[base + j], shv_ref[buf, j:j + 1, :]) * x
                slab[pl.ds(r, SUBLANES, stride=FOLD_PITCH), :] = prod
            folded = slab[0:SUBLANES, :]
            for s in range(1, SUBLANES):
                folded = folded + slab[s * FOLD_PITCH:s * FOLD_PITCH + SUBLANES, :]
            part_ref[n, g * SUBLANES:(g + 1) * SUBLANES, :] = folded

    shv_ref[0] = shifts(0)

    def token_group(i, carry):
        for k in range(TOKENS_PER_STEP):
            n = TOKENS_PER_STEP * i + k
            nxt = shifts(n + 1)
            gather(n, k % 2)
            shv_ref[(k + 1) % 2] = nxt
        return carry
    lax.fori_loop(0, tm // TOKENS_PER_STEP, token_group, 0)

    hid_ref[...] = jnp.zeros((SLOTS, tm), F32)

    def reduce_tokens(i, carry):
        hid = hid_ref[...]
        for r in range(SUBLANES):
            n = i * SUBLANES + r
            hid = jnp.where(lane == n, jnp.sum(part_ref[n], axis=1, keepdims=True), hid)
        hid_ref[...] = hid
        return carry
    lax.fori_loop(0, tm // SUBLANES, reduce_tokens, 0)
    hid = hid_ref[...]
    gelu = 0.5 * hid * (1.0 + lax.erf(hid * (2.0 ** -0.5)))
    coef_ref[...] = gate_ref[...] * gelu


def _peer_hidden(rows_flat, x_tiles, shift, gate, table, tm):
    n = x_tiles.shape[0]
    slot_blk = pl.BlockSpec((SLOTS, tm), lambda i: (0, i))
    return pl.pallas_call(
        _peer_hidden_kernel,
        grid=(n // tm,),
        in_specs=[pl.BlockSpec((tm * SLOTS,), lambda i: (i,), memory_space=pltpu.SMEM),
                  pl.BlockSpec((tm, *ROW_TILE), lambda i: (i, 0, 0)),
                  slot_blk, slot_blk,
                  pl.BlockSpec(memory_space=pltpu.VMEM)],
        out_specs=slot_blk,
        out_shape=jax.ShapeDtypeStruct((SLOTS, n), F32),
        scratch_shapes=[pltpu.VMEM((2, SLOTS, LANES), I32), pltpu.VMEM((tm, SLOTS, LANES), F32),
                        pltpu.VMEM((SLOTS, tm), F32), pltpu.VMEM((2, FOLD_ROWS, LANES), F32)],
        compiler_params=_cparams(("arbitrary",)),
        name="peer_hidden",
    )(rows_flat, x_tiles, shift, gate, table)


N_ACC = 4


def _peer_combine_kernel(row_ref, shift_ref, coef_ref, tab_ref, o_ref, shv_ref, cv_ref):
    tm = o_ref.shape[0]
    lane = lax.broadcasted_iota(I32, (SLOTS, tm), 1)

    def columns(n):
        return _token_column(shift_ref, n, lane).astype(I32), _token_column(coef_ref, n, lane)

    def combine(n, buf):
        base = n * SLOTS
        accs = [None] * N_ACC
        for j in range(SLOTS):
            term = jnp.broadcast_to(cv_ref[buf, j:j + 1, :], ROW_TILE) * _expert_row(
                tab_ref, row_ref[base + j], shv_ref[buf, j:j + 1, :])
            accs[j % N_ACC] = term if accs[j % N_ACC] is None else accs[j % N_ACC] + term
        o_ref[n] = (accs[0] + accs[1]) + (accs[2] + accs[3])

    shv_ref[0], cv_ref[0] = columns(0)

    def token_pair(i, carry):
        for k in range(2):
            n = 2 * i + k
            nxt = columns(n + 1)
            combine(n, k)
            shv_ref[1 - k], cv_ref[1 - k] = nxt
        return carry
    lax.fori_loop(0, tm // 2, token_pair, 0)


def _peer_combine(rows_flat, shift, coef, table, n, tm):
    slot_blk = pl.BlockSpec((SLOTS, tm), lambda i: (0, i))
    return pl.pallas_call(
        _peer_combine_kernel,
        grid=(n // tm,),
        in_specs=[pl.BlockSpec((tm * SLOTS,), lambda i: (i,), memory_space=pltpu.SMEM), slot_blk, slot_blk,
                  pl.BlockSpec(memory_space=pltpu.VMEM)],
        out_specs=pl.BlockSpec((tm, *ROW_TILE), lambda i: (i, 0, 0)),
        out_shape=jax.ShapeDtypeStruct((n, *ROW_TILE), F32),
        scratch_shapes=[pltpu.VMEM((2, SLOTS, LANES), I32), pltpu.VMEM((2, SLOTS, LANES), F32)],
        compiler_params=_cparams(("arbitrary",)),
        name="peer_combine",
    )(rows_flat, shift, coef, table)


def _ln3_kernel(h_ref, p_ref, g_ref, b_ref, y_ref):
    y_ref[...] = _layer_norm(DN_ALPHA * h_ref[...] + p_ref[...], g_ref[...], b_ref[...])


def _ln3(h2d, peer2d, g, b, tm):
    n, d = h2d.shape
    tok = pl.BlockSpec((tm, d), lambda i: (i, 0))
    vec = pl.BlockSpec((1, d), lambda i: (0, 0))
    return pl.pallas_call(
        _ln3_kernel,
        grid=(n // tm,),
        in_specs=[tok, tok, vec, vec],
        out_specs=tok,
        out_shape=jax.ShapeDtypeStruct((n, d), F32),
        compiler_params=_cparams(("parallel",)),
        name="ln3",
    )(h2d, peer2d, g.reshape(1, d), b.reshape(1, d))


def _peer_ffn_ln(h2d, w_pq, keys_a, keys_b, tab_u, tab_v, ln3, tm_route, tm_gather):
    n, d = h2d.shape
    rows, shift, gate = _peer_route(h2d, w_pq, keys_a, keys_b, tm_route)
    rows_flat = rows.T.reshape(-1)
    coef = _peer_hidden(rows_flat, h2d.reshape(n, *ROW_TILE), shift, gate, tab_u, tm_gather)
    out = _peer_combine(rows_flat, shift, coef, tab_v, n, tm_gather)
    return _ln3(h2d, out.reshape(n, d), ln3[0], ln3[1], tm_route)


TOKEN_TILE = 512
ROUTE_TILE = 256
GATHER_TILE = 128
SAMPLE_TPAD = 16
PAGES_PER_STEP = 32


def _pad_axis1(a, rows):
    return jnp.pad(a, ((0, 0), (0, rows - a.shape[1])) + ((0, 0),) * (a.ndim - 2))


def kernel(x_prompt, x_sample, mem_prompt, state_ret, cache_k, cache_v, cache_idx_k, cache_mem_k, cache_mem_v,
           page_table, w_in, ret_gn_g, ret_gn_b, w_out, ln1_g, ln1_b, w_cq, w_ck, w_cv, w_co, ln2_g, ln2_b,
           w_pq, peer_keys_a, peer_keys_b, peer_u, peer_v, ln3_g, ln3_b):
    nb, t_len, d = x_prompt.shape
    db, ds, _ = x_sample.shape
    n_pages = page_table.shape[1]
    past = n_pages * PAGE_SIZE
    bf = lambda w: w.astype(BF16)

    w_in_p = _pad_w_in(w_in)
    w_out_b, w_cq_b, w_co_b, w_pq_b = bf(w_out), bf(w_cq), bf(w_co), bf(w_pq)
    keys_a, keys_b = bf(peer_keys_a), bf(peer_keys_b)
    tab_u, tab_v = _pack_expert_table(peer_u), _pack_expert_table(peer_v)
    ln1, ln2, ln3 = (ln1_g, ln1_b), (ln2_g, ln2_b), (ln3_g, ln3_b)

    def tail(x, ret_y, att_o, mem_k, mem_v, tm):
        h2 = _tail1(x, ret_y, att_o, mem_k, mem_v, w_out_b, w_cq_b, w_co_b, ln1, ln2, tm)
        return h2

    def peer(h2d):
        return _peer_ffn_ln(h2d, w_pq_b, keys_a, keys_b, tab_u, tab_v, ln3, ROUTE_TILE, GATHER_TILE)

    n = nb * t_len
    tabs = _rope_tables(jnp.arange(t_len, dtype=I32))
    rq, rk, rv, rg, aq, ak, akb, av, avt, iq, ikw, ik2, iwt = _project(
        x_prompt.reshape(n, d), w_in_p, tabs, t_len // TOKEN_TILE, TOKEN_TILE)
    ret_y, p_state = _retention(rq, rk, rv, rg, jnp.zeros((nb, RET_HEADS, RET_DK, RET_DV), F32),
                                _retention_tables(RET_CHUNK), ret_gn_g, ret_gn_b, nb, t_len // RET_CHUNK)
    att_o = _dsa_prompt(aq, iq, iwt, ik2, akb, avt, nb, t_len, min(IDX_TOPK_MAX, t_len // 4), kc=TOKEN_TILE)
    mem_kv = _matmul(mem_prompt.reshape(nb * MEM_TOKENS, d), jnp.concatenate([bf(w_ck), bf(w_cv)], axis=1), MEM_TOKENS)
    p_mem_k = mem_kv[:, :d].reshape(nb, MEM_TOKENS, d)
    p_mem_v = mem_kv[:, d:].reshape(nb, MEM_TOKENS, d)
    h2 = tail(x_prompt, ret_y.reshape(nb, t_len, RET_W), att_o.reshape(nb, t_len, ATT_W), p_mem_k, p_mem_v, TOKEN_TILE)
    y_prompt = peer(h2.reshape(n, d)).reshape(nb, t_len, d)

    ns = db * ds
    pos_s = jnp.tile(past + jnp.arange(ds, dtype=I32), db)
    sq, sk, sv, sg, saq, sak, _, sav, _, siq, sikw, _, _ = _project(
        x_sample.reshape(ns, d), w_in_p, _rope_tables(pos_s), 1, ns)
    per_seq = lambda a: a.reshape(db, ds, a.shape[-1])
    chunk = lambda a: _pad_axis1(per_seq(a), RET_CHUNK).reshape(db * RET_CHUNK, a.shape[-1])
    ret_y_s, s_state = _retention(chunk(sq), chunk(sk), chunk(sv), chunk(sg), state_ret,
                                  _retention_tables(ds), ret_gn_g, ret_gn_b, db, 1)
    ret_y_s = ret_y_s.reshape(db, RET_CHUNK, RET_W)[:, :SAMPLE_TPAD]

    pt_flat = page_table.reshape(-1)
    iq_rows = _pad_axis1(per_seq(siq), QPAD).reshape(db, QPAD * IDX_HEADS, IDX_DIM)
    w_rows = _pad_axis1(per_seq(sikw[:, IDX_DIM:IDX_DIM + IDX_HEADS]), QPAD).reshape(db, QPAD * IDX_HEADS, 1)
    ik_new = _pad_axis1(per_seq(sikw[:, :IDX_DIM]), LANES)
    sel = _dsa_sample_select(pt_flat, iq_rows, w_rows, ik_new, cache_idx_k, db, n_pages, ds,
                             min(IDX_TOPK_MAX, (past + ds) // 4), PAGES_PER_STEP)
    page_rows = lambda a: a.reshape(a.shape[0], PAGE_SIZE * ATT_HEADS, ATT_HD)
    new_rows = lambda a: page_rows(_pad_axis1(per_seq(a), PAGE_SIZE))
    q_rows = _pad_axis1(per_seq(saq).astype(F32), QPAD).reshape(db, QPAD, ATT_HEADS, ATT_HD)
    q_rows = q_rows.transpose(0, 2, 1, 3).reshape(db, ATT_HEADS * QPAD, ATT_HD)
    att_s = _dsa_sample_attend(pt_flat, q_rows, sel, new_rows(sak), new_rows(sav),
                               page_rows(cache_k), page_rows(cache_v), db, n_pages, PAGES_PER_STEP)
    att_s = att_s.reshape(db, ATT_HEADS, QPAD, ATT_HD).transpose(0, 2, 1, 3).reshape(db, QPAD, ATT_W)
    h2_s = tail(_pad_axis1(x_sample, SAMPLE_TPAD), ret_y_s, _pad_axis1(att_s, SAMPLE_TPAD),
                cache_mem_k, cache_mem_v, SAMPLE_TPAD)
    y_sample = peer(h2_s[:, :ds].reshape(ns, d)).reshape(db, ds, d)

    heads = lambda a, b_, t_: a.reshape(b_, t_, ATT_HEADS, ATT_HD)
    mem_heads = lambda a: a.reshape(nb, MEM_TOKENS, MEM_HEADS, MEM_HD)
    return (y_prompt, y_sample, p_state, heads(ak, nb, t_len), heads(av, nb, t_len),
            ikw[:, :IDX_DIM].reshape(nb, t_len, IDX_DIM), mem_heads(p_mem_k), mem_heads(p_mem_v),
            s_state, heads(sak, db, ds), heads(sav, db, ds), sikw[:, :IDX_DIM].reshape(db, ds, IDX_DIM))
```

```python
import functools

import jax
import jax.numpy as jnp
from jax import lax
from jax.experimental import pallas as pl
from jax.experimental.pallas import tpu as pltpu

F32 = jnp.float32
BF16 = jnp.bfloat16
I32 = jnp.int32

D_MODEL = 1024
DEPTH = 1
PAST_LEN = 8192
PAGE_SIZE = 128
RET_HEADS = 4
RET_DK = 128
RET_DV = 128
RET_W = RET_HEADS * RET_DV
RET_CHUNK = 128
ATT_HEADS = 4
ATT_HD = 128
ATT_W = ATT_HEADS * ATT_HD
IDX_HEADS = 8
IDX_DIM = 64
IDX_TOPK_MAX = 256
MEM_TOKENS = 256
MEM_HEADS = 4
MEM_HD = D_MODEL // MEM_HEADS
PEER_HEADS = 8
PEER_NKEYS = 128
PEER_EXPERTS = PEER_NKEYS * PEER_NKEYS
PEER_QDIM = 256
PEER_HALF = PEER_QDIM // 2
PEER_TOPK = 16
ROPE_THETA = 10000.0
LN_EPS = 1e-5
GN_EPS = 1e-5
DN_ALPHA = (2 * DEPTH) ** 0.25

LANES = 128
SUBLANES = 8
GROUP_W = 512
N_GROUPS = 8
IN_COLS_PAD = N_GROUPS * GROUP_W + LANES
VMEM_LIMIT = 56 * 1024 * 1024
INT_MIN = -(2 ** 31)
NEG_INF = float("-inf")

_NT = (((1,), (1,)), ((), ()))


def _cparams(sem):
    return pltpu.CompilerParams(dimension_semantics=sem, vmem_limit_bytes=VMEM_LIMIT)


def _rope_tables(pos):
    def tab(d):
        half = d // 2
        inv = ROPE_THETA ** (-jnp.arange(half, dtype=F32) * (2.0 / d))
        ang = pos.astype(F32)[:, None] * inv[None, :]
        cos, sin = jnp.cos(ang), jnp.sin(ang)
        reps = LANES // d
        return (jnp.tile(jnp.concatenate([cos, cos], axis=1), (1, reps)),
                jnp.tile(jnp.concatenate([-sin, sin], axis=1), (1, reps)))
    c128, s128 = tab(ATT_HD)
    c64, s64 = tab(IDX_DIM)
    return c128, s128, c64, s64


def _proj_kernel(x_ref, w_ref, wvt_ref, wwt_ref, c128_ref, s128_ref, c64_ref, s64_ref,
                 rq_ref, rk_ref, rv_ref, rg_ref, aq_ref, ak_ref, akb_ref, av_ref, avt_ref,
                 iq_ref, ikw_ref, ik2_ref, iwt_ref):
    xb = x_ref[...].astype(BF16)
    c128 = c128_ref[...]
    s128 = s128_ref[...]
    c64 = c64_ref[...]
    s64 = s64_ref[...]
    tm = xb.shape[0]
    lane = lax.broadcasted_iota(I32, (tm, LANES), 1)
    first_half64 = (lane & (IDX_DIM // 2)) == 0

    def group(g, width=GROUP_W):
        return jnp.dot(xb, w_ref[:, g * GROUP_W:g * GROUP_W + width], preferred_element_type=F32)

    def rope128(p):
        return p * c128 + pltpu.roll(p, ATT_HD // 2, 1) * s128

    def rope64(p):
        rot = jnp.where(first_half64, pltpu.roll(p, LANES - IDX_DIM // 2, 1), pltpu.roll(p, IDX_DIM // 2, 1))
        return p * c64 + rot * s64

    p = group(0)
    for h in range(RET_HEADS):
        sl = slice(h * LANES, (h + 1) * LANES)
        rq_ref[:, sl] = rope128(p[:, sl]).astype(BF16)
    p = group(1)
    for h in range(RET_HEADS):
        sl = slice(h * LANES, (h + 1) * LANES)
        rk_ref[:, sl] = (rope128(p[:, sl]) * (RET_DK ** -0.5)).astype(BF16)
    rv_ref[...] = group(2).astype(BF16)
    rg_ref[...] = group(3)
    p = group(4)
    for h in range(ATT_HEADS):
        sl = slice(h * LANES, (h + 1) * LANES)
        aq_ref[:, sl] = rope128(p[:, sl]).astype(BF16)
    p = group(5)
    for h in range(ATT_HEADS):
        sl = slice(h * LANES, (h + 1) * LANES)
        r = rope128(p[:, sl])
        ak_ref[:, sl] = r
        akb_ref[:, sl] = r.astype(BF16)
    p = group(6)
    av_ref[...] = p
    avt_ref[0] = lax.dot_general(wvt_ref[...], xb, _NT, preferred_element_type=F32).astype(BF16)
    iwt_ref[...] = lax.dot_general(wwt_ref[...], xb, _NT, preferred_element_type=F32)[:IDX_HEADS]
    p = group(7)
    for j in range(GROUP_W // LANES):
        sl = slice(j * LANES, (j + 1) * LANES)
        iq_ref[:, sl] = rope64(p[:, sl]).astype(BF16)
    p = group(8, LANES)
    is_key = lane < IDX_DIM
    r = jnp.where(is_key, rope64(p), p)
    ikw_ref[...] = r
    ik2_ref[...] = jnp.where(is_key, r, pltpu.roll(r, IDX_DIM, 1)).astype(BF16)


def _project(x2d, w_pad, tables, period_tiles, tm):
    n = x2d.shape[0]
    nt = n // tm
    row = lambda i: (i, 0)
    tab = lambda i: (i % period_tiles, 0)
    const = lambda i: (0, 0)
    flat = lambda w, dt: (pl.BlockSpec((tm, w), row), jax.ShapeDtypeStruct((n, w), dt))
    outs = [
        flat(GROUP_W, BF16), flat(GROUP_W, BF16), flat(GROUP_W, BF16), flat(GROUP_W, F32),
        flat(GROUP_W, BF16), flat(GROUP_W, F32), flat(GROUP_W, BF16), flat(GROUP_W, F32),
        (pl.BlockSpec((1, ATT_W, tm), lambda i: (i, 0, 0)), jax.ShapeDtypeStruct((nt, ATT_W, tm), BF16)),
        flat(GROUP_W, BF16), flat(LANES, F32), flat(LANES, BF16),
        (pl.BlockSpec((IDX_HEADS, tm), lambda i: (0, i)), jax.ShapeDtypeStruct((IDX_HEADS, n), F32)),
    ]
    v_cols = slice(6 * GROUP_W, 7 * GROUP_W)
    w_cols = slice(N_GROUPS * GROUP_W + IDX_DIM, N_GROUPS * GROUP_W + IDX_DIM + 2 * IDX_HEADS)
    return pl.pallas_call(
        _proj_kernel,
        grid=(nt,),
        in_specs=[pl.BlockSpec((tm, D_MODEL), row),
                  pl.BlockSpec((D_MODEL, IN_COLS_PAD), const),
                  pl.BlockSpec((GROUP_W, D_MODEL), const), pl.BlockSpec((2 * IDX_HEADS, D_MODEL), const)]
                 + [pl.BlockSpec((tm, LANES), tab)] * 4,
        out_specs=[spec for spec, _ in outs],
        out_shape=[shape for _, shape in outs],
        compiler_params=_cparams(("parallel",)),
        name="proj_rope",
    )(x2d, w_pad, w_pad[:, v_cols].T, w_pad[:, w_cols].T, *tables)


def _pad_w_in(w_in):
    return jnp.pad(w_in.astype(BF16), ((0, 0), (0, IN_COLS_PAD - w_in.shape[1])))


def _retention_log_decay():
    return jnp.log(1.0 - 2.0 ** (-5.0 - jnp.arange(RET_HEADS, dtype=F32)))


def _retention_tables(chunk_len):
    log_g = _retention_log_decay()
    j = jnp.arange(RET_CHUNK, dtype=F32)
    diff = j[:, None] - j[None, :]
    dmat = jnp.where(diff >= 0, jnp.exp(log_g[:, None, None] * jnp.maximum(diff, 0.0)), 0.0)
    q_dec = jnp.exp(log_g[None, :] * (j[:, None] + 1.0))
    k_dec = jnp.exp(log_g[None, :] * (chunk_len - 1.0 - j[:, None]))
    k_dec = jnp.where(j[:, None] < chunk_len, k_dec, 0.0)
    s_dec = jnp.exp(log_g * chunk_len)
    widen = lambda t: jnp.repeat(t, LANES, axis=1)
    return dmat, widen(q_dec), widen(k_dec), s_dec


def _retention_kernel(sdec_ref, q_ref, k_ref, v_ref, g_ref, s0_ref, dmat_ref, qdec_ref, kdec_ref,
                      gng_ref, gnb_ref, y_ref, sout_ref, state_ref):
    c = pl.program_id(1)

    @pl.when(c == 0)
    def _():
        state_ref[...] = s0_ref[0]

    for h in range(RET_HEADS):
        sl = slice(h * LANES, (h + 1) * LANES)
        q = q_ref[:, sl]
        k = k_ref[:, sl]
        v = v_ref[:, sl]
        s_prev = state_ref[h]
        scores = lax.dot_general(q, k, _NT, preferred_element_type=F32) * dmat_ref[h]
        o = jnp.dot(scores.astype(BF16), v, preferred_element_type=F32)
        o = o + jnp.dot(q, s_prev.astype(BF16), preferred_element_type=F32) * qdec_ref[:, sl]
        kd_t = (k.astype(F32) * kdec_ref[:, sl]).T.astype(BF16)
        state_ref[h] = s_prev * sdec_ref[h] + jnp.dot(kd_t, v, preferred_element_type=F32)
        mu = jnp.mean(o, axis=-1, keepdims=True)
        var = jnp.mean(jnp.square(o - mu), axis=-1, keepdims=True)
        n = (o - mu) * lax.rsqrt(var + GN_EPS) * gng_ref[:, sl] + gnb_ref[:, sl]
        g = g_ref[:, sl]
        y_ref[:, sl] = (g * (1.0 / (1.0 + jnp.exp(-g))) * n).astype(BF16)

    @pl.when(c == pl.num_programs(1) - 1)
    def _():
        sout_ref[0] = state_ref[...]


def _retention(q, k, v, g, s0, tables, gn_g, gn_b, nb, nc):
    dmat, qdec, kdec, sdec = tables
    n = q.shape[0]
    row = lambda b, c: (b * nc + c, 0)
    const2 = lambda b, c: (0, 0)
    blk = pl.BlockSpec((RET_CHUNK, RET_W), row)
    return pl.pallas_call(
        _retention_kernel,
        grid=(nb, nc),
        in_specs=[pl.BlockSpec(memory_space=pltpu.SMEM), blk, blk, blk, blk,
                  pl.BlockSpec((1, RET_HEADS, RET_DK, RET_DV), lambda b, c: (b, 0, 0, 0)),
                  pl.BlockSpec((RET_HEADS, RET_CHUNK, RET_CHUNK), lambda b, c: (0, 0, 0)),
                  pl.BlockSpec((RET_CHUNK, RET_W), const2), pl.BlockSpec((RET_CHUNK, RET_W), const2),
                  pl.BlockSpec((1, RET_W), const2), pl.BlockSpec((1, RET_W), const2)],
        out_specs=[blk, pl.BlockSpec((1, RET_HEADS, RET_DK, RET_DV), lambda b, c: (b, 0, 0, 0))],
        out_shape=[jax.ShapeDtypeStruct((n, RET_W), BF16),
                   jax.ShapeDtypeStruct((nb, RET_HEADS, RET_DK, RET_DV), F32)],
        scratch_shapes=[pltpu.VMEM((RET_HEADS, RET_DK, RET_DV), F32)],
        compiler_params=_cparams(("parallel", "arbitrary")),
        name="retention",
    )(sdec, q, k, v, g, s0, dmat, qdec, kdec, gn_g.reshape(1, RET_W), gn_b.reshape(1, RET_W))


def _sort_key(x):
    bits = pltpu.bitcast(x, I32)
    return bits ^ ((bits >> 31) & 0x7FFFFFFF)


def _kth_largest_key(count_ge, shape, topk):
    def bit_body(bi, t):
        cand = t + lax.shift_left(jnp.int32(1), 31 - bi)
        return jnp.where(count_ge(cand) >= topk, cand, t)
    return lax.fori_loop(0, 32, bit_body, jnp.full(shape, INT_MIN, I32))


def _tie_limit(count_eq_below, need, shape, pos_bits):
    def bit_body(bi, p):
        cand = p + lax.shift_left(jnp.int32(1), pos_bits - 1 - bi)
        return jnp.where(count_eq_below(cand) <= need, cand, p)
    return lax.fori_loop(0, pos_bits, bit_body, jnp.zeros(shape, I32))


def _dsa_prompt_kernel(aq_ref, iq_ref, iwt_ref, ik2_ref, ak_ref, avt_ref, o_ref, keys_ref, bias_ref, *,
                       tq, kc, topk, pos_bits):
    i = pl.program_id(1)
    q0 = i * tq
    nk = (q0 + tq + kc - 1) // kc
    groups = kc // SUBLANES
    state = (SUBLANES, tq)
    lane = lax.broadcasted_iota(I32, (tq, LANES), 1)
    key_in_chunk = (lax.broadcasted_iota(I32, (groups, SUBLANES, tq), 0) * SUBLANES
                    + lax.broadcasted_iota(I32, (groups, SUBLANES, tq), 1))
    qpos = q0 + lax.broadcasted_iota(I32, (groups, SUBLANES, tq), 2)
    iq = iq_ref[...]
    iwt = iwt_ref[...]

    def score_chunk(c, carry):
        k0 = pl.multiple_of(c * kc, kc)
        ikc = ik2_ref[pl.ds(k0, kc), :]
        acc = None
        for h in range(IDX_HEADS):
            pair = iq[:, (h // 2) * LANES:(h // 2 + 1) * LANES]
            keep = (lane < IDX_DIM) if h % 2 == 0 else (lane >= IDX_DIM)
            qh = jnp.where(keep, pair, jnp.zeros_like(pair))
            s = lax.dot_general(ikc, qh, _NT, preferred_element_type=F32)
            term = jnp.maximum(s, 0.0) * iwt[h:h + 1, :]
            acc = term if acc is None else acc + term
        key = _sort_key(acc).reshape(groups, SUBLANES, tq)
        keys_ref[c] = jnp.where(k0 + key_in_chunk <= qpos, key, INT_MIN)
        return carry
    lax.fori_loop(0, nk, score_chunk, 0)

    def total(a):
        return jnp.broadcast_to(jnp.sum(a, axis=0, keepdims=True), state)

    def count_ge(cand):
        body = lambda c, a: a + jnp.sum(jnp.where(keys_ref[c] >= cand[None], 1, 0), axis=0)
        return total(lax.fori_loop(0, nk, body, jnp.zeros(state, I32)))

    t = _kth_largest_key(count_ge, state, topk)
    n_ge = count_ge(t)
    need = topk - count_ge(t + 1)

    def count_eq_below(p):
        def body(c, a):
            hit = (keys_ref[c] == t[None]) & (c * kc + key_in_chunk < p[None])
            return a + jnp.sum(jnp.where(hit, 1, 0), axis=0)
        return total(lax.fori_loop(0, nk, body, jnp.zeros(state, I32)))

    has_excess = jnp.max(jnp.where(n_ge > topk, 1, 0)) > 0
    plim = lax.cond(has_excess,
                    lambda: _tie_limit(count_eq_below, need, state, pos_bits),
                    lambda: jnp.full(state, 2 ** pos_bits, I32))

    def bias_chunk(c, carry):
        key = keys_ref[c]
        sel = (key > t[None]) | ((key == t[None]) & (c * kc + key_in_chunk < plim[None]))
        bias_ref[c] = jnp.where(sel & (key != INT_MIN), 0.0, NEG_INF)
        return carry
    lax.fori_loop(0, nk, bias_chunk, 0)

    scale = ATT_HD ** -0.5
    head_cols = [slice(h * LANES, (h + 1) * LANES) for h in range(ATT_HEADS)]
    queries = [aq_ref[:, sl] for sl in head_cols]

    def attend_chunk(c, carry):
        k0 = pl.multiple_of(c * kc, kc)
        bias = bias_ref[c].reshape(kc, tq)
        out = []
        for h, sl in enumerate(head_cols):
            m, l, acc = carry[h]
            s = lax.dot_general(ak_ref[pl.ds(k0, kc), sl], queries[h], _NT, preferred_element_type=F32)
            s = s * scale + bias
            m_new = jnp.maximum(m, jnp.max(s, axis=0, keepdims=True))
            m_safe = jnp.where(m_new == NEG_INF, 0.0, m_new)
            p = jnp.exp(s - m_safe)
            alpha = jnp.exp(m - m_safe)
            l_new = alpha * l + jnp.sum(p, axis=0, keepdims=True)
            acc_new = alpha * acc + jnp.dot(avt_ref[c, sl, :], p.astype(BF16), preferred_element_type=F32)
            out.append((m_new, l_new, acc_new))
        return tuple(out)

    init = tuple((jnp.full((1, tq), NEG_INF, F32), jnp.zeros((1, tq), F32), jnp.zeros((ATT_HD, tq), F32))
                 for _ in head_cols)
    fin = lax.fori_loop(0, nk, attend_chunk, init)
    for (m, l, acc), sl in zip(fin, head_cols):
        o_ref[:, sl] = (acc / l).T.astype(BF16)


def _dsa_prompt(aq, iq, iwt, ik2, akb, avt, nb, t_len, topk, tq=LANES, kc=512):
    n = aq.shape[0]
    nq = t_len // tq
    qrow = lambda b, i: (b * nq + i, 0)
    brow = lambda b, i: (b, 0)
    kern = functools.partial(_dsa_prompt_kernel, tq=tq, kc=kc, topk=topk, pos_bits=(t_len - 1).bit_length() + 1)
    chunked = (t_len // kc, kc // SUBLANES, SUBLANES, tq)
    return pl.pallas_call(
        kern,
        grid=(nb, nq),
        in_specs=[pl.BlockSpec((tq, ATT_W), qrow), pl.BlockSpec((tq, GROUP_W), qrow),
                  pl.BlockSpec((IDX_HEADS, tq), lambda b, i: (0, b * nq + i)),
                  pl.BlockSpec((t_len, LANES), brow), pl.BlockSpec((t_len, ATT_W), brow),
                  pl.BlockSpec((t_len // kc, ATT_W, kc), lambda b, i: (b, 0, 0))],
        out_specs=pl.BlockSpec((tq, ATT_W), qrow),
        out_shape=jax.ShapeDtypeStruct((n, ATT_W), BF16),
        scratch_shapes=[pltpu.VMEM(chunked, I32), pltpu.VMEM(chunked, F32)],
        compiler_params=_cparams(("parallel", "arbitrary")),
        name="dsa_prompt",
    )(aq, iq, iwt, ik2, akb, avt)


QPAD = SUBLANES


def _dsa_sample_select_kernel(pt_ref, iq_ref, w_ref, iknew_ref, expand_ref, *rest, pg, n_pages, n_new, topk,
                              pos_bits):
    idx_refs = rest[:pg]
    sel_ref = rest[pg]
    keys_ref = rest[pg + 1]
    p = pl.program_id(1)
    iq = iq_ref[0]
    w = w_ref[0]
    key_in_block = lax.broadcasted_iota(I32, (QPAD, LANES), 1)
    row = lax.broadcasted_iota(I32, (QPAD, LANES), 0)

    def scores(keys_t):
        s = jnp.dot(iq, keys_t.astype(BF16), preferred_element_type=F32)
        r = jnp.maximum(s, 0.0) * w
        return jnp.sum(r.reshape(QPAD, IDX_HEADS, LANES), axis=1)

    for j in range(pg):
        keys_ref[p * pg + j] = _sort_key(scores(idx_refs[j][0]))

    @pl.when(p == pl.num_programs(1) - 1)
    def _():
        s_new = scores(iknew_ref[0])
        keys_ref[n_pages] = jnp.where((key_in_block <= row) & (key_in_block < n_new), _sort_key(s_new), INT_MIN)
        keys_ref[n_pages + 1] = jnp.full((QPAD, LANES), INT_MIN, I32)
        keys = keys_ref[...]
        pos = lax.broadcasted_iota(I32, keys.shape, 0) * LANES + key_in_block[None]

        def lane_total(a):
            return jnp.broadcast_to(jnp.sum(a, axis=1, keepdims=True), (QPAD, LANES))

        def count_ge(cand):
            return lane_total(jnp.sum(jnp.where(keys >= cand[None], 1, 0), axis=0))

        t = _kth_largest_key(count_ge, (QPAD, LANES), topk)
        need = topk - count_ge(t + 1)

        def count_eq_below(lim):
            return lane_total(jnp.sum(jnp.where((keys == t[None]) & (pos < lim[None]), 1, 0), axis=0))

        has_excess = jnp.max(jnp.where(count_ge(t) > topk, 1, 0)) > 0
        plim = lax.cond(has_excess,
                        lambda: _tie_limit(count_eq_below, need, (QPAD, LANES), pos_bits),
                        lambda: jnp.full((QPAD, LANES), 2 ** pos_bits, I32))

        sel = ((keys > t[None]) | ((keys == t[None]) & (pos < plim[None]))) & (keys != INT_MIN)
        flags = jnp.where(sel, 1.0, 0.0).reshape(keys.shape[0] * QPAD, LANES).astype(BF16)
        out = jnp.dot(flags, expand_ref[...], preferred_element_type=F32)
        sel_ref[0] = out.reshape(keys.shape[0], QPAD, out.shape[1])


def _dsa_sample_select(pt_flat, iq_rows, w_rows, ik_new, cache_idx_k, nb, n_pages, n_new, topk, pg):
    kern = functools.partial(_dsa_sample_select_kernel, pg=pg, n_pages=n_pages, n_new=n_new, topk=topk,
                             pos_bits=((n_pages + 1) * LANES).bit_length())
    per_b = lambda shape: pl.BlockSpec((1,) + shape, lambda b, p, pt: (b,) + (0,) * len(shape))
    keys_minor = lambda a: a.transpose(0, 2, 1)
    page = lambda j: pl.BlockSpec((1, IDX_DIM, PAGE_SIZE), lambda b, p, pt: (pt[b * n_pages + p * pg + j], 0, 0))
    cols = PAGE_SIZE * ATT_HEADS
    expand = (jnp.arange(cols)[None, :] // ATT_HEADS == jnp.arange(PAGE_SIZE)[:, None]).astype(BF16)
    nblk = n_pages + 2
    return pl.pallas_call(
        kern,
        grid_spec=pltpu.PrefetchScalarGridSpec(
            num_scalar_prefetch=1,
            grid=(nb, n_pages // pg),
            in_specs=[per_b((QPAD * IDX_HEADS, IDX_DIM)), per_b((QPAD * IDX_HEADS, 1)), per_b((IDX_DIM, PAGE_SIZE)),
                      pl.BlockSpec((PAGE_SIZE, cols), lambda b, p, pt: (0, 0))]
                     + [page(j) for j in range(pg)],
            out_specs=per_b((nblk, QPAD, cols)),
            scratch_shapes=[pltpu.VMEM((nblk, QPAD, LANES), I32)]),
        out_shape=jax.ShapeDtypeStruct((nb, nblk, QPAD, cols), F32),
        compiler_params=_cparams(("parallel", "arbitrary")),
        name="dsa_sample_select",
    )(pt_flat, iq_rows, w_rows, keys_minor(ik_new), expand, *([keys_minor(cache_idx_k)] * pg))


def _dsa_sample_attend_kernel(pt_ref, q_ref, sel_ref, knew_ref, vnew_ref, *rest, pg, n_pages):
    k_refs = rest[:pg]
    v_refs = rest[pg:2 * pg]
    o_ref = rest[2 * pg]
    m_ref, l_ref, acc_ref = rest[2 * pg + 1:]
    p = pl.program_id(1)
    scale = ATT_HD ** -0.5
    rows = ATT_HEADS * QPAD
    cols = PAGE_SIZE * ATT_HEADS

    @pl.when(p == 0)
    def _():
        m_ref[...] = jnp.full(m_ref.shape, NEG_INF, F32)
        l_ref[...] = jnp.zeros(l_ref.shape, F32)
        acc_ref[...] = jnp.zeros(acc_ref.shape, F32)

    q = q_ref[0].astype(BF16)
    own_head = ((lax.broadcasted_iota(I32, (rows, cols), 1) & (ATT_HEADS - 1))
                == (lax.broadcasted_iota(I32, (rows, cols), 0) >> (QPAD.bit_length() - 1)))

    def update(blocks):
        scores = []
        for flags, kp, _ in blocks:
            s = lax.dot_general(q, kp.astype(BF16), _NT, preferred_element_type=F32) * scale
            mask = own_head & (jnp.concatenate([flags] * ATT_HEADS, axis=0) > 0.5)
            scores.append(jnp.where(mask, s, NEG_INF))
        mc = scores[0]
        for s in scores[1:]:
            mc = jnp.maximum(mc, s)
        m = m_ref[...]
        m_new = jnp.maximum(m, jnp.max(mc, axis=1, keepdims=True))
        m_safe = jnp.where(m_new == NEG_INF, 0.0, m_new)
        m_row = m_safe[:, :1]
        psum = jnp.zeros((rows, cols), F32)
        acc = jnp.exp(m - m_safe) * acc_ref[...]
        for (_, _, vp), s in zip(blocks, scores):
            pr = jnp.exp(s - m_row)
            psum = psum + pr
            acc = acc + jnp.dot(pr.astype(BF16), vp.astype(BF16), preferred_element_type=F32)
        l_ref[...] = jnp.exp(m - m_safe) * l_ref[...] + jnp.sum(psum, axis=1, keepdims=True)
        acc_ref[...] = acc
        m_ref[...] = m_new

    update([(sel_ref[0, p * pg + j], k_refs[j][0], v_refs[j][0]) for j in range(pg)])

    @pl.when(p == pl.num_programs(1) - 1)
    def _():
        update([(sel_ref[0, n_pages], knew_ref[0], vnew_ref[0])])
        o_ref[0] = acc_ref[...] / l_ref[...]


def _dsa_sample_attend(pt_flat, q_rows, sel, k_new, v_new, cache_k, cache_v, nb, n_pages, pg):
    kern = functools.partial(_dsa_sample_attend_kernel, pg=pg, n_pages=n_pages)
    rows = ATT_HEADS * QPAD
    cols = PAGE_SIZE * ATT_HEADS
    per_b = lambda shape: pl.BlockSpec((1,) + shape, lambda b, p, pt: (b,) + (0,) * len(shape))
    page = lambda j: pl.BlockSpec((1, cols, ATT_HD), lambda b, p, pt: (pt[b * n_pages + p * pg + j], 0, 0))
    return pl.pallas_call(
        kern,
        grid_spec=pltpu.PrefetchScalarGridSpec(
            num_scalar_prefetch=1,
            grid=(nb, n_pages // pg),
            in_specs=[per_b((rows, ATT_HD)), per_b((n_pages + 2, QPAD, cols)), per_b((cols, ATT_HD)),
                      per_b((cols, ATT_HD))] + [page(j) for j in range(pg)] * 2,
            out_specs=per_b((rows, ATT_HD)),
            scratch_shapes=[pltpu.VMEM((rows, LANES), F32)] * 3),
        out_shape=jax.ShapeDtypeStruct((nb, rows, ATT_HD), F32),
        compiler_params=_cparams(("parallel", "arbitrary")),
        name="dsa_sample_attend",
    )(pt_flat, q_rows, sel, k_new, v_new, *([cache_k] * pg), *([cache_v] * pg))


def _layer_norm(x, g, b):
    mu = jnp.mean(x, axis=-1, keepdims=True)
    var = jnp.mean(jnp.square(x - mu), axis=-1, keepdims=True)
    return (x - mu) * lax.rsqrt(var + LN_EPS) * g + b


def _matmul_kernel(x_ref, w_ref, o_ref):
    o_ref[...] = jnp.dot(x_ref[...].astype(BF16), w_ref[...], preferred_element_type=F32)


def _matmul(x, w_bf16, tm):
    n, kdim = x.shape
    m = w_bf16.shape[1]
    return pl.pallas_call(
        _matmul_kernel,
        grid=(n // tm,),
        in_specs=[pl.BlockSpec((tm, kdim), lambda i: (i, 0)), pl.BlockSpec((kdim, m), lambda i: (0, 0))],
        out_specs=pl.BlockSpec((tm, m), lambda i: (i, 0)),
        out_shape=jax.ShapeDtypeStruct((n, m), F32),
        compiler_params=_cparams(("parallel",)),
        name="mem_proj",
    )(x, w_bf16)


def _tail1_kernel(x_ref, ry_ref, ao_ref, mk_ref, mv_ref, wout_ref, wcq_ref, wco_ref,
                  ln1g_ref, ln1b_ref, ln2g_ref, ln2b_ref, h2_ref, *, head_rows):
    x = x_ref[0]
    tm = x.shape[0]
    mix = jnp.dot(ry_ref[0].astype(BF16), wout_ref[:RET_W, :], preferred_element_type=F32)
    mix = mix + jnp.dot(ao_ref[0].astype(BF16), wout_ref[RET_W:, :], preferred_element_type=F32)
    h1 = _layer_norm(DN_ALPHA * x + mix, ln1g_ref[...], ln1b_ref[...])
    q = jnp.dot(h1.astype(BF16), wcq_ref[...], preferred_element_type=F32).astype(BF16)
    head_cols = [slice(h * MEM_HD, (h + 1) * MEM_HD) for h in range(MEM_HEADS)]
    if head_rows:
        q_rows = jnp.concatenate([q[:, sl] for sl in head_cols], axis=0)
        s = lax.dot_general(q_rows, mk_ref[0].astype(BF16), _NT, preferred_element_type=F32) * (MEM_HD ** -0.5)
        assert tm & (tm - 1) == 0, "stacked-head rows are split by a shift"
        own_head = ((lax.broadcasted_iota(I32, s.shape, 1) & (MEM_HEADS - 1))
                    == lax.broadcasted_iota(I32, s.shape, 0) >> (tm.bit_length() - 1))
        s = jnp.where(own_head, s, NEG_INF)
        e = jnp.exp(s - jnp.max(s, axis=-1, keepdims=True))
        p = e / jnp.sum(e, axis=-1, keepdims=True)
        o = jnp.dot(p.astype(BF16), mv_ref[0].astype(BF16), preferred_element_type=F32).astype(BF16)
        heads = [o[h * tm:(h + 1) * tm] for h in range(MEM_HEADS)]
    else:
        heads = []
        for sl in head_cols:
            mk = mk_ref[0, :, sl].astype(BF16)
            mv = mv_ref[0, :, sl].astype(BF16)
            s = lax.dot_general(q[:, sl], mk, _NT, preferred_element_type=F32) * (MEM_HD ** -0.5)
            e = jnp.exp(s - jnp.max(s, axis=-1, keepdims=True))
            p = e / jnp.sum(e, axis=-1, keepdims=True)
            heads.append(jnp.dot(p.astype(BF16), mv, preferred_element_type=F32).astype(BF16))
    ca = jnp.concatenate(heads, axis=1)
    h2 = DN_ALPHA * h1 + jnp.dot(ca, wco_ref[...], preferred_element_type=F32)
    h2_ref[0] = _layer_norm(h2, ln2g_ref[...], ln2b_ref[...])


def _tail1(x, ret_y, att_o, mem_k, mem_v, w_out, w_cq, w_co, ln1, ln2, tm):
    g, tg, d = x.shape
    head_rows = mem_k.ndim == 4
    if head_rows:
        mem_k = mem_k.reshape(g, MEM_TOKENS * MEM_HEADS, MEM_HD)
        mem_v = mem_v.reshape(g, MEM_TOKENS * MEM_HEADS, MEM_HD)
    tok = lambda w: pl.BlockSpec((1, tm, w), lambda b, i: (b, i, 0))
    mem = pl.BlockSpec((1,) + mem_k.shape[1:], lambda b, i: (b, 0, 0))
    wsp = lambda r, c: pl.BlockSpec((r, c), lambda b, i: (0, 0))
    vec = wsp(1, d)
    return pl.pallas_call(
        functools.partial(_tail1_kernel, head_rows=head_rows),
        grid=(g, tg // tm),
        in_specs=[tok(d), tok(RET_W), tok(ATT_W), mem, mem, wsp(d, d), wsp(d, d), wsp(d, d), vec, vec, vec, vec],
        out_specs=tok(d),
        out_shape=jax.ShapeDtypeStruct((g, tg, d), F32),
        compiler_params=_cparams(("parallel", "arbitrary")),
        name="tail1",
    )(x, ret_y, att_o, mem_k, mem_v, w_out, w_cq, w_co,
      ln1[0].reshape(1, d), ln1[1].reshape(1, d), ln2[0].reshape(1, d), ln2[1].reshape(1, d))


HALF_EXPERTS = PEER_EXPERTS // 2
HALF_BIT = HALF_EXPERTS.bit_length() - 1
ROW_TILE = (SUBLANES, LANES)
SLOTS = PEER_HEADS * PEER_TOPK
HIGH_HALF = -65536


PACK_PAIRS = 64


def _pack_kernel(lo_ref, hi_ref, o_ref):
    def bf16_bits(x):
        return pltpu.bitcast(x.astype(BF16).astype(F32), I32)

    for g in range(PACK_PAIRS // SUBLANES):
        rows = slice(g * SUBLANES, (g + 1) * SUBLANES)
        word = lax.shift_right_logical(bf16_bits(lo_ref[rows, :]), 16) | bf16_bits(hi_ref[rows, :])
        for s in range(SUBLANES):
            o_ref[pl.ds(g * SUBLANES * SUBLANES + s, SUBLANES, stride=SUBLANES), :] = word[:, s * LANES:(s + 1) * LANES]


def _pack_expert_table(w):
    d = w.shape[1]
    steps = HALF_EXPERTS // PACK_PAIRS
    return pl.pallas_call(
        _pack_kernel,
        grid=(steps,),
        in_specs=[pl.BlockSpec((PACK_PAIRS, d), lambda i: (i, 0)),
                  pl.BlockSpec((PACK_PAIRS, d), lambda i: (i + steps, 0))],
        out_specs=pl.BlockSpec((PACK_PAIRS * SUBLANES, LANES), lambda i: (i, 0)),
        out_shape=jax.ShapeDtypeStruct((HALF_EXPERTS * SUBLANES, LANES), I32),
        compiler_params=_cparams(("parallel",)),
        name="pack_table",
    )(w, w)


def _top_rows(s, k, payload=None, order=None):
    rows = s.shape[0]
    rid = lax.broadcasted_iota(I32, s.shape, 0)
    order = rid if order is None else order
    big = jnp.int32(2 ** 30)
    vals, pays = [], []
    for _ in range(k):
        m = jnp.max(s, axis=0, keepdims=True)
        win = jnp.min(jnp.where(s == m, order, big), axis=0, keepdims=True)
        hit = order == win
        vals.append(m)
        pays.append(win if payload is None else jnp.max(jnp.where(hit, payload, -1), axis=0, keepdims=True))
        s = jnp.where(hit, NEG_INF, s)
    return jnp.concatenate(vals, axis=0), jnp.concatenate(pays, axis=0)


def _peer_route_kernel(h_ref, wpq_ref, ka_ref, kb_ref, row_ref, shift_ref, gate_ref):
    hb = h_ref[...].astype(BF16)
    tm = hb.shape[0]
    ka = ka_ref[...]
    kb = kb_ref[...]
    a8 = lax.broadcasted_iota(I32, (SUBLANES, tm), 0)
    b16 = lax.broadcasted_iota(I32, (PEER_TOPK, tm), 0)
    for hd in range(PEER_HEADS):
        q = jnp.dot(hb, wpq_ref[:, hd * PEER_QDIM:(hd + 1) * PEER_QDIM], preferred_element_type=F32).astype(BF16)
        s1 = lax.dot_general(ka, q[:, :PEER_HALF], _NT, preferred_element_type=F32)
        s2 = lax.dot_general(kb, q[:, PEER_HALF:], _NT, preferred_element_type=F32)
        v1, i1 = _top_rows(s1, PEER_TOPK)
        v2, i2 = _top_rows(s2, PEER_TOPK)
        cand, cid, fid = [], [], []

        def block(vals, ids, flat, valid):
            cand.append(jnp.where(valid, vals, NEG_INF) if valid is not None else vals)
            cid.append(ids)
            fid.append(flat)

        block(v1[0:1] + v2, i1[0:1] * PEER_NKEYS + i2, b16, None)
        block(v1[1:2] + v2[:8], i1[1:2] * PEER_NKEYS + i2[:8], PEER_TOPK + a8, None)
        block(v1[8:] + v2[0:1], i1[8:] * PEER_NKEYS + i2[0:1], (a8 + 8) * PEER_TOPK, None)
        for b in range(5):
            lim = PEER_TOPK // (b + 1)
            block(v1[:8] + v2[b:b + 1], i1[:8] * PEER_NKEYS + i2[b:b + 1], a8 * PEER_TOPK + b,
                  (a8 >= 2) & (a8 < lim))
        sc, eid = _top_rows(jnp.concatenate(cand, axis=0), PEER_TOPK,
                            payload=jnp.concatenate(cid, axis=0), order=jnp.concatenate(fid, axis=0))
        e = jnp.exp(sc - sc[0:1])
        rows = slice(hd * PEER_TOPK, (hd + 1) * PEER_TOPK)
        gate_ref[rows, :] = e / jnp.sum(e, axis=0, keepdims=True)
        row_ref[rows, :] = (eid & (HALF_EXPERTS - 1)) * SUBLANES
        shift_ref[rows, :] = (16 - ((eid >> HALF_BIT) << 4)).astype(F32)


def _peer_route(h2d, w_pq, keys_a, keys_b, tm):
    n, d = h2d.shape
    slot_blk = pl.BlockSpec((SLOTS, tm), lambda i: (0, i))
    return pl.pallas_call(
        _peer_route_kernel,
        grid=(n // tm,),
        in_specs=[pl.BlockSpec((tm, d), lambda i: (i, 0)),
                  pl.BlockSpec((d, PEER_HEADS * PEER_QDIM), lambda i: (0, 0)),
                  pl.BlockSpec((PEER_NKEYS, PEER_HALF), lambda i: (0, 0)),
                  pl.BlockSpec((PEER_NKEYS, PEER_HALF), lambda i: (0, 0))],
        out_specs=[slot_blk, slot_blk, slot_blk],
        out_shape=[jax.ShapeDtypeStruct((SLOTS, n), I32), jax.ShapeDtypeStruct((SLOTS, n), F32),
                   jax.ShapeDtypeStruct((SLOTS, n), F32)],
        compiler_params=_cparams(("parallel",)),
        name="peer_route",
    )(h2d, w_pq, keys_a, keys_b)


def _token_column(ref, n, lane):
    col = jnp.sum(jnp.where(lane == n, ref[...], 0.0), axis=1, keepdims=True)
    return jnp.broadcast_to(col, (SLOTS, LANES))


def _expert_row(tab_ref, row, shift_row):
    word = tab_ref[pl.ds(pl.multiple_of(row, SUBLANES), SUBLANES), :]
    return pltpu.bitcast(lax.shift_left(word, jnp.broadcast_to(shift_row, ROW_TILE)) & jnp.int32(HIGH_HALF), F32)


def _fold_sublanes(tiles, sub):
    step = SUBLANES // 2
    while len(tiles) > 1:
        half = len(tiles) // 2
        low = (sub & step) == 0
        nxt = []
        for i in range(half):
            a, b = tiles[i], tiles[i + half]
            if 2 * step == SUBLANES:
                folded = jnp.where(low, a, b) + pltpu.roll(jnp.where(low, b, a), step, 0)
            else:
                folded = jnp.where(low, a + pltpu.roll(a, SUBLANES - step, 0), b + pltpu.roll(b, step, 0))
            nxt.append(folded)
        tiles = nxt
        step //= 2
    return tiles[0]


TOKENS_PER_STEP = 8


def _peer_hidden_kernel(row_ref, x_ref, shift_ref, gate_ref, tab_ref, coef_ref, shv_ref, part_ref, hid_ref):
    tm = x_ref.shape[0]
    lane = lax.broadcasted_iota(I32, (SLOTS, tm), 1)
    sub = lax.broadcasted_iota(I32, ROW_TILE, 0)

    def shifts(n):
        return _token_column(shift_ref, n, lane).astype(I32)

    def gather(n, buf):
        x = x_ref[n]
        base = n * SLOTS
        for g in range(SLOTS // SUBLANES):
            prods = []
            for r in range(SUBLANES):
                j = g * SUBLANES + r
                prods.append(_expert_row(tab_ref, row_ref[base + j], shv_ref[buf, j:j + 1, :]) * x)
            part_ref[n, g * SUBLANES:(g + 1) * SUBLANES, :] = _fold_sublanes(prods, sub)

    shv_ref[0] = shifts(0)

    def token_group(i, carry):
        for k in range(TOKENS_PER_STEP):
            n = TOKENS_PER_STEP * i + k
            shv_ref[(k + 1) % 2] = shifts(n + 1)
            gather(n, k % 2)
        return carry
    lax.fori_loop(0, tm // TOKENS_PER_STEP, token_group, 0)

    hid_ref[...] = jnp.zeros((SLOTS, tm), F32)

    def reduce_tokens(i, carry):
        hid = hid_ref[...]
        for r in range(SUBLANES):
            n = i * SUBLANES + r
            hid = jnp.where(lane == n, jnp.sum(part_ref[n], axis=1, keepdims=True), hid)
        hid_ref[...] = hid
        return carry
    lax.fori_loop(0, tm // SUBLANES, reduce_tokens, 0)
    hid = hid_ref[...]
    gelu = 0.5 * hid * (1.0 + lax.erf(hid * (2.0 ** -0.5)))
    coef_ref[...] = gate_ref[...] * gelu


def _peer_hidden(rows_flat, x_tiles, shift, gate, table, tm):
    n = x_tiles.shape[0]
    slot_blk = pl.BlockSpec((SLOTS, tm), lambda i: (0, i))
    return pl.pallas_call(
        _peer_hidden_kernel,
        grid=(n // tm,),
        in_specs=[pl.BlockSpec((tm * SLOTS,), lambda i: (i,), memory_space=pltpu.SMEM),
                  pl.BlockSpec((tm, *ROW_TILE), lambda i: (i, 0, 0)),
                  slot_blk, slot_blk,
                  pl.BlockSpec(memory_space=pltpu.VMEM)],
        out_specs=slot_blk,
        out_shape=jax.ShapeDtypeStruct((SLOTS, n), F32),
        scratch_shapes=[pltpu.VMEM((2, SLOTS, LANES), I32), pltpu.VMEM((tm, SLOTS, LANES), F32),
                        pltpu.VMEM((SLOTS, tm), F32)],
        compiler_params=_cparams(("arbitrary",)),
        name="peer_hidden",
    )(rows_flat, x_tiles, shift, gate, table)


N_ACC = 4


def _peer_combine_kernel(row_ref, shift_ref, coef_ref, tab_ref, o_ref, shv_ref, cv_ref):
    tm = o_ref.shape[0]
    lane = lax.broadcasted_iota(I32, (SLOTS, tm), 1)

    def columns(n):
        return _token_column(shift_ref, n, lane).astype(I32), _token_column(coef_ref, n, lane)

    def combine(n, buf):
        base = n * SLOTS
        accs = [None] * N_ACC
        for j in range(SLOTS):
            term = jnp.broadcast_to(cv_ref[buf, j:j + 1, :], ROW_TILE) * _expert_row(
                tab_ref, row_ref[base + j], shv_ref[buf, j:j + 1, :])
            accs[j % N_ACC] = term if accs[j % N_ACC] is None else accs[j % N_ACC] + term
        o_ref[n] = (accs[0] + accs[1]) + (accs[2] + accs[3])

    shv_ref[0], cv_ref[0] = columns(0)

    def token_pair(i, carry):
        for k in range(2):
            n = 2 * i + k
            shv_ref[1 - k], cv_ref[1 - k] = columns(n + 1)
            combine(n, k)
        return carry
    lax.fori_loop(0, tm // 2, token_pair, 0)


def _peer_combine(rows_flat, shift, coef, table, n, tm):
    slot_blk = pl.BlockSpec((SLOTS, tm), lambda i: (0, i))
    return pl.pallas_call(
        _peer_combine_kernel,
        grid=(n // tm,),
        in_specs=[pl.BlockSpec((tm * SLOTS,), lambda i: (i,), memory_space=pltpu.SMEM), slot_blk, slot_blk,
                  pl.BlockSpec(memory_space=pltpu.VMEM)],
        out_specs=pl.BlockSpec((tm, *ROW_TILE), lambda i: (i, 0, 0)),
        out_shape=jax.ShapeDtypeStruct((n, *ROW_TILE), F32),
        scratch_shapes=[pltpu.VMEM((2, SLOTS, LANES), I32), pltpu.VMEM((2, SLOTS, LANES), F32)],
        compiler_params=_cparams(("arbitrary",)),
        name="peer_combine",
    )(rows_flat, shift, coef, table)


def _ln3_kernel(h_ref, p_ref, g_ref, b_ref, y_ref):
    y_ref[...] = _layer_norm(DN_ALPHA * h_ref[...] + p_ref[...], g_ref[...], b_ref[...])


def _ln3(h2d, peer2d, g, b, tm):
    n, d = h2d.shape
    tok = pl.BlockSpec((tm, d), lambda i: (i, 0))
    vec = pl.BlockSpec((1, d), lambda i: (0, 0))
    return pl.pallas_call(
        _ln3_kernel,
        grid=(n // tm,),
        in_specs=[tok, tok, vec, vec],
        out_specs=tok,
        out_shape=jax.ShapeDtypeStruct((n, d), F32),
        compiler_params=_cparams(("parallel",)),
        name="ln3",
    )(h2d, peer2d, g.reshape(1, d), b.reshape(1, d))


def _peer_ffn_ln(h2d, w_pq, keys_a, keys_b, tab_u, tab_v, ln3, tm_route, tm_gather):
    n, d = h2d.shape
    rows, shift, gate = _peer_route(h2d, w_pq, keys_a, keys_b, tm_route)
    rows_flat = rows.T.reshape(-1)
    coef = _peer_hidden(rows_flat, h2d.reshape(n, *ROW_TILE), shift, gate, tab_u, tm_gather)
    out = _peer_combine(rows_flat, shift, coef, tab_v, n, tm_gather)
    return _ln3(h2d, out.reshape(n, d), ln3[0], ln3[1], tm_route)


TOKEN_TILE = 512
QUERY_TILE = 128
ROUTE_TILE = 256
GATHER_TILE = 128
SAMPLE_TPAD = 16
PAGES_PER_STEP = 32


def _pad_axis1(a, rows):
    return jnp.pad(a, ((0, 0), (0, rows - a.shape[1])) + ((0, 0),) * (a.ndim - 2))


def kernel(x_prompt, x_sample, mem_prompt, state_ret, cache_k, cache_v, cache_idx_k, cache_mem_k, cache_mem_v,
           page_table, w_in, ret_gn_g, ret_gn_b, w_out, ln1_g, ln1_b, w_cq, w_ck, w_cv, w_co, ln2_g, ln2_b,
           w_pq, peer_keys_a, peer_keys_b, peer_u, peer_v, ln3_g, ln3_b):
    nb, t_len, d = x_prompt.shape
    db, ds, _ = x_sample.shape
    n_pages = page_table.shape[1]
    past = n_pages * PAGE_SIZE
    bf = lambda w: w.astype(BF16)

    w_in_p = _pad_w_in(w_in)
    w_out_b, w_cq_b, w_co_b, w_pq_b = bf(w_out), bf(w_cq), bf(w_co), bf(w_pq)
    keys_a, keys_b = bf(peer_keys_a), bf(peer_keys_b)
    tab_u, tab_v = _pack_expert_table(peer_u), _pack_expert_table(peer_v)
    ln1, ln2, ln3 = (ln1_g, ln1_b), (ln2_g, ln2_b), (ln3_g, ln3_b)

    def tail(x, ret_y, att_o, mem_k, mem_v, tm):
        h2 = _tail1(x, ret_y, att_o, mem_k, mem_v, w_out_b, w_cq_b, w_co_b, ln1, ln2, tm)
        return h2

    def peer(h2d):
        return _peer_ffn_ln(h2d, w_pq_b, keys_a, keys_b, tab_u, tab_v, ln3, ROUTE_TILE, GATHER_TILE)

    n = nb * t_len
    tabs = _rope_tables(jnp.arange(t_len, dtype=I32))
    rq, rk, rv, rg, aq, ak, akb, av, avt, iq, ikw, ik2, iwt = _project(
        x_prompt.reshape(n, d), w_in_p, tabs, t_len // TOKEN_TILE, TOKEN_TILE)
    ret_y, p_state = _retention(rq, rk, rv, rg, jnp.zeros((nb, RET_HEADS, RET_DK, RET_DV), F32),
                                _retention_tables(RET_CHUNK), ret_gn_g, ret_gn_b, nb, t_len // RET_CHUNK)
    att_o = _dsa_prompt(aq, iq, iwt, ik2, akb, avt, nb, t_len, min(IDX_TOPK_MAX, t_len // 4), tq=QUERY_TILE,
                        kc=TOKEN_TILE)
    mem_kv = _matmul(mem_prompt.reshape(nb * MEM_TOKENS, d), jnp.concatenate([bf(w_ck), bf(w_cv)], axis=1), MEM_TOKENS)
    p_mem_k = mem_kv[:, :d].reshape(nb, MEM_TOKENS, d)
    p_mem_v = mem_kv[:, d:].reshape(nb, MEM_TOKENS, d)
    h2 = tail(x_prompt, ret_y.reshape(nb, t_len, RET_W), att_o.reshape(nb, t_len, ATT_W), p_mem_k, p_mem_v, TOKEN_TILE)
    y_prompt = peer(h2.reshape(n, d)).reshape(nb, t_len, d)

    ns = db * ds
    pos_s = jnp.tile(past + jnp.arange(ds, dtype=I32), db)
    sq, sk, sv, sg, saq, sak, _, sav, _, siq, sikw, _, _ = _project(
        x_sample.reshape(ns, d), w_in_p, _rope_tables(pos_s), 1, ns)
    per_seq = lambda a: a.reshape(db, ds, a.shape[-1])
    chunk = lambda a: _pad_axis1(per_seq(a), RET_CHUNK).reshape(db * RET_CHUNK, a.shape[-1])
    ret_y_s, s_state = _retention(chunk(sq), chunk(sk), chunk(sv), chunk(sg), state_ret,
                                  _retention_tables(ds), ret_gn_g, ret_gn_b, db, 1)
    ret_y_s = ret_y_s.reshape(db, RET_CHUNK, RET_W)[:, :SAMPLE_TPAD]

    pt_flat = page_table.reshape(-1)
    iq_rows = _pad_axis1(per_seq(siq), QPAD).reshape(db, QPAD * IDX_HEADS, IDX_DIM)
    w_rows = _pad_axis1(per_seq(sikw[:, IDX_DIM:IDX_DIM + IDX_HEADS]), QPAD).reshape(db, QPAD * IDX_HEADS, 1)
    ik_new = _pad_axis1(per_seq(sikw[:, :IDX_DIM]), LANES)
    sel = _dsa_sample_select(pt_flat, iq_rows, w_rows, ik_new, cache_idx_k, db, n_pages, ds,
                             min(IDX_TOPK_MAX, (past + ds) // 4), PAGES_PER_STEP)
    page_rows = lambda a: a.reshape(a.shape[0], PAGE_SIZE * ATT_HEADS, ATT_HD)
    new_rows = lambda a: page_rows(_pad_axis1(per_seq(a), PAGE_SIZE))
    q_rows = _pad_axis1(per_seq(saq).astype(F32), QPAD).reshape(db, QPAD, ATT_HEADS, ATT_HD)
    q_rows = q_rows.transpose(0, 2, 1, 3).reshape(db, ATT_HEADS * QPAD, ATT_HD)
    att_s = _dsa_sample_attend(pt_flat, q_rows, sel, new_rows(sak), new_rows(sav),
                               page_rows(cache_k), page_rows(cache_v), db, n_pages, PAGES_PER_STEP)
    att_s = att_s.reshape(db, ATT_HEADS, QPAD, ATT_HD).transpose(0, 2, 1, 3).reshape(db, QPAD, ATT_W)
    h2_s = tail(_pad_axis1(x_sample, SAMPLE_TPAD), ret_y_s, _pad_axis1(att_s, SAMPLE_TPAD),
                cache_mem_k, cache_mem_v, SAMPLE_TPAD)
    y_sample = peer(h2_s[:, :ds].reshape(ns, d)).reshape(db, ds, d)

    heads = lambda a, b_, t_: a.reshape(b_, t_, ATT_HEADS, ATT_HD)
    mem_heads = lambda a: a.reshape(nb, MEM_TOKENS, MEM_HEADS, MEM_HD)
    return (y_prompt, y_sample, p_state, heads(ak, nb, t_len), heads(av, nb, t_len),
            ikw[:, :IDX_DIM].reshape(nb, t_len, IDX_DIM), mem_heads(p_mem_k), mem_heads(p_mem_v),
            s_state, heads(sak, db, ds), heads(sav, db, ds), sikw[:, :IDX_DIM].reshape(db, ds, IDX_DIM))
```

```python
import functools

import jax
import jax.numpy as jnp
from jax import lax
from jax.experimental import pallas as pl
from jax.experimental.pallas import tpu as pltpu

F32 = jnp.float32
BF16 = jnp.bfloat16
I32 = jnp.int32

D_MODEL = 1024
DEPTH = 1
PAST_LEN = 8192
PAGE_SIZE = 128
RET_HEADS = 4
RET_DK = 128
RET_DV = 128
RET_W = RET_HEADS * RET_DV
RET_CHUNK = 128
ATT_HEADS = 4
ATT_HD = 128
ATT_W = ATT_HEADS * ATT_HD
IDX_HEADS = 8
IDX_DIM = 64
IDX_TOPK_MAX = 256
MEM_TOKENS = 256
MEM_HEADS = 4
MEM_HD = D_MODEL // MEM_HEADS
PEER_HEADS = 8
PEER_NKEYS = 128
PEER_EXPERTS = PEER_NKEYS * PEER_NKEYS
PEER_QDIM = 256
PEER_HALF = PEER_QDIM // 2
PEER_TOPK = 16
ROPE_THETA = 10000.0
LN_EPS = 1e-5
GN_EPS = 1e-5
DN_ALPHA = (2 * DEPTH) ** 0.25

LANES = 128
SUBLANES = 8
GROUP_W = 512
N_GROUPS = 8
IN_COLS_PAD = N_GROUPS * GROUP_W + LANES
VMEM_LIMIT = 56 * 1024 * 1024
INT_MIN = -(2 ** 31)
NEG_INF = float("-inf")

_NT = (((1,), (1,)), ((), ()))


def _cparams(sem):
    return pltpu.CompilerParams(dimension_semantics=sem, vmem_limit_bytes=VMEM_LIMIT)


def _rope_tables(pos):
    def tab(d):
        half = d // 2
        inv = ROPE_THETA ** (-jnp.arange(half, dtype=F32) * (2.0 / d))
        ang = pos.astype(F32)[:, None] * inv[None, :]
        cos, sin = jnp.cos(ang), jnp.sin(ang)
        reps = LANES // d
        return (jnp.tile(jnp.concatenate([cos, cos], axis=1), (1, reps)),
                jnp.tile(jnp.concatenate([-sin, sin], axis=1), (1, reps)))
    c128, s128 = tab(ATT_HD)
    c64, s64 = tab(IDX_DIM)
    return c128, s128, c64, s64


def _proj_kernel(x_ref, w_ref, wvt_ref, wwt_ref, c128_ref, s128_ref, c64_ref, s64_ref,
                 rq_ref, rk_ref, rv_ref, rg_ref, aq_ref, ak_ref, akb_ref, av_ref, avt_ref,
                 iq_ref, ikw_ref, ik2_ref, iwt_ref):
    xb = x_ref[...].astype(BF16)
    c128 = c128_ref[...]
    s128 = s128_ref[...]
    c64 = c64_ref[...]
    s64 = s64_ref[...]
    tm = xb.shape[0]
    lane = lax.broadcasted_iota(I32, (tm, LANES), 1)
    first_half64 = (lane & (IDX_DIM // 2)) == 0

    def group(g, width=GROUP_W):
        return jnp.dot(xb, w_ref[:, g * GROUP_W:g * GROUP_W + width], preferred_element_type=F32)

    def rope128(p):
        return p * c128 + pltpu.roll(p, ATT_HD // 2, 1) * s128

    def rope64(p):
        rot = jnp.where(first_half64, pltpu.roll(p, LANES - IDX_DIM // 2, 1), pltpu.roll(p, IDX_DIM // 2, 1))
        return p * c64 + rot * s64

    p = group(0)
    for h in range(RET_HEADS):
        sl = slice(h * LANES, (h + 1) * LANES)
        rq_ref[:, sl] = rope128(p[:, sl]).astype(BF16)
    p = group(1)
    for h in range(RET_HEADS):
        sl = slice(h * LANES, (h + 1) * LANES)
        rk_ref[:, sl] = (rope128(p[:, sl]) * (RET_DK ** -0.5)).astype(BF16)
    rv_ref[...] = group(2).astype(BF16)
    rg_ref[...] = group(3)
    p = group(4)
    for h in range(ATT_HEADS):
        sl = slice(h * LANES, (h + 1) * LANES)
        aq_ref[:, sl] = rope128(p[:, sl]).astype(BF16)
    p = group(5)
    for h in range(ATT_HEADS):
        sl = slice(h * LANES, (h + 1) * LANES)
        r = rope128(p[:, sl])
        ak_ref[:, sl] = r
        akb_ref[:, sl] = r.astype(BF16)
    p = group(6)
    av_ref[...] = p
    avt_ref[0] = lax.dot_general(wvt_ref[...], xb, _NT, preferred_element_type=F32).astype(BF16)
    iwt_ref[...] = lax.dot_general(wwt_ref[...], xb, _NT, preferred_element_type=F32)[:IDX_HEADS]
    p = group(7)
    for j in range(GROUP_W // LANES):
        sl = slice(j * LANES, (j + 1) * LANES)
        iq_ref[:, sl] = rope64(p[:, sl]).astype(BF16)
    p = group(8, LANES)
    is_key = lane < IDX_DIM
    r = jnp.where(is_key, rope64(p), p)
    ikw_ref[...] = r
    ik2_ref[...] = jnp.where(is_key, r, pltpu.roll(r, IDX_DIM, 1)).astype(BF16)


def _project(x2d, w_pad, tables, period_tiles, tm):
    n = x2d.shape[0]
    nt = n // tm
    row = lambda i: (i, 0)
    tab = lambda i: (i % period_tiles, 0)
    const = lambda i: (0, 0)
    flat = lambda w, dt: (pl.BlockSpec((tm, w), row), jax.ShapeDtypeStruct((n, w), dt))
    outs = [
        flat(GROUP_W, BF16), flat(GROUP_W, BF16), flat(GROUP_W, BF16), flat(GROUP_W, F32),
        flat(GROUP_W, BF16), flat(GROUP_W, F32), flat(GROUP_W, BF16), flat(GROUP_W, F32),
        (pl.BlockSpec((1, ATT_W, tm), lambda i: (i, 0, 0)), jax.ShapeDtypeStruct((nt, ATT_W, tm), BF16)),
        flat(GROUP_W, BF16), flat(LANES, F32), flat(LANES, BF16),
        (pl.BlockSpec((IDX_HEADS, tm), lambda i: (0, i)), jax.ShapeDtypeStruct((IDX_HEADS, n), F32)),
    ]
    v_cols = slice(6 * GROUP_W, 7 * GROUP_W)
    w_cols = slice(N_GROUPS * GROUP_W + IDX_DIM, N_GROUPS * GROUP_W + IDX_DIM + 2 * IDX_HEADS)
    return pl.pallas_call(
        _proj_kernel,
        grid=(nt,),
        in_specs=[pl.BlockSpec((tm, D_MODEL), row),
                  pl.BlockSpec((D_MODEL, IN_COLS_PAD), const),
                  pl.BlockSpec((GROUP_W, D_MODEL), const), pl.BlockSpec((2 * IDX_HEADS, D_MODEL), const)]
                 + [pl.BlockSpec((tm, LANES), tab)] * 4,
        out_specs=[spec for spec, _ in outs],
        out_shape=[shape for _, shape in outs],
        compiler_params=_cparams(("parallel",)),
        name="proj_rope",
    )(x2d, w_pad, w_pad[:, v_cols].T, w_pad[:, w_cols].T, *tables)


def _pad_w_in(w_in):
    return jnp.pad(w_in.astype(BF16), ((0, 0), (0, IN_COLS_PAD - w_in.shape[1])))


def _retention_log_decay():
    return jnp.log(1.0 - 2.0 ** (-5.0 - jnp.arange(RET_HEADS, dtype=F32)))


def _retention_tables(chunk_len):
    log_g = _retention_log_decay()
    j = jnp.arange(RET_CHUNK, dtype=F32)
    diff = j[:, None] - j[None, :]
    dmat = jnp.where(diff >= 0, jnp.exp(log_g[:, None, None] * jnp.maximum(diff, 0.0)), 0.0)
    q_dec = jnp.exp(log_g[None, :] * (j[:, None] + 1.0))
    k_dec = jnp.exp(log_g[None, :] * (chunk_len - 1.0 - j[:, None]))
    k_dec = jnp.where(j[:, None] < chunk_len, k_dec, 0.0)
    s_dec = jnp.exp(log_g * chunk_len)
    widen = lambda t: jnp.repeat(t, LANES, axis=1)
    return dmat, widen(q_dec), widen(k_dec), s_dec


def _retention_kernel(sdec_ref, q_ref, k_ref, v_ref, g_ref, s0_ref, dmat_ref, qdec_ref, kdec_ref,
                      gng_ref, gnb_ref, y_ref, sout_ref, state_ref):
    c = pl.program_id(1)

    @pl.when(c == 0)
    def _():
        state_ref[...] = s0_ref[0]

    for h in range(RET_HEADS):
        sl = slice(h * LANES, (h + 1) * LANES)
        q = q_ref[:, sl]
        k = k_ref[:, sl]
        v = v_ref[:, sl]
        s_prev = state_ref[h]
        scores = lax.dot_general(q, k, _NT, preferred_element_type=F32) * dmat_ref[h]
        o = jnp.dot(scores.astype(BF16), v, preferred_element_type=F32)
        o = o + jnp.dot(q, s_prev.astype(BF16), preferred_element_type=F32) * qdec_ref[:, sl]
        kd_t = (k.astype(F32) * kdec_ref[:, sl]).T.astype(BF16)
        state_ref[h] = s_prev * sdec_ref[h] + jnp.dot(kd_t, v, preferred_element_type=F32)
        mu = jnp.mean(o, axis=-1, keepdims=True)
        var = jnp.mean(jnp.square(o - mu), axis=-1, keepdims=True)
        n = (o - mu) * lax.rsqrt(var + GN_EPS) * gng_ref[:, sl] + gnb_ref[:, sl]
        g = g_ref[:, sl]
        y_ref[:, sl] = (g * (1.0 / (1.0 + jnp.exp(-g))) * n).astype(BF16)

    @pl.when(c == pl.num_programs(1) - 1)
    def _():
        sout_ref[0] = state_ref[...]


def _retention(q, k, v, g, s0, tables, gn_g, gn_b, nb, nc):
    dmat, qdec, kdec, sdec = tables
    n = q.shape[0]
    row = lambda b, c: (b * nc + c, 0)
    const2 = lambda b, c: (0, 0)
    blk = pl.BlockSpec((RET_CHUNK, RET_W), row)
    return pl.pallas_call(
        _retention_kernel,
        grid=(nb, nc),
        in_specs=[pl.BlockSpec(memory_space=pltpu.SMEM), blk, blk, blk, blk,
                  pl.BlockSpec((1, RET_HEADS, RET_DK, RET_DV), lambda b, c: (b, 0, 0, 0)),
                  pl.BlockSpec((RET_HEADS, RET_CHUNK, RET_CHUNK), lambda b, c: (0, 0, 0)),
                  pl.BlockSpec((RET_CHUNK, RET_W), const2), pl.BlockSpec((RET_CHUNK, RET_W), const2),
                  pl.BlockSpec((1, RET_W), const2), pl.BlockSpec((1, RET_W), const2)],
        out_specs=[blk, pl.BlockSpec((1, RET_HEADS, RET_DK, RET_DV), lambda b, c: (b, 0, 0, 0))],
        out_shape=[jax.ShapeDtypeStruct((n, RET_W), BF16),
                   jax.ShapeDtypeStruct((nb, RET_HEADS, RET_DK, RET_DV), F32)],
        scratch_shapes=[pltpu.VMEM((RET_HEADS, RET_DK, RET_DV), F32)],
        compiler_params=_cparams(("parallel", "arbitrary")),
        name="retention",
    )(sdec, q, k, v, g, s0, dmat, qdec, kdec, gn_g.reshape(1, RET_W), gn_b.reshape(1, RET_W))


def _sort_key(x):
    bits = pltpu.bitcast(x, I32)
    return bits ^ ((bits >> 31) & 0x7FFFFFFF)


def _kth_largest_key(count_ge, shape, topk):
    def bit_body(bi, t):
        cand = t + lax.shift_left(jnp.int32(1), 31 - bi)
        return jnp.where(count_ge(cand) >= topk, cand, t)
    return lax.fori_loop(0, 32, bit_body, jnp.full(shape, INT_MIN, I32))


def _tie_limit(count_eq_below, need, shape, pos_bits):
    def bit_body(bi, p):
        cand = p + lax.shift_left(jnp.int32(1), pos_bits - 1 - bi)
        return jnp.where(count_eq_below(cand) <= need, cand, p)
    return lax.fori_loop(0, pos_bits, bit_body, jnp.zeros(shape, I32))


def _dsa_prompt_kernel(aq_ref, iq_ref, iwt_ref, ik2_ref, ak_ref, avt_ref, o_ref, keys_ref, bias_ref, *,
                       tq, kc, topk, pos_bits):
    i = pl.program_id(1)
    q0 = i * tq
    nk = (q0 + tq + kc - 1) // kc
    groups = kc // SUBLANES
    state = (SUBLANES, tq)
    lane = lax.broadcasted_iota(I32, (tq, LANES), 1)
    key_in_chunk = (lax.broadcasted_iota(I32, (groups, SUBLANES, tq), 0) * SUBLANES
                    + lax.broadcasted_iota(I32, (groups, SUBLANES, tq), 1))
    qpos = q0 + lax.broadcasted_iota(I32, (groups, SUBLANES, tq), 2)
    iq = iq_ref[...]
    iwt = iwt_ref[...]

    def score_chunk(c, carry):
        k0 = pl.multiple_of(c * kc, kc)
        ikc = ik2_ref[pl.ds(k0, kc), :]
        acc = None
        for h in range(IDX_HEADS):
            pair = iq[:, (h // 2) * LANES:(h // 2 + 1) * LANES]
            keep = (lane < IDX_DIM) if h % 2 == 0 else (lane >= IDX_DIM)
            qh = jnp.where(keep, pair, jnp.zeros_like(pair))
            s = lax.dot_general(ikc, qh, _NT, preferred_element_type=F32)
            term = jnp.maximum(s, 0.0) * iwt[h:h + 1, :]
            acc = term if acc is None else acc + term
        key = _sort_key(acc).reshape(groups, SUBLANES, tq)
        keys_ref[c] = jnp.where(k0 + key_in_chunk <= qpos, key, INT_MIN)
        return carry
    lax.fori_loop(0, nk, score_chunk, 0)

    def total(a):
        return jnp.broadcast_to(jnp.sum(a, axis=0, keepdims=True), state)

    def count_ge(cand):
        body = lambda c, a: a + jnp.sum(jnp.where(keys_ref[c] >= cand[None], 1, 0), axis=0)
        return total(lax.fori_loop(0, nk, body, jnp.zeros(state, I32)))

    t = _kth_largest_key(count_ge, state, topk)
    n_ge = count_ge(t)
    need = topk - count_ge(t + 1)

    def count_eq_below(p):
        def body(c, a):
            hit = (keys_ref[c] == t[None]) & (c * kc + key_in_chunk < p[None])
            return a + jnp.sum(jnp.where(hit, 1, 0), axis=0)
        return total(lax.fori_loop(0, nk, body, jnp.zeros(state, I32)))

    has_excess = jnp.max(jnp.where(n_ge > topk, 1, 0)) > 0
    plim = lax.cond(has_excess,
                    lambda: _tie_limit(count_eq_below, need, state, pos_bits),
                    lambda: jnp.full(state, 2 ** pos_bits, I32))

    def bias_chunk(c, carry):
        key = keys_ref[c]
        sel = (key > t[None]) | ((key == t[None]) & (c * kc + key_in_chunk < plim[None]))
        bias_ref[c] = jnp.where(sel & (key != INT_MIN), 0.0, NEG_INF)
        return carry
    lax.fori_loop(0, nk, bias_chunk, 0)

    scale = ATT_HD ** -0.5
    head_cols = [slice(h * LANES, (h + 1) * LANES) for h in range(ATT_HEADS)]
    queries = [aq_ref[:, sl] for sl in head_cols]

    def attend_chunk(c, carry):
        k0 = pl.multiple_of(c * kc, kc)
        bias = bias_ref[c].reshape(kc, tq)
        out = []
        for h, sl in enumerate(head_cols):
            m, l, acc = carry[h]
            s = lax.dot_general(ak_ref[pl.ds(k0, kc), sl], queries[h], _NT, preferred_element_type=F32)
            s = s * scale + bias
            m_new = jnp.maximum(m, jnp.max(s, axis=0, keepdims=True))
            m_safe = jnp.where(m_new == NEG_INF, 0.0, m_new)
            p = jnp.exp(s - m_safe)
            alpha = jnp.exp(m - m_safe)
            l_new = alpha * l + jnp.sum(p, axis=0, keepdims=True)
            acc_new = alpha * acc + jnp.dot(avt_ref[c, sl, :], p.astype(BF16), preferred_element_type=F32)
            out.append((m_new, l_new, acc_new))
        return tuple(out)

    init = tuple((jnp.full((1, tq), NEG_INF, F32), jnp.zeros((1, tq), F32), jnp.zeros((ATT_HD, tq), F32))
                 for _ in head_cols)
    fin = lax.fori_loop(0, nk, attend_chunk, init)
    for (m, l, acc), sl in zip(fin, head_cols):
        o_ref[:, sl] = (acc / l).T.astype(BF16)


def _dsa_prompt(aq, iq, iwt, ik2, akb, avt, nb, t_len, topk, tq=LANES, kc=512):
    n = aq.shape[0]
    nq = t_len // tq
    qrow = lambda b, i: (b * nq + i, 0)
    brow = lambda b, i: (b, 0)
    kern = functools.partial(_dsa_prompt_kernel, tq=tq, kc=kc, topk=topk, pos_bits=(t_len - 1).bit_length() + 1)
    chunked = (t_len // kc, kc // SUBLANES, SUBLANES, tq)
    return pl.pallas_call(
        kern,
        grid=(nb, nq),
        in_specs=[pl.BlockSpec((tq, ATT_W), qrow), pl.BlockSpec((tq, GROUP_W), qrow),
                  pl.BlockSpec((IDX_HEADS, tq), lambda b, i: (0, b * nq + i)),
                  pl.BlockSpec((t_len, LANES), brow), pl.BlockSpec((t_len, ATT_W), brow),
                  pl.BlockSpec((t_len // kc, ATT_W, kc), lambda b, i: (b, 0, 0))],
        out_specs=pl.BlockSpec((tq, ATT_W), qrow),
        out_shape=jax.ShapeDtypeStruct((n, ATT_W), BF16),
        scratch_shapes=[pltpu.VMEM(chunked, I32), pltpu.VMEM(chunked, F32)],
        compiler_params=_cparams(("parallel", "arbitrary")),
        name="dsa_prompt",
    )(aq, iq, iwt, ik2, akb, avt)


QPAD = SUBLANES


def _dsa_sample_select_kernel(pt_ref, iq_ref, w_ref, iknew_ref, expand_ref, *rest, pg, n_pages, n_new, topk,
                              pos_bits):
    idx_refs = rest[:pg]
    sel_ref = rest[pg]
    keys_ref = rest[pg + 1]
    p = pl.program_id(1)
    iq = iq_ref[0]
    w = w_ref[0]
    key_in_block = lax.broadcasted_iota(I32, (QPAD, LANES), 1)
    row = lax.broadcasted_iota(I32, (QPAD, LANES), 0)

    def scores(keys_t):
        s = jnp.dot(iq, keys_t.astype(BF16), preferred_element_type=F32)
        r = jnp.maximum(s, 0.0) * w
        return jnp.sum(r.reshape(QPAD, IDX_HEADS, LANES), axis=1)

    for j in range(pg):
        keys_ref[p * pg + j] = _sort_key(scores(idx_refs[j][0]))

    @pl.when(p == pl.num_programs(1) - 1)
    def _():
        s_new = scores(iknew_ref[0])
        keys_ref[n_pages] = jnp.where((key_in_block <= row) & (key_in_block < n_new), _sort_key(s_new), INT_MIN)
        keys_ref[n_pages + 1] = jnp.full((QPAD, LANES), INT_MIN, I32)
        keys = keys_ref[...]
        pos = lax.broadcasted_iota(I32, keys.shape, 0) * LANES + key_in_block[None]

        def lane_total(a):
            return jnp.broadcast_to(jnp.sum(a, axis=1, keepdims=True), (QPAD, LANES))

        def count_ge(cand):
            return lane_total(jnp.sum(jnp.where(keys >= cand[None], 1, 0), axis=0))

        t = _kth_largest_key(count_ge, (QPAD, LANES), topk)
        need = topk - count_ge(t + 1)

        def count_eq_below(lim):
            return lane_total(jnp.sum(jnp.where((keys == t[None]) & (pos < lim[None]), 1, 0), axis=0))

        has_excess = jnp.max(jnp.where(count_ge(t) > topk, 1, 0)) > 0
        plim = lax.cond(has_excess,
                        lambda: _tie_limit(count_eq_below, need, (QPAD, LANES), pos_bits),
                        lambda: jnp.full((QPAD, LANES), 2 ** pos_bits, I32))

        sel = ((keys > t[None]) | ((keys == t[None]) & (pos < plim[None]))) & (keys != INT_MIN)
        flags = jnp.where(sel, 1.0, 0.0).reshape(keys.shape[0] * QPAD, LANES).astype(BF16)
        out = jnp.dot(flags, expand_ref[...], preferred_element_type=F32)
        sel_ref[0] = out.reshape(keys.shape[0], QPAD, out.shape[1])


def _dsa_sample_select(pt_flat, iq_rows, w_rows, ik_new, cache_idx_k, nb, n_pages, n_new, topk, pg):
    kern = functools.partial(_dsa_sample_select_kernel, pg=pg, n_pages=n_pages, n_new=n_new, topk=topk,
                             pos_bits=((n_pages + 1) * LANES).bit_length())
    per_b = lambda shape: pl.BlockSpec((1,) + shape, lambda b, p, pt: (b,) + (0,) * len(shape))
    keys_minor = lambda a: a.transpose(0, 2, 1)
    page = lambda j: pl.BlockSpec((1, IDX_DIM, PAGE_SIZE), lambda b, p, pt: (pt[b * n_pages + p * pg + j], 0, 0))
    cols = PAGE_SIZE * ATT_HEADS
    expand = (jnp.arange(cols)[None, :] // ATT_HEADS == jnp.arange(PAGE_SIZE)[:, None]).astype(BF16)
    nblk = n_pages + 2
    return pl.pallas_call(
        kern,
        grid_spec=pltpu.PrefetchScalarGridSpec(
            num_scalar_prefetch=1,
            grid=(nb, n_pages // pg),
            in_specs=[per_b((QPAD * IDX_HEADS, IDX_DIM)), per_b((QPAD * IDX_HEADS, 1)), per_b((IDX_DIM, PAGE_SIZE)),
                      pl.BlockSpec((PAGE_SIZE, cols), lambda b, p, pt: (0, 0))]
                     + [page(j) for j in range(pg)],
            out_specs=per_b((nblk, QPAD, cols)),
            scratch_shapes=[pltpu.VMEM((nblk, QPAD, LANES), I32)]),
        out_shape=jax.ShapeDtypeStruct((nb, nblk, QPAD, cols), F32),
        compiler_params=_cparams(("parallel", "arbitrary")),
        name="dsa_sample_select",
    )(pt_flat, iq_rows, w_rows, keys_minor(ik_new), expand, *([keys_minor(cache_idx_k)] * pg))


def _dsa_sample_attend_kernel(pt_ref, q_ref, sel_ref, knew_ref, vnew_ref, *rest, pg, n_pages):
    k_refs = rest[:pg]
    v_refs = rest[pg:2 * pg]
    o_ref = rest[2 * pg]
    m_ref, l_ref, acc_ref = rest[2 * pg + 1:]
    p = pl.program_id(1)
    scale = ATT_HD ** -0.5
    rows = ATT_HEADS * QPAD
    cols = PAGE_SIZE * ATT_HEADS

    @pl.when(p == 0)
    def _():
        m_ref[...] = jnp.full(m_ref.shape, NEG_INF, F32)
        l_ref[...] = jnp.zeros(l_ref.shape, F32)
        acc_ref[...] = jnp.zeros(acc_ref.shape, F32)

    q = q_ref[0].astype(BF16)
    own_head = ((lax.broadcasted_iota(I32, (rows, cols), 1) & (ATT_HEADS - 1))
                == (lax.broadcasted_iota(I32, (rows, cols), 0) >> (QPAD.bit_length() - 1)))

    def update(blocks):
        scores = []
        for flags, kp, _ in blocks:
            s = lax.dot_general(q, kp.astype(BF16), _NT, preferred_element_type=F32) * scale
            mask = own_head & (jnp.concatenate([flags] * ATT_HEADS, axis=0) > 0.5)
            scores.append(jnp.where(mask, s, NEG_INF))
        mc = scores[0]
        for s in scores[1:]:
            mc = jnp.maximum(mc, s)
        m = m_ref[...]
        m_new = jnp.maximum(m, jnp.max(mc, axis=1, keepdims=True))
        m_safe = jnp.where(m_new == NEG_INF, 0.0, m_new)
        m_row = m_safe[:, :1]
        psum = jnp.zeros((rows, cols), F32)
        acc = jnp.exp(m - m_safe) * acc_ref[...]
        for (_, _, vp), s in zip(blocks, scores):
            pr = jnp.exp(s - m_row)
            psum = psum + pr
            acc = acc + jnp.dot(pr.astype(BF16), vp.astype(BF16), preferred_element_type=F32)
        l_ref[...] = jnp.exp(m - m_safe) * l_ref[...] + jnp.sum(psum, axis=1, keepdims=True)
        acc_ref[...] = acc
        m_ref[...] = m_new

    update([(sel_ref[0, p * pg + j], k_refs[j][0], v_refs[j][0]) for j in range(pg)])

    @pl.when(p == pl.num_programs(1) - 1)
    def _():
        update([(sel_ref[0, n_pages], knew_ref[0], vnew_ref[0])])
        o_ref[0] = acc_ref[...] / l_ref[...]


def _dsa_sample_attend(pt_flat, q_rows, sel, k_new, v_new, cache_k, cache_v, nb, n_pages, pg):
    kern = functools.partial(_dsa_sample_attend_kernel, pg=pg, n_pages=n_pages)
    rows = ATT_HEADS * QPAD
    cols = PAGE_SIZE * ATT_HEADS
    per_b = lambda shape: pl.BlockSpec((1,) + shape, lambda b, p, pt: (b,) + (0,) * len(shape))
    page = lambda j: pl.BlockSpec((1, cols, ATT_HD), lambda b, p, pt: (pt[b * n_pages + p * pg + j], 0, 0))
    return pl.pallas_call(
        kern,
        grid_spec=pltpu.PrefetchScalarGridSpec(
            num_scalar_prefetch=1,
            grid=(nb, n_pages // pg),
            in_specs=[per_b((rows, ATT_HD)), per_b((n_pages + 2, QPAD, cols)), per_b((cols, ATT_HD)),
                      per_b((cols, ATT_HD))] + [page(j) for j in range(pg)] * 2,
            out_specs=per_b((rows, ATT_HD)),
            scratch_shapes=[pltpu.VMEM((rows, LANES), F32)] * 3),
        out_shape=jax.ShapeDtypeStruct((nb, rows, ATT_HD), F32),
        compiler_params=_cparams(("parallel", "arbitrary")),
        name="dsa_sample_attend",
    )(pt_flat, q_rows, sel, k_new, v_new, *([cache_k] * pg), *([cache_v] * pg))


def _layer_norm(x, g, b):
    mu = jnp.mean(x, axis=-1, keepdims=True)
    var = jnp.mean(jnp.square(x - mu), axis=-1, keepdims=True)
    return (x - mu) * lax.rsqrt(var + LN_EPS) * g + b


def _matmul_kernel(x_ref, w_ref, o_ref):
    o_ref[...] = jnp.dot(x_ref[...].astype(BF16), w_ref[...], preferred_element_type=F32)


def _matmul(x, w_bf16, tm):
    n, kdim = x.shape
    m = w_bf16.shape[1]
    return pl.pallas_call(
        _matmul_kernel,
        grid=(n // tm,),
        in_specs=[pl.BlockSpec((tm, kdim), lambda i: (i, 0)), pl.BlockSpec((kdim, m), lambda i: (0, 0))],
        out_specs=pl.BlockSpec((tm, m), lambda i: (i, 0)),
        out_shape=jax.ShapeDtypeStruct((n, m), F32),
        compiler_params=_cparams(("parallel",)),
        name="mem_proj",
    )(x, w_bf16)


def _tail1_kernel(x_ref, ry_ref, ao_ref, mk_ref, mv_ref, wout_ref, wcq_ref, wco_ref,
                  ln1g_ref, ln1b_ref, ln2g_ref, ln2b_ref, h2_ref, *, head_rows):
    x = x_ref[0]
    tm = x.shape[0]
    mix = jnp.dot(ry_ref[0].astype(BF16), wout_ref[:RET_W, :], preferred_element_type=F32)
    mix = mix + jnp.dot(ao_ref[0].astype(BF16), wout_ref[RET_W:, :], preferred_element_type=F32)
    h1 = _layer_norm(DN_ALPHA * x + mix, ln1g_ref[...], ln1b_ref[...])
    q = jnp.dot(h1.astype(BF16), wcq_ref[...], preferred_element_type=F32).astype(BF16)
    head_cols = [slice(h * MEM_HD, (h + 1) * MEM_HD) for h in range(MEM_HEADS)]
    if head_rows:
        q_rows = jnp.concatenate([q[:, sl] for sl in head_cols], axis=0)
        s = lax.dot_general(q_rows, mk_ref[0].astype(BF16), _NT, preferred_element_type=F32) * (MEM_HD ** -0.5)
        assert tm & (tm - 1) == 0, "stacked-head rows are split by a shift"
        own_head = ((lax.broadcasted_iota(I32, s.shape, 1) & (MEM_HEADS - 1))
                    == lax.broadcasted_iota(I32, s.shape, 0) >> (tm.bit_length() - 1))
        s = jnp.where(own_head, s, NEG_INF)
        e = jnp.exp(s - jnp.max(s, axis=-1, keepdims=True))
        p = e / jnp.sum(e, axis=-1, keepdims=True)
        o = jnp.dot(p.astype(BF16), mv_ref[0].astype(BF16), preferred_element_type=F32).astype(BF16)
        heads = [o[h * tm:(h + 1) * tm] for h in range(MEM_HEADS)]
    else:
        heads = []
        for sl in head_cols:
            mk = mk_ref[0, :, sl].astype(BF16)
            mv = mv_ref[0, :, sl].astype(BF16)
            s = lax.dot_general(q[:, sl], mk, _NT, preferred_element_type=F32) * (MEM_HD ** -0.5)
            e = jnp.exp(s - jnp.max(s, axis=-1, keepdims=True))
            p = e / jnp.sum(e, axis=-1, keepdims=True)
            heads.append(jnp.dot(p.astype(BF16), mv, preferred_element_type=F32).astype(BF16))
    ca = jnp.concatenate(heads, axis=1)
    h2 = DN_ALPHA * h1 + jnp.dot(ca, wco_ref[...], preferred_element_type=F32)
    h2_ref[0] = _layer_norm(h2, ln2g_ref[...], ln2b_ref[...])


def _tail1(x, ret_y, att_o, mem_k, mem_v, w_out, w_cq, w_co, ln1, ln2, tm):
    g, tg, d = x.shape
    head_rows = mem_k.ndim == 4
    if head_rows:
        mem_k = mem_k.reshape(g, MEM_TOKENS * MEM_HEADS, MEM_HD)
        mem_v = mem_v.reshape(g, MEM_TOKENS * MEM_HEADS, MEM_HD)
    tok = lambda w: pl.BlockSpec((1, tm, w), lambda b, i: (b, i, 0))
    mem = pl.BlockSpec((1,) + mem_k.shape[1:], lambda b, i: (b, 0, 0))
    wsp = lambda r, c: pl.BlockSpec((r, c), lambda b, i: (0, 0))
    vec = wsp(1, d)
    return pl.pallas_call(
        functools.partial(_tail1_kernel, head_rows=head_rows),
        grid=(g, tg // tm),
        in_specs=[tok(d), tok(RET_W), tok(ATT_W), mem, mem, wsp(d, d), wsp(d, d), wsp(d, d), vec, vec, vec, vec],
        out_specs=tok(d),
        out_shape=jax.ShapeDtypeStruct((g, tg, d), F32),
        compiler_params=_cparams(("parallel", "arbitrary")),
        name="tail1",
    )(x, ret_y, att_o, mem_k, mem_v, w_out, w_cq, w_co,
      ln1[0].reshape(1, d), ln1[1].reshape(1, d), ln2[0].reshape(1, d), ln2[1].reshape(1, d))


HALF_EXPERTS = PEER_EXPERTS // 2
HALF_BIT = HALF_EXPERTS.bit_length() - 1
ROW_TILE = (SUBLANES, LANES)
SLOTS = PEER_HEADS * PEER_TOPK
HIGH_HALF = -65536


PACK_PAIRS = 64


def _pack_kernel(lo_ref, hi_ref, o_ref):
    def bf16_bits(x):
        return pltpu.bitcast(x.astype(BF16).astype(F32), I32)

    for g in range(PACK_PAIRS // SUBLANES):
        rows = slice(g * SUBLANES, (g + 1) * SUBLANES)
        word = lax.shift_right_logical(bf16_bits(lo_ref[rows, :]), 16) | bf16_bits(hi_ref[rows, :])
        for s in range(SUBLANES):
            o_ref[pl.ds(g * SUBLANES * SUBLANES + s, SUBLANES, stride=SUBLANES), :] = word[:, s * LANES:(s + 1) * LANES]


def _pack_expert_table(w):
    d = w.shape[1]
    steps = HALF_EXPERTS // PACK_PAIRS
    return pl.pallas_call(
        _pack_kernel,
        grid=(steps,),
        in_specs=[pl.BlockSpec((PACK_PAIRS, d), lambda i: (i, 0)),
                  pl.BlockSpec((PACK_PAIRS, d), lambda i: (i + steps, 0))],
        out_specs=pl.BlockSpec((PACK_PAIRS * SUBLANES, LANES), lambda i: (i, 0)),
        out_shape=jax.ShapeDtypeStruct((HALF_EXPERTS * SUBLANES, LANES), I32),
        compiler_params=_cparams(("parallel",)),
        name="pack_table",
    )(w, w)


def _top_rows(s, k, payload=None, order=None):
    rid = lax.broadcasted_iota(I32, s.shape, 0).astype(F32)
    order = rid if order is None else order
    big = jnp.float32(2.0 ** 30)
    vals, pays = [], []
    for _ in range(k):
        m = jnp.max(s, axis=0, keepdims=True)
        win = jnp.min(jnp.where(s == m, order, big), axis=0, keepdims=True)
        hit = order == win
        vals.append(m)
        pays.append(win if payload is None else jnp.max(jnp.where(hit, payload, -1.0), axis=0, keepdims=True))
        s = jnp.where(hit, NEG_INF, s)
    return jnp.concatenate(vals, axis=0), jnp.concatenate(pays, axis=0)


def _peer_route_kernel(h_ref, wpq_ref, ka_ref, kb_ref, row_ref, shift_ref, gate_ref):
    hb = h_ref[...].astype(BF16)
    tm = hb.shape[0]
    ka = ka_ref[...]
    kb = kb_ref[...]
    a8 = lax.broadcasted_iota(I32, (SUBLANES, tm), 0).astype(F32)
    b16 = lax.broadcasted_iota(I32, (PEER_TOPK, tm), 0).astype(F32)
    for hd in range(PEER_HEADS):
        q = jnp.dot(hb, wpq_ref[:, hd * PEER_QDIM:(hd + 1) * PEER_QDIM], preferred_element_type=F32).astype(BF16)
        s1 = lax.dot_general(ka, q[:, :PEER_HALF], _NT, preferred_element_type=F32)
        s2 = lax.dot_general(kb, q[:, PEER_HALF:], _NT, preferred_element_type=F32)
        v1, i1 = _top_rows(s1, PEER_TOPK)
        v2, i2 = _top_rows(s2, PEER_TOPK)
        cand, cid, fid = [], [], []

        def block(vals, ids, flat, valid):
            cand.append(jnp.where(valid, vals, NEG_INF) if valid is not None else vals)
            cid.append(ids)
            fid.append(flat)

        block(v1[0:1] + v2, i1[0:1] * PEER_NKEYS + i2, b16, None)
        block(v1[1:2] + v2[:8], i1[1:2] * PEER_NKEYS + i2[:8], PEER_TOPK + a8, None)
        block(v1[8:] + v2[0:1], i1[8:] * PEER_NKEYS + i2[0:1], (a8 + 8) * PEER_TOPK, None)
        for b in range(5):
            lim = PEER_TOPK // (b + 1)
            block(v1[:8] + v2[b:b + 1], i1[:8] * PEER_NKEYS + i2[b:b + 1], a8 * PEER_TOPK + b,
                  (a8 >= 2) & (a8 < lim))
        sc, eid = _top_rows(jnp.concatenate(cand, axis=0), PEER_TOPK,
                            payload=jnp.concatenate(cid, axis=0), order=jnp.concatenate(fid, axis=0))
        eid = eid.astype(I32)
        e = jnp.exp(sc - sc[0:1])
        rows = slice(hd * PEER_TOPK, (hd + 1) * PEER_TOPK)
        gate_ref[rows, :] = e / jnp.sum(e, axis=0, keepdims=True)
        row_ref[rows, :] = (eid & (HALF_EXPERTS - 1)) * SUBLANES
        shift_ref[rows, :] = (16 - ((eid >> HALF_BIT) << 4)).astype(F32)


def _peer_route(h2d, w_pq, keys_a, keys_b, tm):
    n, d = h2d.shape
    slot_blk = pl.BlockSpec((SLOTS, tm), lambda i: (0, i))
    return pl.pallas_call(
        _peer_route_kernel,
        grid=(n // tm,),
        in_specs=[pl.BlockSpec((tm, d), lambda i: (i, 0)),
                  pl.BlockSpec((d, PEER_HEADS * PEER_QDIM), lambda i: (0, 0)),
                  pl.BlockSpec((PEER_NKEYS, PEER_HALF), lambda i: (0, 0)),
                  pl.BlockSpec((PEER_NKEYS, PEER_HALF), lambda i: (0, 0))],
        out_specs=[slot_blk, slot_blk, slot_blk],
        out_shape=[jax.ShapeDtypeStruct((SLOTS, n), I32), jax.ShapeDtypeStruct((SLOTS, n), F32),
                   jax.ShapeDtypeStruct((SLOTS, n), F32)],
        compiler_params=_cparams(("parallel",)),
        name="peer_route",
    )(h2d, w_pq, keys_a, keys_b)


def _token_column(ref, n, lane):
    col = jnp.sum(jnp.where(lane == n, ref[...], 0.0), axis=1, keepdims=True)
    return jnp.broadcast_to(col, (SLOTS, LANES))


def _expert_row(tab_ref, row, shift_row):
    word = tab_ref[pl.ds(pl.multiple_of(row, SUBLANES), SUBLANES), :]
    return pltpu.bitcast(lax.shift_left(word, jnp.broadcast_to(shift_row, ROW_TILE)) & jnp.int32(HIGH_HALF), F32)


def _fold_sublanes(tiles, sub):
    step = SUBLANES // 2
    while len(tiles) > 1:
        half = len(tiles) // 2
        low = (sub & step) == 0
        nxt = []
        for i in range(half):
            a, b = tiles[i], tiles[i + half]
            if 2 * step == SUBLANES:
                folded = jnp.where(low, a, b) + pltpu.roll(jnp.where(low, b, a), step, 0)
            else:
                folded = jnp.where(low, a + pltpu.roll(a, SUBLANES - step, 0), b + pltpu.roll(b, step, 0))
            nxt.append(folded)
        tiles = nxt
        step //= 2
    return tiles[0]


TOKENS_PER_STEP = 8


def _peer_hidden_kernel(row_ref, x_ref, shift_ref, gate_ref, tab_ref, coef_ref, shv_ref, part_ref, hid_ref):
    tm = x_ref.shape[0]
    lane = lax.broadcasted_iota(I32, (SLOTS, tm), 1)
    sub = lax.broadcasted_iota(I32, ROW_TILE, 0)

    def shifts(n):
        return _token_column(shift_ref, n, lane).astype(I32)

    def gather(n, buf):
        x = x_ref[n]
        base = n * SLOTS
        for g in range(SLOTS // SUBLANES):
            prods = []
            for r in range(SUBLANES):
                j = g * SUBLANES + r
                prods.append(_expert_row(tab_ref, row_ref[base + j], shv_ref[buf, j:j + 1, :]) * x)
            part_ref[n, g * SUBLANES:(g + 1) * SUBLANES, :] = _fold_sublanes(prods, sub)

    shv_ref[0] = shifts(0)

    def token_group(i, carry):
        for k in range(TOKENS_PER_STEP):
            n = TOKENS_PER_STEP * i + k
            shv_ref[(k + 1) % 2] = shifts(n + 1)
            gather(n, k % 2)
        return carry
    lax.fori_loop(0, tm // TOKENS_PER_STEP, token_group, 0)

    hid_ref[...] = jnp.zeros((SLOTS, tm), F32)

    def reduce_tokens(i, carry):
        hid = hid_ref[...]
        for r in range(SUBLANES):
            n = i * SUBLANES + r
            hid = jnp.where(lane == n, jnp.sum(part_ref[n], axis=1, keepdims=True), hid)
        hid_ref[...] = hid
        return carry
    lax.fori_loop(0, tm // SUBLANES, reduce_tokens, 0)
    hid = hid_ref[...]
    gelu = 0.5 * hid * (1.0 + lax.erf(hid * (2.0 ** -0.5)))
    coef_ref[...] = gate_ref[...] * gelu


def _peer_hidden(rows_flat, x_tiles, shift, gate, table, tm):
    n = x_tiles.shape[0]
    slot_blk = pl.BlockSpec((SLOTS, tm), lambda i: (0, i))
    return pl.pallas_call(
        _peer_hidden_kernel,
        grid=(n // tm,),
        in_specs=[pl.BlockSpec((tm * SLOTS,), lambda i: (i,), memory_space=pltpu.SMEM),
                  pl.BlockSpec((tm, *ROW_TILE), lambda i: (i, 0, 0)),
                  slot_blk, slot_blk,
                  pl.BlockSpec(memory_space=pltpu.VMEM)],
        out_specs=slot_blk,
        out_shape=jax.ShapeDtypeStruct((SLOTS, n), F32),
        scratch_shapes=[pltpu.VMEM((2, SLOTS, LANES), I32), pltpu.VMEM((tm, SLOTS, LANES), F32),
                        pltpu.VMEM((SLOTS, tm), F32)],
        compiler_params=_cparams(("arbitrary",)),
        name="peer_hidden",
    )(rows_flat, x_tiles, shift, gate, table)


N_ACC = 4


def _peer_combine_kernel(row_ref, shift_ref, coef_ref, tab_ref, o_ref, shv_ref, cv_ref):
    tm = o_ref.shape[0]
    lane = lax.broadcasted_iota(I32, (SLOTS, tm), 1)

    def columns(n):
        return _token_column(shift_ref, n, lane).astype(I32), _token_column(coef_ref, n, lane)

    def combine(n, buf):
        base = n * SLOTS
        accs = [None] * N_ACC
        for j in range(SLOTS):
            term = jnp.broadcast_to(cv_ref[buf, j:j + 1, :], ROW_TILE) * _expert_row(
                tab_ref, row_ref[base + j], shv_ref[buf, j:j + 1, :])
            accs[j % N_ACC] = term if accs[j % N_ACC] is None else accs[j % N_ACC] + term
        o_ref[n] = (accs[0] + accs[1]) + (accs[2] + accs[3])

    shv_ref[0], cv_ref[0] = columns(0)

    def token_pair(i, carry):
        for k in range(2):
            n = 2 * i + k
            shv_ref[1 - k], cv_ref[1 - k] = columns(n + 1)
            combine(n, k)
        return carry
    lax.fori_loop(0, tm // 2, token_pair, 0)


def _peer_combine(rows_flat, shift, coef, table, n, tm):
    slot_blk = pl.BlockSpec((SLOTS, tm), lambda i: (0, i))
    return pl.pallas_call(
        _peer_combine_kernel,
        grid=(n // tm,),
        in_specs=[pl.BlockSpec((tm * SLOTS,), lambda i: (i,), memory_space=pltpu.SMEM), slot_blk, slot_blk,
                  pl.BlockSpec(memory_space=pltpu.VMEM)],
        out_specs=pl.BlockSpec((tm, *ROW_TILE), lambda i: (i, 0, 0)),
        out_shape=jax.ShapeDtypeStruct((n, *ROW_TILE), F32),
        scratch_shapes=[pltpu.VMEM((2, SLOTS, LANES), I32), pltpu.VMEM((2, SLOTS, LANES), F32)],
        compiler_params=_cparams(("arbitrary",)),
        name="peer_combine",
    )(rows_flat, shift, coef, table)


def _ln3_kernel(h_ref, p_ref, g_ref, b_ref, y_ref):
    y_ref[...] = _layer_norm(DN_ALPHA * h_ref[...] + p_ref[...], g_ref[...], b_ref[...])


def _ln3(h2d, peer2d, g, b, tm):
    n, d = h2d.shape
    tok = pl.BlockSpec((tm, d), lambda i: (i, 0))
    vec = pl.BlockSpec((1, d), lambda i: (0, 0))
    return pl.pallas_call(
        _ln3_kernel,
        grid=(n // tm,),
        in_specs=[tok, tok, vec, vec],
        out_specs=tok,
        out_shape=jax.ShapeDtypeStruct((n, d), F32),
        compiler_params=_cparams(("parallel",)),
        name="ln3",
    )(h2d, peer2d, g.reshape(1, d), b.reshape(1, d))


def _peer_ffn_ln(h2d, w_pq, keys_a, keys_b, tab_u, tab_v, ln3, tm_route, tm_gather):
    n, d = h2d.shape
    rows, shift, gate = _peer_route(h2d, w_pq, keys_a, keys_b, tm_route)
    rows_flat = rows.T.reshape(-1)
    coef = _peer_hidden(rows_flat, h2d.reshape(n, *ROW_TILE), shift, gate, tab_u, tm_gather)
    out = _peer_combine(rows_flat, shift, coef, tab_v, n, tm_gather)
    return _ln3(h2d, out.reshape(n, d), ln3[0], ln3[1], tm_route)


TOKEN_TILE = 512
QUERY_TILE = 128
ROUTE_TILE = 256
GATHER_TILE = 128
SAMPLE_TPAD = 16
PAGES_PER_STEP = 32


def _pad_axis1(a, rows):
    return jnp.pad(a, ((0, 0), (0, rows - a.shape[1])) + ((0, 0),) * (a.ndim - 2))


def kernel(x_prompt, x_sample, mem_prompt, state_ret, cache_k, cache_v, cache_idx_k, cache_mem_k, cache_mem_v,
           page_table, w_in, ret_gn_g, ret_gn_b, w_out, ln1_g, ln1_b, w_cq, w_ck, w_cv, w_co, ln2_g, ln2_b,
           w_pq, peer_keys_a, peer_keys_b, peer_u, peer_v, ln3_g, ln3_b):
    nb, t_len, d = x_prompt.shape
    db, ds, _ = x_sample.shape
    n_pages = page_table.shape[1]
    past = n_pages * PAGE_SIZE
    bf = lambda w: w.astype(BF16)

    w_in_p = _pad_w_in(w_in)
    w_out_b, w_cq_b, w_co_b, w_pq_b = bf(w_out), bf(w_cq), bf(w_co), bf(w_pq)
    keys_a, keys_b = bf(peer_keys_a), bf(peer_keys_b)
    tab_u, tab_v = _pack_expert_table(peer_u), _pack_expert_table(peer_v)
    ln1, ln2, ln3 = (ln1_g, ln1_b), (ln2_g, ln2_b), (ln3_g, ln3_b)

    def tail(x, ret_y, att_o, mem_k, mem_v, tm):
        h2 = _tail1(x, ret_y, att_o, mem_k, mem_v, w_out_b, w_cq_b, w_co_b, ln1, ln2, tm)
        return h2

    def peer(h2d):
        return _peer_ffn_ln(h2d, w_pq_b, keys_a, keys_b, tab_u, tab_v, ln3, ROUTE_TILE, GATHER_TILE)

    n = nb * t_len
    tabs = _rope_tables(jnp.arange(t_len, dtype=I32))
    rq, rk, rv, rg, aq, ak, akb, av, avt, iq, ikw, ik2, iwt = _project(
        x_prompt.reshape(n, d), w_in_p, tabs, t_len // TOKEN_TILE, TOKEN_TILE)
    ret_y, p_state = _retention(rq, rk, rv, rg, jnp.zeros((nb, RET_HEADS, RET_DK, RET_DV), F32),
                                _retention_tables(RET_CHUNK), ret_gn_g, ret_gn_b, nb, t_len // RET_CHUNK)
    att_o = _dsa_prompt(aq, iq, iwt, ik2, akb, avt, nb, t_len, min(IDX_TOPK_MAX, t_len // 4), tq=QUERY_TILE,
                        kc=TOKEN_TILE)
    mem_kv = _matmul(mem_prompt.reshape(nb * MEM_TOKENS, d), jnp.concatenate([bf(w_ck), bf(w_cv)], axis=1), MEM_TOKENS)
    p_mem_k = mem_kv[:, :d].reshape(nb, MEM_TOKENS, d)
    p_mem_v = mem_kv[:, d:].reshape(nb, MEM_TOKENS, d)
    h2 = tail(x_prompt, ret_y.reshape(nb, t_len, RET_W), att_o.reshape(nb, t_len, ATT_W), p_mem_k, p_mem_v, TOKEN_TILE)
    y_prompt = peer(h2.reshape(n, d)).reshape(nb, t_len, d)

    ns = db * ds
    pos_s = jnp.tile(past + jnp.arange(ds, dtype=I32), db)
    sq, sk, sv, sg, saq, sak, _, sav, _, siq, sikw, _, _ = _project(
        x_sample.reshape(ns, d), w_in_p, _rope_tables(pos_s), 1, ns)
    per_seq = lambda a: a.reshape(db, ds, a.shape[-1])
    chunk = lambda a: _pad_axis1(per_seq(a), RET_CHUNK).reshape(db * RET_CHUNK, a.shape[-1])
    ret_y_s, s_state = _retention(chunk(sq), chunk(sk), chunk(sv), chunk(sg), state_ret,
                                  _retention_tables(ds), ret_gn_g, ret_gn_b, db, 1)
    ret_y_s = ret_y_s.reshape(db, RET_CHUNK, RET_W)[:, :SAMPLE_TPAD]

    pt_flat = page_table.reshape(-1)
    iq_rows = _pad_axis1(per_seq(siq), QPAD).reshape(db, QPAD * IDX_HEADS, IDX_DIM)
    w_rows = _pad_axis1(per_seq(sikw[:, IDX_DIM:IDX_DIM + IDX_HEADS]), QPAD).reshape(db, QPAD * IDX_HEADS, 1)
    ik_new = _pad_axis1(per_seq(sikw[:, :IDX_DIM]), LANES)
    sel = _dsa_sample_select(pt_flat, iq_rows, w_rows, ik_new, cache_idx_k, db, n_pages, ds,
                             min(IDX_TOPK_MAX, (past + ds) // 4), PAGES_PER_STEP)
    page_rows = lambda a: a.reshape(a.shape[0], PAGE_SIZE * ATT_HEADS, ATT_HD)
    new_rows = lambda a: page_rows(_pad_axis1(per_seq(a), PAGE_SIZE))
    q_rows = _pad_axis1(per_seq(saq).astype(F32), QPAD).reshape(db, QPAD, ATT_HEADS, ATT_HD)
    q_rows = q_rows.transpose(0, 2, 1, 3).reshape(db, ATT_HEADS * QPAD, ATT_HD)
    att_s = _dsa_sample_attend(pt_flat, q_rows, sel, new_rows(sak), new_rows(sav),
                               page_rows(cache_k), page_rows(cache_v), db, n_pages, PAGES_PER_STEP)
    att_s = att_s.reshape(db, ATT_HEADS, QPAD, ATT_HD).transpose(0, 2, 1, 3).reshape(db, QPAD, ATT_W)
    h2_s = tail(_pad_axis1(x_sample, SAMPLE_TPAD), ret_y_s, _pad_axis1(att_s, SAMPLE_TPAD),
                cache_mem_k, cache_mem_v, SAMPLE_TPAD)
    y_sample = peer(h2_s[:, :ds].reshape(ns, d)).reshape(db, ds, d)

    heads = lambda a, b_, t_: a.reshape(b_, t_, ATT_HEADS, ATT_HD)
    mem_heads = lambda a: a.reshape(nb, MEM_TOKENS, MEM_HEADS, MEM_HD)
    return (y_prompt, y_sample, p_state, heads(ak, nb, t_len), heads(av, nb, t_len),
            ikw[:, :IDX_DIM].reshape(nb, t_len, IDX_DIM), mem_heads(p_mem_k), mem_heads(p_mem_v),
            s_state, heads(sak, db, ds), heads(sav, db, ds), sikw[:, :IDX_DIM].reshape(db, ds, IDX_DIM))
```

```python
import functools

import jax
import jax.numpy as jnp
from jax import lax
from jax.experimental import pallas as pl
from jax.experimental.pallas import tpu as pltpu

F32 = jnp.float32
BF16 = jnp.bfloat16
I32 = jnp.int32

D_MODEL = 1024
DEPTH = 1
PAST_LEN = 8192
PAGE_SIZE = 128
RET_HEADS = 4
RET_DK = 128
RET_DV = 128
RET_W = RET_HEADS * RET_DV
RET_CHUNK = 128
ATT_HEADS = 4
ATT_HD = 128
ATT_W = ATT_HEADS * ATT_HD
IDX_HEADS = 8
IDX_DIM = 64
IDX_TOPK_MAX = 256
MEM_TOKENS = 256
MEM_HEADS = 4
MEM_HD = D_MODEL // MEM_HEADS
PEER_HEADS = 8
PEER_NKEYS = 128
PEER_EXPERTS = PEER_NKEYS * PEER_NKEYS
PEER_QDIM = 256
PEER_HALF = PEER_QDIM // 2
PEER_TOPK = 16
ROPE_THETA = 10000.0
LN_EPS = 1e-5
GN_EPS = 1e-5
DN_ALPHA = (2 * DEPTH) ** 0.25

LANES = 128
SUBLANES = 8
GROUP_W = 512
N_GROUPS = 8
IN_COLS_PAD = N_GROUPS * GROUP_W + LANES
VMEM_LIMIT = 56 * 1024 * 1024
INT_MIN = -(2 ** 31)
NEG_INF = float("-inf")

_NT = (((1,), (1,)), ((), ()))


def _cparams(sem):
    return pltpu.CompilerParams(dimension_semantics=sem, vmem_limit_bytes=VMEM_LIMIT)


def _rope_tables(pos):
    def tab(d):
        half = d // 2
        inv = ROPE_THETA ** (-jnp.arange(half, dtype=F32) * (2.0 / d))
        ang = pos.astype(F32)[:, None] * inv[None, :]
        cos, sin = jnp.cos(ang), jnp.sin(ang)
        reps = LANES // d
        return (jnp.tile(jnp.concatenate([cos, cos], axis=1), (1, reps)),
                jnp.tile(jnp.concatenate([-sin, sin], axis=1), (1, reps)))
    c128, s128 = tab(ATT_HD)
    c64, s64 = tab(IDX_DIM)
    return c128, s128, c64, s64


def _proj_kernel(x_ref, w_ref, wvt_ref, wwt_ref, c128_ref, s128_ref, c64_ref, s64_ref,
                 rq_ref, rk_ref, rv_ref, rg_ref, aq_ref, ak_ref, akb_ref, av_ref, avt_ref,
                 iq_ref, ikw_ref, ik2_ref, iwt_ref):
    xb = x_ref[...].astype(BF16)
    c128 = c128_ref[...]
    s128 = s128_ref[...]
    c64 = c64_ref[...]
    s64 = s64_ref[...]
    tm = xb.shape[0]
    lane = lax.broadcasted_iota(I32, (tm, LANES), 1)
    first_half64 = (lane & (IDX_DIM // 2)) == 0

    def group(g, width=GROUP_W):
        return jnp.dot(xb, w_ref[:, g * GROUP_W:g * GROUP_W + width], preferred_element_type=F32)

    def rope128(p):
        return p * c128 + pltpu.roll(p, ATT_HD // 2, 1) * s128

    def rope64(p):
        rot = jnp.where(first_half64, pltpu.roll(p, LANES - IDX_DIM // 2, 1), pltpu.roll(p, IDX_DIM // 2, 1))
        return p * c64 + rot * s64

    p = group(0)
    for h in range(RET_HEADS):
        sl = slice(h * LANES, (h + 1) * LANES)
        rq_ref[:, sl] = rope128(p[:, sl]).astype(BF16)
    p = group(1)
    for h in range(RET_HEADS):
        sl = slice(h * LANES, (h + 1) * LANES)
        rk_ref[:, sl] = (rope128(p[:, sl]) * (RET_DK ** -0.5)).astype(BF16)
    rv_ref[...] = group(2).astype(BF16)
    rg_ref[...] = group(3)
    p = group(4)
    for h in range(ATT_HEADS):
        sl = slice(h * LANES, (h + 1) * LANES)
        aq_ref[:, sl] = rope128(p[:, sl]).astype(BF16)
    p = group(5)
    for h in range(ATT_HEADS):
        sl = slice(h * LANES, (h + 1) * LANES)
        r = rope128(p[:, sl])
        ak_ref[:, sl] = r
        akb_ref[:, sl] = r.astype(BF16)
    p = group(6)
    av_ref[...] = p
    avt_ref[0] = lax.dot_general(wvt_ref[...], xb, _NT, preferred_element_type=F32).astype(BF16)
    iwt_ref[...] = lax.dot_general(wwt_ref[...], xb, _NT, preferred_element_type=F32)[:IDX_HEADS]
    p = group(7)
    for j in range(GROUP_W // LANES):
        sl = slice(j * LANES, (j + 1) * LANES)
        iq_ref[:, sl] = rope64(p[:, sl]).astype(BF16)
    p = group(8, LANES)
    is_key = lane < IDX_DIM
    r = jnp.where(is_key, rope64(p), p)
    ikw_ref[...] = r
    ik2_ref[...] = jnp.where(is_key, r, pltpu.roll(r, IDX_DIM, 1)).astype(BF16)


def _project(x2d, w_pad, tables, period_tiles, tm):
    n = x2d.shape[0]
    nt = n // tm
    row = lambda i: (i, 0)
    tab = lambda i: (i % period_tiles, 0)
    const = lambda i: (0, 0)
    flat = lambda w, dt: (pl.BlockSpec((tm, w), row), jax.ShapeDtypeStruct((n, w), dt))
    outs = [
        flat(GROUP_W, BF16), flat(GROUP_W, BF16), flat(GROUP_W, BF16), flat(GROUP_W, F32),
        flat(GROUP_W, BF16), flat(GROUP_W, F32), flat(GROUP_W, BF16), flat(GROUP_W, F32),
        (pl.BlockSpec((1, ATT_W, tm), lambda i: (i, 0, 0)), jax.ShapeDtypeStruct((nt, ATT_W, tm), BF16)),
        flat(GROUP_W, BF16), flat(LANES, F32), flat(LANES, BF16),
        (pl.BlockSpec((IDX_HEADS, tm), lambda i: (0, i)), jax.ShapeDtypeStruct((IDX_HEADS, n), F32)),
    ]
    v_cols = slice(6 * GROUP_W, 7 * GROUP_W)
    w_cols = slice(N_GROUPS * GROUP_W + IDX_DIM, N_GROUPS * GROUP_W + IDX_DIM + 2 * IDX_HEADS)
    return pl.pallas_call(
        _proj_kernel,
        grid=(nt,),
        in_specs=[pl.BlockSpec((tm, D_MODEL), row),
                  pl.BlockSpec((D_MODEL, IN_COLS_PAD), const),
                  pl.BlockSpec((GROUP_W, D_MODEL), const), pl.BlockSpec((2 * IDX_HEADS, D_MODEL), const)]
                 + [pl.BlockSpec((tm, LANES), tab)] * 4,
        out_specs=[spec for spec, _ in outs],
        out_shape=[shape for _, shape in outs],
        compiler_params=_cparams(("parallel",)),
        name="proj_rope",
    )(x2d, w_pad, w_pad[:, v_cols].T, w_pad[:, w_cols].T, *tables)


def _pad_w_in(w_in):
    return jnp.pad(w_in.astype(BF16), ((0, 0), (0, IN_COLS_PAD - w_in.shape[1])))


def _retention_log_decay():
    return jnp.log(1.0 - 2.0 ** (-5.0 - jnp.arange(RET_HEADS, dtype=F32)))


def _retention_tables(chunk_len):
    log_g = _retention_log_decay()
    j = jnp.arange(RET_CHUNK, dtype=F32)
    diff = j[:, None] - j[None, :]
    dmat = jnp.where(diff >= 0, jnp.exp(log_g[:, None, None] * jnp.maximum(diff, 0.0)), 0.0)
    q_dec = jnp.exp(log_g[None, :] * (j[:, None] + 1.0))
    k_dec = jnp.exp(log_g[None, :] * (chunk_len - 1.0 - j[:, None]))
    k_dec = jnp.where(j[:, None] < chunk_len, k_dec, 0.0)
    s_dec = jnp.exp(log_g * chunk_len)
    widen = lambda t: jnp.repeat(t, LANES, axis=1)
    return dmat, widen(q_dec), widen(k_dec), s_dec


def _retention_kernel(sdec_ref, q_ref, k_ref, v_ref, g_ref, s0_ref, dmat_ref, qdec_ref, kdec_ref,
                      gng_ref, gnb_ref, y_ref, sout_ref, state_ref):
    c = pl.program_id(1)

    @pl.when(c == 0)
    def _():
        state_ref[...] = s0_ref[0]

    for h in range(RET_HEADS):
        sl = slice(h * LANES, (h + 1) * LANES)
        q = q_ref[:, sl]
        k = k_ref[:, sl]
        v = v_ref[:, sl]
        s_prev = state_ref[h]
        scores = lax.dot_general(q, k, _NT, preferred_element_type=F32) * dmat_ref[h]
        o = jnp.dot(scores.astype(BF16), v, preferred_element_type=F32)
        o = o + jnp.dot(q, s_prev.astype(BF16), preferred_element_type=F32) * qdec_ref[:, sl]
        kd_t = (k.astype(F32) * kdec_ref[:, sl]).T.astype(BF16)
        state_ref[h] = s_prev * sdec_ref[h] + jnp.dot(kd_t, v, preferred_element_type=F32)
        mu = jnp.mean(o, axis=-1, keepdims=True)
        var = jnp.mean(jnp.square(o - mu), axis=-1, keepdims=True)
        n = (o - mu) * lax.rsqrt(var + GN_EPS) * gng_ref[:, sl] + gnb_ref[:, sl]
        g = g_ref[:, sl]
        y_ref[:, sl] = (g * (1.0 / (1.0 + jnp.exp(-g))) * n).astype(BF16)

    @pl.when(c == pl.num_programs(1) - 1)
    def _():
        sout_ref[0] = state_ref[...]


def _retention(q, k, v, g, s0, tables, gn_g, gn_b, nb, nc):
    dmat, qdec, kdec, sdec = tables
    n = q.shape[0]
    row = lambda b, c: (b * nc + c, 0)
    const2 = lambda b, c: (0, 0)
    blk = pl.BlockSpec((RET_CHUNK, RET_W), row)
    return pl.pallas_call(
        _retention_kernel,
        grid=(nb, nc),
        in_specs=[pl.BlockSpec(memory_space=pltpu.SMEM), blk, blk, blk, blk,
                  pl.BlockSpec((1, RET_HEADS, RET_DK, RET_DV), lambda b, c: (b, 0, 0, 0)),
                  pl.BlockSpec((RET_HEADS, RET_CHUNK, RET_CHUNK), lambda b, c: (0, 0, 0)),
                  pl.BlockSpec((RET_CHUNK, RET_W), const2), pl.BlockSpec((RET_CHUNK, RET_W), const2),
                  pl.BlockSpec((1, RET_W), const2), pl.BlockSpec((1, RET_W), const2)],
        out_specs=[blk, pl.BlockSpec((1, RET_HEADS, RET_DK, RET_DV), lambda b, c: (b, 0, 0, 0))],
        out_shape=[jax.ShapeDtypeStruct((n, RET_W), BF16),
                   jax.ShapeDtypeStruct((nb, RET_HEADS, RET_DK, RET_DV), F32)],
        scratch_shapes=[pltpu.VMEM((RET_HEADS, RET_DK, RET_DV), F32)],
        compiler_params=_cparams(("parallel", "arbitrary")),
        name="retention",
    )(sdec, q, k, v, g, s0, dmat, qdec, kdec, gn_g.reshape(1, RET_W), gn_b.reshape(1, RET_W))


def _sort_key(x):
    bits = pltpu.bitcast(x, I32)
    return bits ^ ((bits >> 31) & 0x7FFFFFFF)


def _kth_largest_key(count_ge, shape, topk):
    def bit_body(bi, t):
        cand = t + lax.shift_left(jnp.int32(1), 31 - bi)
        return jnp.where(count_ge(cand) >= topk, cand, t)
    return lax.fori_loop(0, 32, bit_body, jnp.full(shape, INT_MIN, I32))


def _tie_limit(count_eq_below, need, shape, pos_bits):
    def bit_body(bi, p):
        cand = p + lax.shift_left(jnp.int32(1), pos_bits - 1 - bi)
        return jnp.where(count_eq_below(cand) <= need, cand, p)
    return lax.fori_loop(0, pos_bits, bit_body, jnp.zeros(shape, I32))


def _dsa_prompt_kernel(aq_ref, iq_ref, iwt_ref, ik2_ref, ak_ref, avt_ref, o_ref, keys_ref, bias_ref, *,
                       tq, kc, topk, pos_bits):
    i = pl.program_id(1)
    q0 = i * tq
    nk = (q0 + tq + kc - 1) // kc
    groups = kc // SUBLANES
    state = (SUBLANES, tq)
    lane = lax.broadcasted_iota(I32, (tq, LANES), 1)
    key_in_chunk = (lax.broadcasted_iota(I32, (groups, SUBLANES, tq), 0) * SUBLANES
                    + lax.broadcasted_iota(I32, (groups, SUBLANES, tq), 1))
    qpos = q0 + lax.broadcasted_iota(I32, (groups, SUBLANES, tq), 2)
    iq = iq_ref[...]
    iwt = iwt_ref[...]

    def score_chunk(c, carry):
        k0 = pl.multiple_of(c * kc, kc)
        ikc = ik2_ref[pl.ds(k0, kc), :]
        acc = None
        for h in range(IDX_HEADS):
            pair = iq[:, (h // 2) * LANES:(h // 2 + 1) * LANES]
            keep = (lane < IDX_DIM) if h % 2 == 0 else (lane >= IDX_DIM)
            qh = jnp.where(keep, pair, jnp.zeros_like(pair))
            s = lax.dot_general(ikc, qh, _NT, preferred_element_type=F32)
            term = jnp.maximum(s, 0.0) * iwt[h:h + 1, :]
            acc = term if acc is None else acc + term
        key = _sort_key(acc).reshape(groups, SUBLANES, tq)
        keys_ref[c] = jnp.where(k0 + key_in_chunk <= qpos, key, INT_MIN)
        return carry
    lax.fori_loop(0, nk, score_chunk, 0)

    def total(a):
        return jnp.broadcast_to(jnp.sum(a, axis=0, keepdims=True), state)

    def count_ge(cand):
        body = lambda c, a: a + jnp.sum(jnp.where(keys_ref[c] >= cand[None], 1, 0), axis=0)
        return total(lax.fori_loop(0, nk, body, jnp.zeros(state, I32)))

    t = _kth_largest_key(count_ge, state, topk)
    n_ge = count_ge(t)
    need = topk - count_ge(t + 1)

    def count_eq_below(p):
        def body(c, a):
            hit = (keys_ref[c] == t[None]) & (c * kc + key_in_chunk < p[None])
            return a + jnp.sum(jnp.where(hit, 1, 0), axis=0)
        return total(lax.fori_loop(0, nk, body, jnp.zeros(state, I32)))

    has_excess = jnp.max(jnp.where(n_ge > topk, 1, 0)) > 0
    plim = lax.cond(has_excess,
                    lambda: _tie_limit(count_eq_below, need, state, pos_bits),
                    lambda: jnp.full(state, 2 ** pos_bits, I32))

    def bias_chunk(c, carry):
        key = keys_ref[c]
        sel = (key > t[None]) | ((key == t[None]) & (c * kc + key_in_chunk < plim[None]))
        bias_ref[c] = jnp.where(sel & (key != INT_MIN), 0.0, NEG_INF)
        return carry
    lax.fori_loop(0, nk, bias_chunk, 0)

    scale = ATT_HD ** -0.5
    head_cols = [slice(h * LANES, (h + 1) * LANES) for h in range(ATT_HEADS)]
    queries = [aq_ref[:, sl] for sl in head_cols]

    def attend_chunk(c, carry):
        k0 = pl.multiple_of(c * kc, kc)
        bias = bias_ref[c].reshape(kc, tq)
        out = []
        for h, sl in enumerate(head_cols):
            m, l, acc = carry[h]
            s = lax.dot_general(ak_ref[pl.ds(k0, kc), sl], queries[h], _NT, preferred_element_type=F32)
            s = s * scale + bias
            m_new = jnp.maximum(m, jnp.max(s, axis=0, keepdims=True))
            m_safe = jnp.where(m_new == NEG_INF, 0.0, m_new)
            p = jnp.exp(s - m_safe)
            alpha = jnp.exp(m - m_safe)
            l_new = alpha * l + jnp.sum(p, axis=0, keepdims=True)
            acc_new = alpha * acc + jnp.dot(avt_ref[c, sl, :], p.astype(BF16), preferred_element_type=F32)
            out.append((m_new, l_new, acc_new))
        return tuple(out)

    init = tuple((jnp.full((1, tq), NEG_INF, F32), jnp.zeros((1, tq), F32), jnp.zeros((ATT_HD, tq), F32))
                 for _ in head_cols)
    fin = lax.fori_loop(0, nk, attend_chunk, init)
    for (m, l, acc), sl in zip(fin, head_cols):
        o_ref[:, sl] = (acc / l).T.astype(BF16)


def _dsa_prompt(aq, iq, iwt, ik2, akb, avt, nb, t_len, topk, tq=LANES, kc=512):
    n = aq.shape[0]
    nq = t_len // tq
    qrow = lambda b, i: (b * nq + i, 0)
    brow = lambda b, i: (b, 0)
    kern = functools.partial(_dsa_prompt_kernel, tq=tq, kc=kc, topk=topk, pos_bits=(t_len - 1).bit_length() + 1)
    chunked = (t_len // kc, kc // SUBLANES, SUBLANES, tq)
    return pl.pallas_call(
        kern,
        grid=(nb, nq),
        in_specs=[pl.BlockSpec((tq, ATT_W), qrow), pl.BlockSpec((tq, GROUP_W), qrow),
                  pl.BlockSpec((IDX_HEADS, tq), lambda b, i: (0, b * nq + i)),
                  pl.BlockSpec((t_len, LANES), brow), pl.BlockSpec((t_len, ATT_W), brow),
                  pl.BlockSpec((t_len // kc, ATT_W, kc), lambda b, i: (b, 0, 0))],
        out_specs=pl.BlockSpec((tq, ATT_W), qrow),
        out_shape=jax.ShapeDtypeStruct((n, ATT_W), BF16),
        scratch_shapes=[pltpu.VMEM(chunked, I32), pltpu.VMEM(chunked, F32)],
        compiler_params=_cparams(("parallel", "arbitrary")),
        name="dsa_prompt",
    )(aq, iq, iwt, ik2, akb, avt)


QPAD = SUBLANES


def _dsa_sample_select_kernel(pt_ref, iq_ref, w_ref, iknew_ref, expand_ref, *rest, pg, n_pages, n_new, topk,
                              pos_bits):
    idx_refs = rest[:pg]
    sel_ref = rest[pg]
    keys_ref = rest[pg + 1]
    p = pl.program_id(1)
    iq = iq_ref[0]
    w = w_ref[0]
    key_in_block = lax.broadcasted_iota(I32, (QPAD, LANES), 1)
    row = lax.broadcasted_iota(I32, (QPAD, LANES), 0)

    def scores(keys_t):
        s = jnp.dot(iq, keys_t.astype(BF16), preferred_element_type=F32)
        r = jnp.maximum(s, 0.0) * w
        return jnp.sum(r.reshape(QPAD, IDX_HEADS, LANES), axis=1)

    for j in range(pg):
        keys_ref[p * pg + j] = _sort_key(scores(idx_refs[j][0]))

    @pl.when(p == pl.num_programs(1) - 1)
    def _():
        s_new = scores(iknew_ref[0])
        keys_ref[n_pages] = jnp.where((key_in_block <= row) & (key_in_block < n_new), _sort_key(s_new), INT_MIN)
        keys_ref[n_pages + 1] = jnp.full((QPAD, LANES), INT_MIN, I32)
        keys = keys_ref[...]
        pos = lax.broadcasted_iota(I32, keys.shape, 0) * LANES + key_in_block[None]

        def lane_total(a):
            return jnp.broadcast_to(jnp.sum(a, axis=1, keepdims=True), (QPAD, LANES))

        def count_ge(cand):
            return lane_total(jnp.sum(jnp.where(keys >= cand[None], 1, 0), axis=0))

        t = _kth_largest_key(count_ge, (QPAD, LANES), topk)
        need = topk - count_ge(t + 1)

        def count_eq_below(lim):
            return lane_total(jnp.sum(jnp.where((keys == t[None]) & (pos < lim[None]), 1, 0), axis=0))

        has_excess = jnp.max(jnp.where(count_ge(t) > topk, 1, 0)) > 0
        plim = lax.cond(has_excess,
                        lambda: _tie_limit(count_eq_below, need, (QPAD, LANES), pos_bits),
                        lambda: jnp.full((QPAD, LANES), 2 ** pos_bits, I32))

        sel = ((keys > t[None]) | ((keys == t[None]) & (pos < plim[None]))) & (keys != INT_MIN)
        flags = jnp.where(sel, 1.0, 0.0).reshape(keys.shape[0] * QPAD, LANES).astype(BF16)
        out = jnp.dot(flags, expand_ref[...], preferred_element_type=F32)
        sel_ref[0] = out.reshape(keys.shape[0], QPAD, out.shape[1])


def _dsa_sample_select(pt_flat, iq_rows, w_rows, ik_new, cache_idx_k, nb, n_pages, n_new, topk, pg):
    kern = functools.partial(_dsa_sample_select_kernel, pg=pg, n_pages=n_pages, n_new=n_new, topk=topk,
                             pos_bits=((n_pages + 1) * LANES).bit_length())
    per_b = lambda shape: pl.BlockSpec((1,) + shape, lambda b, p, pt: (b,) + (0,) * len(shape))
    keys_minor = lambda a: a.transpose(0, 2, 1)
    page = lambda j: pl.BlockSpec((1, IDX_DIM, PAGE_SIZE), lambda b, p, pt: (pt[b * n_pages + p * pg + j], 0, 0))
    cols = PAGE_SIZE * ATT_HEADS
    expand = (jnp.arange(cols)[None, :] // ATT_HEADS == jnp.arange(PAGE_SIZE)[:, None]).astype(BF16)
    nblk = n_pages + 2
    return pl.pallas_call(
        kern,
        grid_spec=pltpu.PrefetchScalarGridSpec(
            num_scalar_prefetch=1,
            grid=(nb, n_pages // pg),
            in_specs=[per_b((QPAD * IDX_HEADS, IDX_DIM)), per_b((QPAD * IDX_HEADS, 1)), per_b((IDX_DIM, PAGE_SIZE)),
                      pl.BlockSpec((PAGE_SIZE, cols), lambda b, p, pt: (0, 0))]
                     + [page(j) for j in range(pg)],
            out_specs=per_b((nblk, QPAD, cols)),
            scratch_shapes=[pltpu.VMEM((nblk, QPAD, LANES), I32)]),
        out_shape=jax.ShapeDtypeStruct((nb, nblk, QPAD, cols), F32),
        compiler_params=_cparams(("parallel", "arbitrary")),
        name="dsa_sample_select",
    )(pt_flat, iq_rows, w_rows, keys_minor(ik_new), expand, *([keys_minor(cache_idx_k)] * pg))


def _dsa_sample_attend_kernel(pt_ref, q_ref, sel_ref, knew_ref, vnew_ref, *rest, pg, n_pages):
    k_refs = rest[:pg]
    v_refs = rest[pg:2 * pg]
    o_ref = rest[2 * pg]
    m_ref, l_ref, acc_ref = rest[2 * pg + 1:]
    p = pl.program_id(1)
    scale = ATT_HD ** -0.5
    rows = ATT_HEADS * QPAD
    cols = PAGE_SIZE * ATT_HEADS

    @pl.when(p == 0)
    def _():
        m_ref[...] = jnp.full(m_ref.shape, NEG_INF, F32)
        l_ref[...] = jnp.zeros(l_ref.shape, F32)
        acc_ref[...] = jnp.zeros(acc_ref.shape, F32)

    q = q_ref[0].astype(BF16)
    own_head = ((lax.broadcasted_iota(I32, (rows, cols), 1) & (ATT_HEADS - 1))
                == (lax.broadcasted_iota(I32, (rows, cols), 0) >> (QPAD.bit_length() - 1)))

    def update(blocks):
        scores = []
        for flags, kp, _ in blocks:
            s = lax.dot_general(q, kp.astype(BF16), _NT, preferred_element_type=F32) * scale
            mask = own_head & (jnp.concatenate([flags] * ATT_HEADS, axis=0) > 0.5)
            scores.append(jnp.where(mask, s, NEG_INF))
        mc = scores[0]
        for s in scores[1:]:
            mc = jnp.maximum(mc, s)
        m = m_ref[...]
        m_new = jnp.maximum(m, jnp.max(mc, axis=1, keepdims=True))
        m_safe = jnp.where(m_new == NEG_INF, 0.0, m_new)
        m_row = m_safe[:, :1]
        psum = jnp.zeros((rows, cols), F32)
        acc = jnp.exp(m - m_safe) * acc_ref[...]
        for (_, _, vp), s in zip(blocks, scores):
            pr = jnp.exp(s - m_row)
            psum = psum + pr
            acc = acc + jnp.dot(pr.astype(BF16), vp.astype(BF16), preferred_element_type=F32)
        l_ref[...] = jnp.exp(m - m_safe) * l_ref[...] + jnp.sum(psum, axis=1, keepdims=True)
        acc_ref[...] = acc
        m_ref[...] = m_new

    update([(sel_ref[0, p * pg + j], k_refs[j][0], v_refs[j][0]) for j in range(pg)])

    @pl.when(p == pl.num_programs(1) - 1)
    def _():
        update([(sel_ref[0, n_pages], knew_ref[0], vnew_ref[0])])
        o_ref[0] = acc_ref[...] / l_ref[...]


def _dsa_sample_attend(pt_flat, q_rows, sel, k_new, v_new, cache_k, cache_v, nb, n_pages, pg):
    kern = functools.partial(_dsa_sample_attend_kernel, pg=pg, n_pages=n_pages)
    rows = ATT_HEADS * QPAD
    cols = PAGE_SIZE * ATT_HEADS
    per_b = lambda shape: pl.BlockSpec((1,) + shape, lambda b, p, pt: (b,) + (0,) * len(shape))
    page = lambda j: pl.BlockSpec((1, cols, ATT_HD), lambda b, p, pt: (pt[b * n_pages + p * pg + j], 0, 0))
    return pl.pallas_call(
        kern,
        grid_spec=pltpu.PrefetchScalarGridSpec(
            num_scalar_prefetch=1,
            grid=(nb, n_pages // pg),
            in_specs=[per_b((rows, ATT_HD)), per_b((n_pages + 2, QPAD, cols)), per_b((cols, ATT_HD)),
                      per_b((cols, ATT_HD))] + [page(j) for j in range(pg)] * 2,
            out_specs=per_b((rows, ATT_HD)),
            scratch_shapes=[pltpu.VMEM((rows, LANES), F32)] * 3),
        out_shape=jax.ShapeDtypeStruct((nb, rows, ATT_HD), F32),
        compiler_params=_cparams(("parallel", "arbitrary")),
        name="dsa_sample_attend",
    )(pt_flat, q_rows, sel, k_new, v_new, *([cache_k] * pg), *([cache_v] * pg))


def _layer_norm(x, g, b):
    mu = jnp.mean(x, axis=-1, keepdims=True)
    var = jnp.mean(jnp.square(x - mu), axis=-1, keepdims=True)
    return (x - mu) * lax.rsqrt(var + LN_EPS) * g + b


def _matmul_kernel(x_ref, w_ref, o_ref):
    o_ref[...] = jnp.dot(x_ref[...].astype(BF16), w_ref[...], preferred_element_type=F32)


def _matmul(x, w_bf16, tm):
    n, kdim = x.shape
    m = w_bf16.shape[1]
    return pl.pallas_call(
        _matmul_kernel,
        grid=(n // tm,),
        in_specs=[pl.BlockSpec((tm, kdim), lambda i: (i, 0)), pl.BlockSpec((kdim, m), lambda i: (0, 0))],
        out_specs=pl.BlockSpec((tm, m), lambda i: (i, 0)),
        out_shape=jax.ShapeDtypeStruct((n, m), F32),
        compiler_params=_cparams(("parallel",)),
        name="mem_proj",
    )(x, w_bf16)


def _tail1_kernel(x_ref, ry_ref, ao_ref, mk_ref, mv_ref, wout_ref, wcq_ref, wco_ref,
                  ln1g_ref, ln1b_ref, ln2g_ref, ln2b_ref, h2_ref, *, head_rows):
    x = x_ref[0]
    tm = x.shape[0]
    mix = jnp.dot(ry_ref[0].astype(BF16), wout_ref[:RET_W, :], preferred_element_type=F32)
    mix = mix + jnp.dot(ao_ref[0].astype(BF16), wout_ref[RET_W:, :], preferred_element_type=F32)
    h1 = _layer_norm(DN_ALPHA * x + mix, ln1g_ref[...], ln1b_ref[...])
    q = jnp.dot(h1.astype(BF16), wcq_ref[...], preferred_element_type=F32).astype(BF16)
    head_cols = [slice(h * MEM_HD, (h + 1) * MEM_HD) for h in range(MEM_HEADS)]
    if head_rows:
        q_rows = jnp.concatenate([q[:, sl] for sl in head_cols], axis=0)
        s = lax.dot_general(q_rows, mk_ref[0].astype(BF16), _NT, preferred_element_type=F32) * (MEM_HD ** -0.5)
        assert tm & (tm - 1) == 0, "stacked-head rows are split by a shift"
        own_head = ((lax.broadcasted_iota(I32, s.shape, 1) & (MEM_HEADS - 1))
                    == lax.broadcasted_iota(I32, s.shape, 0) >> (tm.bit_length() - 1))
        s = jnp.where(own_head, s, NEG_INF)
        e = jnp.exp(s - jnp.max(s, axis=-1, keepdims=True))
        p = e / jnp.sum(e, axis=-1, keepdims=True)
        o = jnp.dot(p.astype(BF16), mv_ref[0].astype(BF16), preferred_element_type=F32).astype(BF16)
        heads = [o[h * tm:(h + 1) * tm] for h in range(MEM_HEADS)]
    else:
        heads = []
        for sl in head_cols:
            mk = mk_ref[0, :, sl].astype(BF16)
            mv = mv_ref[0, :, sl].astype(BF16)
            s = lax.dot_general(q[:, sl], mk, _NT, preferred_element_type=F32) * (MEM_HD ** -0.5)
            e = jnp.exp(s - jnp.max(s, axis=-1, keepdims=True))
            p = e / jnp.sum(e, axis=-1, keepdims=True)
            heads.append(jnp.dot(p.astype(BF16), mv, preferred_element_type=F32).astype(BF16))
    ca = jnp.concatenate(heads, axis=1)
    h2 = DN_ALPHA * h1 + jnp.dot(ca, wco_ref[...], preferred_element_type=F32)
    h2_ref[0] = _layer_norm(h2, ln2g_ref[...], ln2b_ref[...])


def _tail1(x, ret_y, att_o, mem_k, mem_v, w_out, w_cq, w_co, ln1, ln2, tm):
    g, tg, d = x.shape
    head_rows = mem_k.ndim == 4
    if head_rows:
        mem_k = mem_k.reshape(g, MEM_TOKENS * MEM_HEADS, MEM_HD)
        mem_v = mem_v.reshape(g, MEM_TOKENS * MEM_HEADS, MEM_HD)
    tok = lambda w: pl.BlockSpec((1, tm, w), lambda b, i: (b, i, 0))
    mem = pl.BlockSpec((1,) + mem_k.shape[1:], lambda b, i: (b, 0, 0))
    wsp = lambda r, c: pl.BlockSpec((r, c), lambda b, i: (0, 0))
    vec = wsp(1, d)
    return pl.pallas_call(
        functools.partial(_tail1_kernel, head_rows=head_rows),
        grid=(g, tg // tm),
        in_specs=[tok(d), tok(RET_W), tok(ATT_W), mem, mem, wsp(d, d), wsp(d, d), wsp(d, d), vec, vec, vec, vec],
        out_specs=tok(d),
        out_shape=jax.ShapeDtypeStruct((g, tg, d), F32),
        compiler_params=_cparams(("parallel", "arbitrary")),
        name="tail1",
    )(x, ret_y, att_o, mem_k, mem_v, w_out, w_cq, w_co,
      ln1[0].reshape(1, d), ln1[1].reshape(1, d), ln2[0].reshape(1, d), ln2[1].reshape(1, d))


HALF_EXPERTS = PEER_EXPERTS // 2
HALF_BIT = HALF_EXPERTS.bit_length() - 1
ROW_TILE = (SUBLANES, LANES)
SLOTS = PEER_HEADS * PEER_TOPK
HIGH_HALF = -65536


PACK_PAIRS = 64


def _pack_kernel(lo_ref, hi_ref, o_ref):
    def bf16_bits(x):
        return pltpu.bitcast(x.astype(BF16).astype(F32), I32)

    for g in range(PACK_PAIRS // SUBLANES):
        rows = slice(g * SUBLANES, (g + 1) * SUBLANES)
        word = lax.shift_right_logical(bf16_bits(lo_ref[rows, :]), 16) | bf16_bits(hi_ref[rows, :])
        for s in range(SUBLANES):
            o_ref[pl.ds(g * SUBLANES * SUBLANES + s, SUBLANES, stride=SUBLANES), :] = word[:, s * LANES:(s + 1) * LANES]


def _pack_expert_table(w):
    d = w.shape[1]
    steps = HALF_EXPERTS // PACK_PAIRS
    return pl.pallas_call(
        _pack_kernel,
        grid=(steps,),
        in_specs=[pl.BlockSpec((PACK_PAIRS, d), lambda i: (i, 0)),
                  pl.BlockSpec((PACK_PAIRS, d), lambda i: (i + steps, 0))],
        out_specs=pl.BlockSpec((PACK_PAIRS * SUBLANES, LANES), lambda i: (i, 0)),
        out_shape=jax.ShapeDtypeStruct((HALF_EXPERTS * SUBLANES, LANES), I32),
        compiler_params=_cparams(("parallel",)),
        name="pack_table",
    )(w, w)


def _top_rows(s, k, payload=None, order=None):
    rid = lax.broadcasted_iota(I32, s.shape, 0).astype(F32)
    order = rid if order is None else order
    big = jnp.float32(2.0 ** 30)
    vals, pays = [], []
    for _ in range(k):
        m = jnp.max(s, axis=0, keepdims=True)
        win = jnp.min(jnp.where(s == m, order, big), axis=0, keepdims=True)
        hit = order == win
        vals.append(m)
        pays.append(win if payload is None else jnp.max(jnp.where(hit, payload, -1.0), axis=0, keepdims=True))
        s = jnp.where(hit, NEG_INF, s)
    return jnp.concatenate(vals, axis=0), jnp.concatenate(pays, axis=0)


def _peer_route_kernel(h_ref, wpq_ref, ka_ref, kb_ref, row_ref, shift_ref, gate_ref):
    hb = h_ref[...].astype(BF16)
    tm = hb.shape[0]
    ka = ka_ref[...]
    kb = kb_ref[...]
    a8 = lax.broadcasted_iota(I32, (SUBLANES, tm), 0).astype(F32)
    b16 = lax.broadcasted_iota(I32, (PEER_TOPK, tm), 0).astype(F32)
    for hd in range(PEER_HEADS):
        q = jnp.dot(hb, wpq_ref[:, hd * PEER_QDIM:(hd + 1) * PEER_QDIM], preferred_element_type=F32).astype(BF16)
        s1 = lax.dot_general(ka, q[:, :PEER_HALF], _NT, preferred_element_type=F32)
        s2 = lax.dot_general(kb, q[:, PEER_HALF:], _NT, preferred_element_type=F32)
        v1, i1 = _top_rows(s1, PEER_TOPK)
        v2, i2 = _top_rows(s2, PEER_TOPK)
        cand, cid, fid = [], [], []

        def block(vals, ids, flat, valid):
            cand.append(jnp.where(valid, vals, NEG_INF) if valid is not None else vals)
            cid.append(ids)
            fid.append(flat)

        block(v1[0:1] + v2, i1[0:1] * PEER_NKEYS + i2, b16, None)
        block(v1[1:2] + v2[:8], i1[1:2] * PEER_NKEYS + i2[:8], PEER_TOPK + a8, None)
        block(v1[8:] + v2[0:1], i1[8:] * PEER_NKEYS + i2[0:1], (a8 + 8) * PEER_TOPK, None)
        for b in range(5):
            lim = PEER_TOPK // (b + 1)
            block(v1[:8] + v2[b:b + 1], i1[:8] * PEER_NKEYS + i2[b:b + 1], a8 * PEER_TOPK + b,
                  (a8 >= 2) & (a8 < lim))
        sc, eid = _top_rows(jnp.concatenate(cand, axis=0), PEER_TOPK,
                            payload=jnp.concatenate(cid, axis=0), order=jnp.concatenate(fid, axis=0))
        eid = eid.astype(I32)
        e = jnp.exp(sc - sc[0:1])
        rows = slice(hd * PEER_TOPK, (hd + 1) * PEER_TOPK)
        gate_ref[rows, :] = e / jnp.sum(e, axis=0, keepdims=True)
        row_ref[rows, :] = (eid & (HALF_EXPERTS - 1)) * SUBLANES
        shift_ref[rows, :] = (16 - ((eid >> HALF_BIT) << 4)).astype(F32)


def _peer_route(h2d, w_pq, keys_a, keys_b, tm):
    n, d = h2d.shape
    slot_blk = pl.BlockSpec((SLOTS, tm), lambda i: (0, i))
    return pl.pallas_call(
        _peer_route_kernel,
        grid=(n // tm,),
        in_specs=[pl.BlockSpec((tm, d), lambda i: (i, 0)),
                  pl.BlockSpec((d, PEER_HEADS * PEER_QDIM), lambda i: (0, 0)),
                  pl.BlockSpec((PEER_NKEYS, PEER_HALF), lambda i: (0, 0)),
                  pl.BlockSpec((PEER_NKEYS, PEER_HALF), lambda i: (0, 0))],
        out_specs=[slot_blk, slot_blk, slot_blk],
        out_shape=[jax.ShapeDtypeStruct((SLOTS, n), I32), jax.ShapeDtypeStruct((SLOTS, n), F32),
                   jax.ShapeDtypeStruct((SLOTS, n), F32)],
        compiler_params=_cparams(("parallel",)),
        name="peer_route",
    )(h2d, w_pq, keys_a, keys_b)


def _token_column(ref, n, lane):
    col = jnp.sum(jnp.where(lane == n, ref[...], 0.0), axis=1, keepdims=True)
    return jnp.broadcast_to(col, (SLOTS, LANES))


def _expert_row(tab_ref, row, shift_row):
    word = tab_ref[pl.ds(pl.multiple_of(row, SUBLANES), SUBLANES), :]
    return pltpu.bitcast(lax.shift_left(word, jnp.broadcast_to(shift_row, ROW_TILE)) & jnp.int32(HIGH_HALF), F32)


def _fold_sublanes(tiles, sub):
    step = SUBLANES // 2
    while len(tiles) > 1:
        half = len(tiles) // 2
        low = (sub & step) == 0
        nxt = []
        for i in range(half):
            a, b = tiles[i], tiles[i + half]
            if 2 * step == SUBLANES:
                folded = jnp.where(low, a, b) + pltpu.roll(jnp.where(low, b, a), step, 0)
            else:
                folded = jnp.where(low, a + pltpu.roll(a, SUBLANES - step, 0), b + pltpu.roll(b, step, 0))
            nxt.append(folded)
        tiles = nxt
        step //= 2
    return tiles[0]


TOKENS_PER_STEP = 8


def _peer_hidden_kernel(row_ref, x_ref, shift_ref, gate_ref, tab_ref, coef_ref, shv_ref, part_ref, hid_ref):
    tm = x_ref.shape[0]
    lane = lax.broadcasted_iota(I32, (SLOTS, tm), 1)
    sub = lax.broadcasted_iota(I32, ROW_TILE, 0)

    def shifts(n):
        return _token_column(shift_ref, n, lane).astype(I32)

    def gather(n, buf):
        x = x_ref[n]
        base = n * SLOTS
        for g in range(SLOTS // SUBLANES):
            prods = []
            for r in range(SUBLANES):
                j = g * SUBLANES + r
                prods.append(_expert_row(tab_ref, row_ref[base + j], shv_ref[buf, j:j + 1, :]) * x)
            part_ref[n, g * SUBLANES:(g + 1) * SUBLANES, :] = _fold_sublanes(prods, sub)

    shv_ref[0] = shifts(0)

    def finish(n_read, n_lane):
        total = jnp.sum(part_ref[n_read], axis=1, keepdims=True)
        hid_ref[...] = jnp.where(lane == n_lane, total, hid_ref[...])

    hid_ref[...] = jnp.zeros((SLOTS, tm), F32)
    part_ref[0] = jnp.zeros((SLOTS, LANES), F32)

    def token_group(i, carry):
        first = TOKENS_PER_STEP * i
        finish(jnp.maximum(first - 1, 0), first - 1)
        for k in range(TOKENS_PER_STEP):
            n = first + k
            shv_ref[(k + 1) % 2] = shifts(n + 1)
            gather(n, k % 2)
            if k > 0:
                finish(n - 1, n - 1)
        return carry
    lax.fori_loop(0, tm // TOKENS_PER_STEP, token_group, 0)
    finish(tm - 1, tm - 1)
    hid = hid_ref[...]
    gelu = 0.5 * hid * (1.0 + lax.erf(hid * (2.0 ** -0.5)))
    coef_ref[...] = gate_ref[...] * gelu


def _peer_hidden(rows_flat, x_tiles, shift, gate, table, tm):
    n = x_tiles.shape[0]
    slot_blk = pl.BlockSpec((SLOTS, tm), lambda i: (0, i))
    return pl.pallas_call(
        _peer_hidden_kernel,
        grid=(n // tm,),
        in_specs=[pl.BlockSpec((tm * SLOTS,), lambda i: (i,), memory_space=pltpu.SMEM),
                  pl.BlockSpec((tm, *ROW_TILE), lambda i: (i, 0, 0)),
                  slot_blk, slot_blk,
                  pl.BlockSpec(memory_space=pltpu.VMEM)],
        out_specs=slot_blk,
        out_shape=jax.ShapeDtypeStruct((SLOTS, n), F32),
        scratch_shapes=[pltpu.VMEM((2, SLOTS, LANES), I32), pltpu.VMEM((tm, SLOTS, LANES), F32),
                        pltpu.VMEM((SLOTS, tm), F32)],
        compiler_params=_cparams(("arbitrary",)),
        name="peer_hidden",
    )(rows_flat, x_tiles, shift, gate, table)


N_ACC = 4


def _peer_combine_kernel(row_ref, shift_ref, coef_ref, tab_ref, o_ref, shv_ref, cv_ref):
    tm = o_ref.shape[0]
    lane = lax.broadcasted_iota(I32, (SLOTS, tm), 1)

    def columns(n):
        return _token_column(shift_ref, n, lane).astype(I32), _token_column(coef_ref, n, lane)

    def combine(n, buf):
        base = n * SLOTS
        accs = [None] * N_ACC
        for j in range(SLOTS):
            term = jnp.broadcast_to(cv_ref[buf, j:j + 1, :], ROW_TILE) * _expert_row(
                tab_ref, row_ref[base + j], shv_ref[buf, j:j + 1, :])
            accs[j % N_ACC] = term if accs[j % N_ACC] is None else accs[j % N_ACC] + term
        o_ref[n] = (accs[0] + accs[1]) + (accs[2] + accs[3])

    shv_ref[0], cv_ref[0] = columns(0)

    def token_pair(i, carry):
        for k in range(2):
            n = 2 * i + k
            shv_ref[1 - k], cv_ref[1 - k] = columns(n + 1)
            combine(n, k)
        return carry
    lax.fori_loop(0, tm // 2, token_pair, 0)


def _peer_combine(rows_flat, shift, coef, table, n, tm):
    slot_blk = pl.BlockSpec((SLOTS, tm), lambda i: (0, i))
    return pl.pallas_call(
        _peer_combine_kernel,
        grid=(n // tm,),
        in_specs=[pl.BlockSpec((tm * SLOTS,), lambda i: (i,), memory_space=pltpu.SMEM), slot_blk, slot_blk,
                  pl.BlockSpec(memory_space=pltpu.VMEM)],
        out_specs=pl.BlockSpec((tm, *ROW_TILE), lambda i: (i, 0, 0)),
        out_shape=jax.ShapeDtypeStruct((n, *ROW_TILE), F32),
        scratch_shapes=[pltpu.VMEM((2, SLOTS, LANES), I32), pltpu.VMEM((2, SLOTS, LANES), F32)],
        compiler_params=_cparams(("arbitrary",)),
        name="peer_combine",
    )(rows_flat, shift, coef, table)


def _ln3_kernel(h_ref, p_ref, g_ref, b_ref, y_ref):
    y_ref[...] = _layer_norm(DN_ALPHA * h_ref[...] + p_ref[...], g_ref[...], b_ref[...])


def _ln3(h2d, peer2d, g, b, tm):
    n, d = h2d.shape
    tok = pl.BlockSpec((tm, d), lambda i: (i, 0))
    vec = pl.BlockSpec((1, d), lambda i: (0, 0))
    return pl.pallas_call(
        _ln3_kernel,
        grid=(n // tm,),
        in_specs=[tok, tok, vec, vec],
        out_specs=tok,
        out_shape=jax.ShapeDtypeStruct((n, d), F32),
        compiler_params=_cparams(("parallel",)),
        name="ln3",
    )(h2d, peer2d, g.reshape(1, d), b.reshape(1, d))


def _peer_ffn_ln(h2d, w_pq, keys_a, keys_b, tab_u, tab_v, ln3, tm_route, tm_gather):
    n, d = h2d.shape
    rows, shift, gate = _peer_route(h2d, w_pq, keys_a, keys_b, tm_route)
    rows_flat = rows.T.reshape(-1)
    coef = _peer_hidden(rows_flat, h2d.reshape(n, *ROW_TILE), shift, gate, tab_u, tm_gather)
    out = _peer_combine(rows_flat, shift, coef, tab_v, n, tm_gather)
    return _ln3(h2d, out.reshape(n, d), ln3[0], ln3[1], tm_route)


TOKEN_TILE = 512
QUERY_TILE = 128
ROUTE_TILE = 256
GATHER_TILE = 128
SAMPLE_TPAD = 16
PAGES_PER_STEP = 32


def _pad_axis1(a, rows):
    return jnp.pad(a, ((0, 0), (0, rows - a.shape[1])) + ((0, 0),) * (a.ndim - 2))


def kernel(x_prompt, x_sample, mem_prompt, state_ret, cache_k, cache_v, cache_idx_k, cache_mem_k, cache_mem_v,
           page_table, w_in, ret_gn_g, ret_gn_b, w_out, ln1_g, ln1_b, w_cq, w_ck, w_cv, w_co, ln2_g, ln2_b,
           w_pq, peer_keys_a, peer_keys_b, peer_u, peer_v, ln3_g, ln3_b):
    nb, t_len, d = x_prompt.shape
    db, ds, _ = x_sample.shape
    n_pages = page_table.shape[1]
    past = n_pages * PAGE_SIZE
    bf = lambda w: w.astype(BF16)

    w_in_p = _pad_w_in(w_in)
    w_out_b, w_cq_b, w_co_b, w_pq_b = bf(w_out), bf(w_cq), bf(w_co), bf(w_pq)
    keys_a, keys_b = bf(peer_keys_a), bf(peer_keys_b)
    tab_u, tab_v = _pack_expert_table(peer_u), _pack_expert_table(peer_v)
    ln1, ln2, ln3 = (ln1_g, ln1_b), (ln2_g, ln2_b), (ln3_g, ln3_b)

    def tail(x, ret_y, att_o, mem_k, mem_v, tm):
        h2 = _tail1(x, ret_y, att_o, mem_k, mem_v, w_out_b, w_cq_b, w_co_b, ln1, ln2, tm)
        return h2

    def peer(h2d):
        return _peer_ffn_ln(h2d, w_pq_b, keys_a, keys_b, tab_u, tab_v, ln3, ROUTE_TILE, GATHER_TILE)

    n = nb * t_len
    tabs = _rope_tables(jnp.arange(t_len, dtype=I32))
    rq, rk, rv, rg, aq, ak, akb, av, avt, iq, ikw, ik2, iwt = _project(
        x_prompt.reshape(n, d), w_in_p, tabs, t_len // TOKEN_TILE, TOKEN_TILE)
    ret_y, p_state = _retention(rq, rk, rv, rg, jnp.zeros((nb, RET_HEADS, RET_DK, RET_DV), F32),
                                _retention_tables(RET_CHUNK), ret_gn_g, ret_gn_b, nb, t_len // RET_CHUNK)
    att_o = _dsa_prompt(aq, iq, iwt, ik2, akb, avt, nb, t_len, min(IDX_TOPK_MAX, t_len // 4), tq=QUERY_TILE,
                        kc=TOKEN_TILE)
    mem_kv = _matmul(mem_prompt.reshape(nb * MEM_TOKENS, d), jnp.concatenate([bf(w_ck), bf(w_cv)], axis=1), MEM_TOKENS)
    p_mem_k = mem_kv[:, :d].reshape(nb, MEM_TOKENS, d)
    p_mem_v = mem_kv[:, d:].reshape(nb, MEM_TOKENS, d)
    h2 = tail(x_prompt, ret_y.reshape(nb, t_len, RET_W), att_o.reshape(nb, t_len, ATT_W), p_mem_k, p_mem_v, TOKEN_TILE)
    y_prompt = peer(h2.reshape(n, d)).reshape(nb, t_len, d)

    ns = db * ds
    pos_s = jnp.tile(past + jnp.arange(ds, dtype=I32), db)
    sq, sk, sv, sg, saq, sak, _, sav, _, siq, sikw, _, _ = _project(
        x_sample.reshape(ns, d), w_in_p, _rope_tables(pos_s), 1, ns)
    per_seq = lambda a: a.reshape(db, ds, a.shape[-1])
    chunk = lambda a: _pad_axis1(per_seq(a), RET_CHUNK).reshape(db * RET_CHUNK, a.shape[-1])
    ret_y_s, s_state = _retention(chunk(sq), chunk(sk), chunk(sv), chunk(sg), state_ret,
                                  _retention_tables(ds), ret_gn_g, ret_gn_b, db, 1)
    ret_y_s = ret_y_s.reshape(db, RET_CHUNK, RET_W)[:, :SAMPLE_TPAD]

    pt_flat = page_table.reshape(-1)
    iq_rows = _pad_axis1(per_seq(siq), QPAD).reshape(db, QPAD * IDX_HEADS, IDX_DIM)
    w_rows = _pad_axis1(per_seq(sikw[:, IDX_DIM:IDX_DIM + IDX_HEADS]), QPAD).reshape(db, QPAD * IDX_HEADS, 1)
    ik_new = _pad_axis1(per_seq(sikw[:, :IDX_DIM]), LANES)
    sel = _dsa_sample_select(pt_flat, iq_rows, w_rows, ik_new, cache_idx_k, db, n_pages, ds,
                             min(IDX_TOPK_MAX, (past + ds) // 4), PAGES_PER_STEP)
    page_rows = lambda a: a.reshape(a.shape[0], PAGE_SIZE * ATT_HEADS, ATT_HD)
    new_rows = lambda a: page_rows(_pad_axis1(per_seq(a), PAGE_SIZE))
    q_rows = _pad_axis1(per_seq(saq).astype(F32), QPAD).reshape(db, QPAD, ATT_HEADS, ATT_HD)
    q_rows = q_rows.transpose(0, 2, 1, 3).reshape(db, ATT_HEADS * QPAD, ATT_HD)
    att_s = _dsa_sample_attend(pt_flat, q_rows, sel, new_rows(sak), new_rows(sav),
                               page_rows(cache_k), page_rows(cache_v), db, n_pages, PAGES_PER_STEP)
    att_s = att_s.reshape(db, ATT_HEADS, QPAD, ATT_HD).transpose(0, 2, 1, 3).reshape(db, QPAD, ATT_W)
    h2_s = tail(_pad_axis1(x_sample, SAMPLE_TPAD), ret_y_s, _pad_axis1(att_s, SAMPLE_TPAD),
                cache_mem_k, cache_mem_v, SAMPLE_TPAD)
    y_sample = peer(h2_s[:, :ds].reshape(ns, d)).reshape(db, ds, d)

    heads = lambda a, b_, t_: a.reshape(b_, t_, ATT_HEADS, ATT_HD)
    mem_heads = lambda a: a.reshape(nb, MEM_TOKENS, MEM_HEADS, MEM_HD)
    return (y_prompt, y_sample, p_state, heads(ak, nb, t_len), heads(av, nb, t_len),
            ikw[:, :IDX_DIM].reshape(nb, t_len, IDX_DIM), mem_heads(p_mem_k), mem_heads(p_mem_v),
            s_state, heads(sak, db, ds), heads(sav, db, ds), sikw[:, :IDX_DIM].reshape(db, ds, IDX_DIM))
```

```python
import functools

import jax
import jax.numpy as jnp
from jax import lax
from jax.experimental import pallas as pl
from jax.experimental.pallas import tpu as pltpu

F32 = jnp.float32
BF16 = jnp.bfloat16
I32 = jnp.int32

D_MODEL = 1024
DEPTH = 1
PAST_LEN = 8192
PAGE_SIZE = 128
RET_HEADS = 4
RET_DK = 128
RET_DV = 128
RET_W = RET_HEADS * RET_DV
RET_CHUNK = 128
ATT_HEADS = 4
ATT_HD = 128
ATT_W = ATT_HEADS * ATT_HD
IDX_HEADS = 8
IDX_DIM = 64
IDX_TOPK_MAX = 256
MEM_TOKENS = 256
MEM_HEADS = 4
MEM_HD = D_MODEL // MEM_HEADS
PEER_HEADS = 8
PEER_NKEYS = 128
PEER_EXPERTS = PEER_NKEYS * PEER_NKEYS
PEER_QDIM = 256
PEER_HALF = PEER_QDIM // 2
PEER_TOPK = 16
ROPE_THETA = 10000.0
LN_EPS = 1e-5
GN_EPS = 1e-5
DN_ALPHA = (2 * DEPTH) ** 0.25

LANES = 128
SUBLANES = 8
GROUP_W = 512
N_GROUPS = 8
IN_COLS_PAD = N_GROUPS * GROUP_W + LANES
VMEM_LIMIT = 56 * 1024 * 1024
INT_MIN = -(2 ** 31)
NEG_INF = float("-inf")

_NT = (((1,), (1,)), ((), ()))


def _cparams(sem):
    return pltpu.CompilerParams(dimension_semantics=sem, vmem_limit_bytes=VMEM_LIMIT)


def _rope_tables(pos):
    def tab(d):
        half = d // 2
        inv = ROPE_THETA ** (-jnp.arange(half, dtype=F32) * (2.0 / d))
        ang = pos.astype(F32)[:, None] * inv[None, :]
        cos, sin = jnp.cos(ang), jnp.sin(ang)
        reps = LANES // d
        return (jnp.tile(jnp.concatenate([cos, cos], axis=1), (1, reps)),
                jnp.tile(jnp.concatenate([-sin, sin], axis=1), (1, reps)))
    c128, s128 = tab(ATT_HD)
    c64, s64 = tab(IDX_DIM)
    return c128, s128, c64, s64


def _proj_kernel(x_ref, w_ref, wvt_ref, wwt_ref, c128_ref, s128_ref, c64_ref, s64_ref,
                 rq_ref, rk_ref, rv_ref, rg_ref, aq_ref, ak_ref, akb_ref, av_ref, avt_ref,
                 iq_ref, ikw_ref, ik2_ref, iwt_ref):
    xb = x_ref[...].astype(BF16)
    c128 = c128_ref[...]
    s128 = s128_ref[...]
    c64 = c64_ref[...]
    s64 = s64_ref[...]
    tm = xb.shape[0]
    lane = lax.broadcasted_iota(I32, (tm, LANES), 1)
    first_half64 = (lane & (IDX_DIM // 2)) == 0

    def group(g, width=GROUP_W):
        return jnp.dot(xb, w_ref[:, g * GROUP_W:g * GROUP_W + width], preferred_element_type=F32)

    def rope128(p):
        return p * c128 + pltpu.roll(p, ATT_HD // 2, 1) * s128

    def rope64(p):
        rot = jnp.where(first_half64, pltpu.roll(p, LANES - IDX_DIM // 2, 1), pltpu.roll(p, IDX_DIM // 2, 1))
        return p * c64 + rot * s64

    p = group(0)
    for h in range(RET_HEADS):
        sl = slice(h * LANES, (h + 1) * LANES)
        rq_ref[:, sl] = rope128(p[:, sl]).astype(BF16)
    p = group(1)
    for h in range(RET_HEADS):
        sl = slice(h * LANES, (h + 1) * LANES)
        rk_ref[:, sl] = (rope128(p[:, sl]) * (RET_DK ** -0.5)).astype(BF16)
    rv_ref[...] = group(2).astype(BF16)
    rg_ref[...] = group(3)
    p = group(4)
    for h in range(ATT_HEADS):
        sl = slice(h * LANES, (h + 1) * LANES)
        aq_ref[:, sl] = rope128(p[:, sl]).astype(BF16)
    p = group(5)
    for h in range(ATT_HEADS):
        sl = slice(h * LANES, (h + 1) * LANES)
        r = rope128(p[:, sl])
        ak_ref[:, sl] = r
        akb_ref[:, sl] = r.astype(BF16)
    p = group(6)
    av_ref[...] = p
    avt_ref[0] = lax.dot_general(wvt_ref[...], xb, _NT, preferred_element_type=F32).astype(BF16)
    iwt_ref[...] = lax.dot_general(wwt_ref[...], xb, _NT, preferred_element_type=F32)[:IDX_HEADS]
    p = group(7)
    for j in range(GROUP_W // LANES):
        sl = slice(j * LANES, (j + 1) * LANES)
        iq_ref[:, sl] = rope64(p[:, sl]).astype(BF16)
    p = group(8, LANES)
    is_key = lane < IDX_DIM
    r = jnp.where(is_key, rope64(p), p)
    ikw_ref[...] = r
    ik2_ref[...] = jnp.where(is_key, r, pltpu.roll(r, IDX_DIM, 1)).astype(BF16)


def _project(x2d, w_pad, tables, period_tiles, tm):
    n = x2d.shape[0]
    nt = n // tm
    row = lambda i: (i, 0)
    tab = lambda i: (i % period_tiles, 0)
    const = lambda i: (0, 0)
    flat = lambda w, dt: (pl.BlockSpec((tm, w), row), jax.ShapeDtypeStruct((n, w), dt))
    outs = [
        flat(GROUP_W, BF16), flat(GROUP_W, BF16), flat(GROUP_W, BF16), flat(GROUP_W, F32),
        flat(GROUP_W, BF16), flat(GROUP_W, F32), flat(GROUP_W, BF16), flat(GROUP_W, F32),
        (pl.BlockSpec((1, ATT_W, tm), lambda i: (i, 0, 0)), jax.ShapeDtypeStruct((nt, ATT_W, tm), BF16)),
        flat(GROUP_W, BF16), flat(LANES, F32), flat(LANES, BF16),
        (pl.BlockSpec((IDX_HEADS, tm), lambda i: (0, i)), jax.ShapeDtypeStruct((IDX_HEADS, n), F32)),
    ]
    v_cols = slice(6 * GROUP_W, 7 * GROUP_W)
    w_cols = slice(N_GROUPS * GROUP_W + IDX_DIM, N_GROUPS * GROUP_W + IDX_DIM + 2 * IDX_HEADS)
    return pl.pallas_call(
        _proj_kernel,
        grid=(nt,),
        in_specs=[pl.BlockSpec((tm, D_MODEL), row),
                  pl.BlockSpec((D_MODEL, IN_COLS_PAD), const),
                  pl.BlockSpec((GROUP_W, D_MODEL), const), pl.BlockSpec((2 * IDX_HEADS, D_MODEL), const)]
                 + [pl.BlockSpec((tm, LANES), tab)] * 4,
        out_specs=[spec for spec, _ in outs],
        out_shape=[shape for _, shape in outs],
        compiler_params=_cparams(("parallel",)),
        name="proj_rope",
    )(x2d, w_pad, w_pad[:, v_cols].T, w_pad[:, w_cols].T, *tables)


def _pad_w_in(w_in):
    return jnp.pad(w_in.astype(BF16), ((0, 0), (0, IN_COLS_PAD - w_in.shape[1])))


def _retention_log_decay():
    return jnp.log(1.0 - 2.0 ** (-5.0 - jnp.arange(RET_HEADS, dtype=F32)))


def _retention_tables(chunk_len):
    log_g = _retention_log_decay()
    j = jnp.arange(RET_CHUNK, dtype=F32)
    diff = j[:, None] - j[None, :]
    dmat = jnp.where(diff >= 0, jnp.exp(log_g[:, None, None] * jnp.maximum(diff, 0.0)), 0.0)
    q_dec = jnp.exp(log_g[None, :] * (j[:, None] + 1.0))
    k_dec = jnp.exp(log_g[None, :] * (chunk_len - 1.0 - j[:, None]))
    k_dec = jnp.where(j[:, None] < chunk_len, k_dec, 0.0)
    s_dec = jnp.exp(log_g * chunk_len)
    widen = lambda t: jnp.repeat(t, LANES, axis=1)
    return dmat, widen(q_dec), widen(k_dec), s_dec


def _retention_kernel(sdec_ref, q_ref, k_ref, v_ref, g_ref, s0_ref, dmat_ref, qdec_ref, kdec_ref,
                      gng_ref, gnb_ref, y_ref, sout_ref, state_ref):
    c = pl.program_id(1)

    @pl.when(c == 0)
    def _():
        state_ref[...] = s0_ref[0]

    for h in range(RET_HEADS):
        sl = slice(h * LANES, (h + 1) * LANES)
        q = q_ref[:, sl]
        k = k_ref[:, sl]
        v = v_ref[:, sl]
        s_prev = state_ref[h]
        scores = lax.dot_general(q, k, _NT, preferred_element_type=F32) * dmat_ref[h]
        o = jnp.dot(scores.astype(BF16), v, preferred_element_type=F32)
        o = o + jnp.dot(q, s_prev.astype(BF16), preferred_element_type=F32) * qdec_ref[:, sl]
        kd_t = (k.astype(F32) * kdec_ref[:, sl]).T.astype(BF16)
        state_ref[h] = s_prev * sdec_ref[h] + jnp.dot(kd_t, v, preferred_element_type=F32)
        mu = jnp.mean(o, axis=-1, keepdims=True)
        var = jnp.mean(jnp.square(o - mu), axis=-1, keepdims=True)
        n = (o - mu) * lax.rsqrt(var + GN_EPS) * gng_ref[:, sl] + gnb_ref[:, sl]
        g = g_ref[:, sl]
        y_ref[:, sl] = (g * (1.0 / (1.0 + jnp.exp(-g))) * n).astype(BF16)

    @pl.when(c == pl.num_programs(1) - 1)
    def _():
        sout_ref[0] = state_ref[...]


def _retention(q, k, v, g, s0, tables, gn_g, gn_b, nb, nc):
    dmat, qdec, kdec, sdec = tables
    n = q.shape[0]
    row = lambda b, c: (b * nc + c, 0)
    const2 = lambda b, c: (0, 0)
    blk = pl.BlockSpec((RET_CHUNK, RET_W), row)
    return pl.pallas_call(
        _retention_kernel,
        grid=(nb, nc),
        in_specs=[pl.BlockSpec(memory_space=pltpu.SMEM), blk, blk, blk, blk,
                  pl.BlockSpec((1, RET_HEADS, RET_DK, RET_DV), lambda b, c: (b, 0, 0, 0)),
                  pl.BlockSpec((RET_HEADS, RET_CHUNK, RET_CHUNK), lambda b, c: (0, 0, 0)),
                  pl.BlockSpec((RET_CHUNK, RET_W), const2), pl.BlockSpec((RET_CHUNK, RET_W), const2),
                  pl.BlockSpec((1, RET_W), const2), pl.BlockSpec((1, RET_W), const2)],
        out_specs=[blk, pl.BlockSpec((1, RET_HEADS, RET_DK, RET_DV), lambda b, c: (b, 0, 0, 0))],
        out_shape=[jax.ShapeDtypeStruct((n, RET_W), BF16),
                   jax.ShapeDtypeStruct((nb, RET_HEADS, RET_DK, RET_DV), F32)],
        scratch_shapes=[pltpu.VMEM((RET_HEADS, RET_DK, RET_DV), F32)],
        compiler_params=_cparams(("parallel", "arbitrary")),
        name="retention",
    )(sdec, q, k, v, g, s0, dmat, qdec, kdec, gn_g.reshape(1, RET_W), gn_b.reshape(1, RET_W))


def _sort_key(x):
    bits = pltpu.bitcast(x, I32)
    return bits ^ ((bits >> 31) & 0x7FFFFFFF)


def _kth_largest_key(count_ge, shape, topk):
    def bit_body(bi, t):
        cand = t + lax.shift_left(jnp.int32(1), 31 - bi)
        return jnp.where(count_ge(cand) >= topk, cand, t)
    return lax.fori_loop(0, 32, bit_body, jnp.full(shape, INT_MIN, I32))


def _tie_limit(count_eq_below, need, shape, pos_bits):
    def bit_body(bi, p):
        cand = p + lax.shift_left(jnp.int32(1), pos_bits - 1 - bi)
        return jnp.where(count_eq_below(cand) <= need, cand, p)
    return lax.fori_loop(0, pos_bits, bit_body, jnp.zeros(shape, I32))


def _dsa_prompt_kernel(aq_ref, iq_ref, iwt_ref, ik2_ref, ak_ref, avt_ref, o_ref, keys_ref, *,
                       tq, kc, topk, pos_bits):
    i = pl.program_id(1)
    q0 = i * tq
    nk = (q0 + tq + kc - 1) // kc
    groups = kc // SUBLANES
    state = (SUBLANES, tq)
    lane = lax.broadcasted_iota(I32, (tq, LANES), 1)
    key_in_chunk = (lax.broadcasted_iota(I32, (groups, SUBLANES, tq), 0) * SUBLANES
                    + lax.broadcasted_iota(I32, (groups, SUBLANES, tq), 1))
    qpos = q0 + lax.broadcasted_iota(I32, (groups, SUBLANES, tq), 2)
    iq = iq_ref[...]
    iwt = iwt_ref[...]

    def score_chunk(c, carry):
        k0 = pl.multiple_of(c * kc, kc)
        ikc = ik2_ref[pl.ds(k0, kc), :]
        acc = None
        for h in range(IDX_HEADS):
            pair = iq[:, (h // 2) * LANES:(h // 2 + 1) * LANES]
            keep = (lane < IDX_DIM) if h % 2 == 0 else (lane >= IDX_DIM)
            qh = jnp.where(keep, pair, jnp.zeros_like(pair))
            s = lax.dot_general(ikc, qh, _NT, preferred_element_type=F32)
            term = jnp.maximum(s, 0.0) * iwt[h:h + 1, :]
            acc = term if acc is None else acc + term
        key = _sort_key(acc).reshape(groups, SUBLANES, tq)
        keys_ref[c] = jnp.where(k0 + key_in_chunk <= qpos, key, INT_MIN)
        return carry
    lax.fori_loop(0, nk, score_chunk, 0)

    def total(a):
        return jnp.broadcast_to(jnp.sum(a, axis=0, keepdims=True), state)

    def count_ge(cand):
        body = lambda c, a: a + jnp.sum(jnp.where(keys_ref[c] >= cand[None], 1, 0), axis=0)
        return total(lax.fori_loop(0, nk, body, jnp.zeros(state, I32)))

    t = _kth_largest_key(count_ge, state, topk)
    n_ge = count_ge(t)
    need = topk - count_ge(t + 1)

    def count_eq_below(p):
        def body(c, a):
            hit = (keys_ref[c] == t[None]) & (c * kc + key_in_chunk < p[None])
            return a + jnp.sum(jnp.where(hit, 1, 0), axis=0)
        return total(lax.fori_loop(0, nk, body, jnp.zeros(state, I32)))

    has_excess = jnp.max(jnp.where(n_ge > topk, 1, 0)) > 0
    plim = lax.cond(has_excess,
                    lambda: _tie_limit(count_eq_below, need, state, pos_bits),
                    lambda: jnp.full(state, 2 ** pos_bits, I32))

    scale = ATT_HD ** -0.5
    head_cols = [slice(h * LANES, (h + 1) * LANES) for h in range(ATT_HEADS)]
    queries = [aq_ref[:, sl] for sl in head_cols]

    def attend_chunk(c, carry):
        k0 = pl.multiple_of(c * kc, kc)
        key = keys_ref[c]
        sel = (key > t[None]) | ((key == t[None]) & (c * kc + key_in_chunk < plim[None]))
        bias = jnp.where(sel & (key != INT_MIN), 0.0, NEG_INF).reshape(kc, tq)
        out = []
        for h, sl in enumerate(head_cols):
            m, l, acc = carry[h]
            s = lax.dot_general(ak_ref[pl.ds(k0, kc), sl], queries[h], _NT, preferred_element_type=F32)
            s = s * scale + bias
            m_new = jnp.maximum(m, jnp.max(s, axis=0, keepdims=True))
            m_safe = jnp.where(m_new == NEG_INF, 0.0, m_new)
            p = jnp.exp(s - m_safe)
            alpha = jnp.exp(m - m_safe)
            l_new = alpha * l + jnp.sum(p, axis=0, keepdims=True)
            acc_new = alpha * acc + jnp.dot(avt_ref[c, sl, :], p.astype(BF16), preferred_element_type=F32)
            out.append((m_new, l_new, acc_new))
        return tuple(out)

    init = tuple((jnp.full((1, tq), NEG_INF, F32), jnp.zeros((1, tq), F32), jnp.zeros((ATT_HD, tq), F32))
                 for _ in head_cols)
    fin = lax.fori_loop(0, nk, attend_chunk, init)
    for (m, l, acc), sl in zip(fin, head_cols):
        o_ref[:, sl] = (acc / l).T.astype(BF16)


def _dsa_prompt(aq, iq, iwt, ik2, akb, avt, nb, t_len, topk, tq=LANES, kc=512):
    n = aq.shape[0]
    nq = t_len // tq
    qrow = lambda b, i: (b * nq + i, 0)
    brow = lambda b, i: (b, 0)
    kern = functools.partial(_dsa_prompt_kernel, tq=tq, kc=kc, topk=topk, pos_bits=(t_len - 1).bit_length() + 1)
    chunked = (t_len // kc, kc // SUBLANES, SUBLANES, tq)
    return pl.pallas_call(
        kern,
        grid=(nb, nq),
        in_specs=[pl.BlockSpec((tq, ATT_W), qrow), pl.BlockSpec((tq, GROUP_W), qrow),
                  pl.BlockSpec((IDX_HEADS, tq), lambda b, i: (0, b * nq + i)),
                  pl.BlockSpec((t_len, LANES), brow), pl.BlockSpec((t_len, ATT_W), brow),
                  pl.BlockSpec((t_len // kc, ATT_W, kc), lambda b, i: (b, 0, 0))],
        out_specs=pl.BlockSpec((tq, ATT_W), qrow),
        out_shape=jax.ShapeDtypeStruct((n, ATT_W), BF16),
        scratch_shapes=[pltpu.VMEM(chunked, I32)],
        compiler_params=_cparams(("parallel", "arbitrary")),
        name="dsa_prompt",
    )(aq, iq, iwt, ik2, akb, avt)


QPAD = SUBLANES


def _dsa_sample_select_kernel(pt_ref, iq_ref, w_ref, iknew_ref, expand_ref, *rest, pg, n_pages, n_new, topk,
                              pos_bits):
    idx_refs = rest[:pg]
    sel_ref = rest[pg]
    keys_ref = rest[pg + 1]
    p = pl.program_id(1)
    iq = iq_ref[0]
    w = w_ref[0]
    key_in_block = lax.broadcasted_iota(I32, (QPAD, LANES), 1)
    row = lax.broadcasted_iota(I32, (QPAD, LANES), 0)

    def scores(keys_t):
        s = jnp.dot(iq, keys_t.astype(BF16), preferred_element_type=F32)
        r = jnp.maximum(s, 0.0) * w
        return jnp.sum(r.reshape(QPAD, IDX_HEADS, LANES), axis=1)

    for j in range(pg):
        keys_ref[p * pg + j] = _sort_key(scores(idx_refs[j][0]))

    @pl.when(p == pl.num_programs(1) - 1)
    def _():
        s_new = scores(iknew_ref[0])
        keys_ref[n_pages] = jnp.where((key_in_block <= row) & (key_in_block < n_new), _sort_key(s_new), INT_MIN)
        keys_ref[n_pages + 1] = jnp.full((QPAD, LANES), INT_MIN, I32)
        keys = keys_ref[...]
        pos = lax.broadcasted_iota(I32, keys.shape, 0) * LANES + key_in_block[None]

        def lane_total(a):
            return jnp.broadcast_to(jnp.sum(a, axis=1, keepdims=True), (QPAD, LANES))

        def count_ge(cand):
            return lane_total(jnp.sum(jnp.where(keys >= cand[None], 1, 0), axis=0))

        t = _kth_largest_key(count_ge, (QPAD, LANES), topk)
        need = topk - count_ge(t + 1)

        def count_eq_below(lim):
            return lane_total(jnp.sum(jnp.where((keys == t[None]) & (pos < lim[None]), 1, 0), axis=0))

        has_excess = jnp.max(jnp.where(count_ge(t) > topk, 1, 0)) > 0
        plim = lax.cond(has_excess,
                        lambda: _tie_limit(count_eq_below, need, (QPAD, LANES), pos_bits),
                        lambda: jnp.full((QPAD, LANES), 2 ** pos_bits, I32))

        sel = ((keys > t[None]) | ((keys == t[None]) & (pos < plim[None]))) & (keys != INT_MIN)
        flags = jnp.where(sel, 1.0, 0.0).reshape(keys.shape[0] * QPAD, LANES).astype(BF16)
        out = jnp.dot(flags, expand_ref[...], preferred_element_type=F32)
        sel_ref[0] = out.reshape(keys.shape[0], QPAD, out.shape[1])


def _dsa_sample_select(pt_flat, iq_rows, w_rows, ik_new, cache_idx_k, nb, n_pages, n_new, topk, pg):
    kern = functools.partial(_dsa_sample_select_kernel, pg=pg, n_pages=n_pages, n_new=n_new, topk=topk,
                             pos_bits=((n_pages + 1) * LANES).bit_length())
    per_b = lambda shape: pl.BlockSpec((1,) + shape, lambda b, p, pt: (b,) + (0,) * len(shape))
    keys_minor = lambda a: a.transpose(0, 2, 1)
    page = lambda j: pl.BlockSpec((1, IDX_DIM, PAGE_SIZE), lambda b, p, pt: (pt[b * n_pages + p * pg + j], 0, 0))
    cols = PAGE_SIZE * ATT_HEADS
    expand = (jnp.arange(cols)[None, :] // ATT_HEADS == jnp.arange(PAGE_SIZE)[:, None]).astype(BF16)
    nblk = n_pages + 2
    return pl.pallas_call(
        kern,
        grid_spec=pltpu.PrefetchScalarGridSpec(
            num_scalar_prefetch=1,
            grid=(nb, n_pages // pg),
            in_specs=[per_b((QPAD * IDX_HEADS, IDX_DIM)), per_b((QPAD * IDX_HEADS, 1)), per_b((IDX_DIM, PAGE_SIZE)),
                      pl.BlockSpec((PAGE_SIZE, cols), lambda b, p, pt: (0, 0))]
                     + [page(j) for j in range(pg)],
            out_specs=per_b((nblk, QPAD, cols)),
            scratch_shapes=[pltpu.VMEM((nblk, QPAD, LANES), I32)]),
        out_shape=jax.ShapeDtypeStruct((nb, nblk, QPAD, cols), F32),
        compiler_params=_cparams(("parallel", "arbitrary")),
        name="dsa_sample_select",
    )(pt_flat, iq_rows, w_rows, keys_minor(ik_new), expand, *([keys_minor(cache_idx_k)] * pg))


def _dsa_sample_attend_kernel(pt_ref, q_ref, sel_ref, knew_ref, vnew_ref, *rest, pg, n_pages):
    k_refs = rest[:pg]
    v_refs = rest[pg:2 * pg]
    o_ref = rest[2 * pg]
    m_ref, l_ref, acc_ref = rest[2 * pg + 1:]
    p = pl.program_id(1)
    scale = ATT_HD ** -0.5
    rows = ATT_HEADS * QPAD
    cols = PAGE_SIZE * ATT_HEADS

    @pl.when(p == 0)
    def _():
        m_ref[...] = jnp.full(m_ref.shape, NEG_INF, F32)
        l_ref[...] = jnp.zeros(l_ref.shape, F32)
        acc_ref[...] = jnp.zeros(acc_ref.shape, F32)

    q = q_ref[0].astype(BF16)
    own_head = ((lax.broadcasted_iota(I32, (rows, cols), 1) & (ATT_HEADS - 1))
                == (lax.broadcasted_iota(I32, (rows, cols), 0) >> (QPAD.bit_length() - 1)))

    def update(blocks):
        scores = []
        for flags, kp, _ in blocks:
            s = lax.dot_general(q, kp.astype(BF16), _NT, preferred_element_type=F32) * scale
            mask = own_head & (jnp.concatenate([flags] * ATT_HEADS, axis=0) > 0.5)
            scores.append(jnp.where(mask, s, NEG_INF))
        mc = scores[0]
        for s in scores[1:]:
            mc = jnp.maximum(mc, s)
        m = m_ref[...]
        m_new = jnp.maximum(m, jnp.max(mc, axis=1, keepdims=True))
        m_safe = jnp.where(m_new == NEG_INF, 0.0, m_new)
        m_row = m_safe[:, :1]
        psum = jnp.zeros((rows, cols), F32)
        acc = jnp.exp(m - m_safe) * acc_ref[...]
        for (_, _, vp), s in zip(blocks, scores):
            pr = jnp.exp(s - m_row)
            psum = psum + pr
            acc = acc + jnp.dot(pr.astype(BF16), vp.astype(BF16), preferred_element_type=F32)
        l_ref[...] = jnp.exp(m - m_safe) * l_ref[...] + jnp.sum(psum, axis=1, keepdims=True)
        acc_ref[...] = acc
        m_ref[...] = m_new

    update([(sel_ref[0, p * pg + j], k_refs[j][0], v_refs[j][0]) for j in range(pg)])

    @pl.when(p == pl.num_programs(1) - 1)
    def _():
        update([(sel_ref[0, n_pages], knew_ref[0], vnew_ref[0])])
        o_ref[0] = acc_ref[...] / l_ref[...]


def _dsa_sample_attend(pt_flat, q_rows, sel, k_new, v_new, cache_k, cache_v, nb, n_pages, pg):
    kern = functools.partial(_dsa_sample_attend_kernel, pg=pg, n_pages=n_pages)
    rows = ATT_HEADS * QPAD
    cols = PAGE_SIZE * ATT_HEADS
    per_b = lambda shape: pl.BlockSpec((1,) + shape, lambda b, p, pt: (b,) + (0,) * len(shape))
    page = lambda j: pl.BlockSpec((1, cols, ATT_HD), lambda b, p, pt: (pt[b * n_pages + p * pg + j], 0, 0))
    return pl.pallas_call(
        kern,
        grid_spec=pltpu.PrefetchScalarGridSpec(
            num_scalar_prefetch=1,
            grid=(nb, n_pages // pg),
            in_specs=[per_b((rows, ATT_HD)), per_b((n_pages + 2, QPAD, cols)), per_b((cols, ATT_HD)),
                      per_b((cols, ATT_HD))] + [page(j) for j in range(pg)] * 2,
            out_specs=per_b((rows, ATT_HD)),
            scratch_shapes=[pltpu.VMEM((rows, LANES), F32)] * 3),
        out_shape=jax.ShapeDtypeStruct((nb, rows, ATT_HD), F32),
        compiler_params=_cparams(("parallel", "arbitrary")),
        name="dsa_sample_attend",
    )(pt_flat, q_rows, sel, k_new, v_new, *([cache_k] * pg), *([cache_v] * pg))


def _layer_norm(x, g, b):
    mu = jnp.mean(x, axis=-1, keepdims=True)
    var = jnp.mean(jnp.square(x - mu), axis=-1, keepdims=True)
    return (x - mu) * lax.rsqrt(var + LN_EPS) * g + b


def _matmul_kernel(x_ref, w_ref, o_ref):
    o_ref[...] = jnp.dot(x_ref[...].astype(BF16), w_ref[...], preferred_element_type=F32)


def _matmul(x, w_bf16, tm):
    n, kdim = x.shape
    m = w_bf16.shape[1]
    return pl.pallas_call(
        _matmul_kernel,
        grid=(n // tm,),
        in_specs=[pl.BlockSpec((tm, kdim), lambda i: (i, 0)), pl.BlockSpec((kdim, m), lambda i: (0, 0))],
        out_specs=pl.BlockSpec((tm, m), lambda i: (i, 0)),
        out_shape=jax.ShapeDtypeStruct((n, m), F32),
        compiler_params=_cparams(("parallel",)),
        name="mem_proj",
    )(x, w_bf16)


def _tail1_kernel(x_ref, ry_ref, ao_ref, mk_ref, mv_ref, wout_ref, wcq_ref, wco_ref,
                  ln1g_ref, ln1b_ref, ln2g_ref, ln2b_ref, h2_ref, *, head_rows):
    x = x_ref[0]
    tm = x.shape[0]
    mix = jnp.dot(ry_ref[0].astype(BF16), wout_ref[:RET_W, :], preferred_element_type=F32)
    mix = mix + jnp.dot(ao_ref[0].astype(BF16), wout_ref[RET_W:, :], preferred_element_type=F32)
    h1 = _layer_norm(DN_ALPHA * x + mix, ln1g_ref[...], ln1b_ref[...])
    q = jnp.dot(h1.astype(BF16), wcq_ref[...], preferred_element_type=F32).astype(BF16)
    head_cols = [slice(h * MEM_HD, (h + 1) * MEM_HD) for h in range(MEM_HEADS)]
    if head_rows:
        q_rows = jnp.concatenate([q[:, sl] for sl in head_cols], axis=0)
        s = lax.dot_general(q_rows, mk_ref[0].astype(BF16), _NT, preferred_element_type=F32) * (MEM_HD ** -0.5)
        assert tm & (tm - 1) == 0, "stacked-head rows are split by a shift"
        own_head = ((lax.broadcasted_iota(I32, s.shape, 1) & (MEM_HEADS - 1))
                    == lax.broadcasted_iota(I32, s.shape, 0) >> (tm.bit_length() - 1))
        s = jnp.where(own_head, s, NEG_INF)
        e = jnp.exp(s - jnp.max(s, axis=-1, keepdims=True))
        p = e / jnp.sum(e, axis=-1, keepdims=True)
        o = jnp.dot(p.astype(BF16), mv_ref[0].astype(BF16), preferred_element_type=F32).astype(BF16)
        heads = [o[h * tm:(h + 1) * tm] for h in range(MEM_HEADS)]
    else:
        heads = []
        for sl in head_cols:
            mk = mk_ref[0, :, sl].astype(BF16)
            mv = mv_ref[0, :, sl].astype(BF16)
            s = lax.dot_general(q[:, sl], mk, _NT, preferred_element_type=F32) * (MEM_HD ** -0.5)
            e = jnp.exp(s - jnp.max(s, axis=-1, keepdims=True))
            p = e / jnp.sum(e, axis=-1, keepdims=True)
            heads.append(jnp.dot(p.astype(BF16), mv, preferred_element_type=F32).astype(BF16))
    ca = jnp.concatenate(heads, axis=1)
    h2 = DN_ALPHA * h1 + jnp.dot(ca, wco_ref[...], preferred_element_type=F32)
    h2_ref[0] = _layer_norm(h2, ln2g_ref[...], ln2b_ref[...])


def _tail1(x, ret_y, att_o, mem_k, mem_v, w_out, w_cq, w_co, ln1, ln2, tm):
    g, tg, d = x.shape
    head_rows = mem_k.ndim == 4
    if head_rows:
        mem_k = mem_k.reshape(g, MEM_TOKENS * MEM_HEADS, MEM_HD)
        mem_v = mem_v.reshape(g, MEM_TOKENS * MEM_HEADS, MEM_HD)
    tok = lambda w: pl.BlockSpec((1, tm, w), lambda b, i: (b, i, 0))
    mem = pl.BlockSpec((1,) + mem_k.shape[1:], lambda b, i: (b, 0, 0))
    wsp = lambda r, c: pl.BlockSpec((r, c), lambda b, i: (0, 0))
    vec = wsp(1, d)
    return pl.pallas_call(
        functools.partial(_tail1_kernel, head_rows=head_rows),
        grid=(g, tg // tm),
        in_specs=[tok(d), tok(RET_W), tok(ATT_W), mem, mem, wsp(d, d), wsp(d, d), wsp(d, d), vec, vec, vec, vec],
        out_specs=tok(d),
        out_shape=jax.ShapeDtypeStruct((g, tg, d), F32),
        compiler_params=_cparams(("parallel", "arbitrary")),
        name="tail1",
    )(x, ret_y, att_o, mem_k, mem_v, w_out, w_cq, w_co,
      ln1[0].reshape(1, d), ln1[1].reshape(1, d), ln2[0].reshape(1, d), ln2[1].reshape(1, d))


HALF_EXPERTS = PEER_EXPERTS // 2
HALF_BIT = HALF_EXPERTS.bit_length() - 1
ROW_TILE = (SUBLANES, LANES)
SLOTS = PEER_HEADS * PEER_TOPK
HIGH_HALF = -65536


PACK_PAIRS = 64


def _pack_kernel(lo_ref, hi_ref, o_ref):
    def bf16_bits(x):
        return pltpu.bitcast(x.astype(BF16).astype(F32), I32)

    for g in range(PACK_PAIRS // SUBLANES):
        rows = slice(g * SUBLANES, (g + 1) * SUBLANES)
        word = lax.shift_right_logical(bf16_bits(lo_ref[rows, :]), 16) | bf16_bits(hi_ref[rows, :])
        for s in range(SUBLANES):
            o_ref[pl.ds(g * SUBLANES * SUBLANES + s, SUBLANES, stride=SUBLANES), :] = word[:, s * LANES:(s + 1) * LANES]


def _pack_expert_table(w):
    d = w.shape[1]
    steps = HALF_EXPERTS // PACK_PAIRS
    return pl.pallas_call(
        _pack_kernel,
        grid=(steps,),
        in_specs=[pl.BlockSpec((PACK_PAIRS, d), lambda i: (i, 0)),
                  pl.BlockSpec((PACK_PAIRS, d), lambda i: (i + steps, 0))],
        out_specs=pl.BlockSpec((PACK_PAIRS * SUBLANES, LANES), lambda i: (i, 0)),
        out_shape=jax.ShapeDtypeStruct((HALF_EXPERTS * SUBLANES, LANES), I32),
        compiler_params=_cparams(("parallel",)),
        name="pack_table",
    )(w, w)


def _top_rows(s, k, payload=None, order=None):
    rid = lax.broadcasted_iota(I32, s.shape, 0).astype(F32)
    order = rid if order is None else order
    big = jnp.float32(2.0 ** 30)
    vals, pays = [], []
    for _ in range(k):
        m = jnp.max(s, axis=0, keepdims=True)
        win = jnp.min(jnp.where(s == m, order, big), axis=0, keepdims=True)
        hit = order == win
        vals.append(m)
        pays.append(win if payload is None else jnp.max(jnp.where(hit, payload, -1.0), axis=0, keepdims=True))
        s = jnp.where(hit, NEG_INF, s)
    return jnp.concatenate(vals, axis=0), jnp.concatenate(pays, axis=0)


def _peer_route_kernel(h_ref, wpq_ref, ka_ref, kb_ref, row_ref, shift_ref, gate_ref):
    hb = h_ref[...].astype(BF16)
    tm = hb.shape[0]
    ka = ka_ref[...]
    kb = kb_ref[...]
    a8 = lax.broadcasted_iota(I32, (SUBLANES, tm), 0).astype(F32)
    b16 = lax.broadcasted_iota(I32, (PEER_TOPK, tm), 0).astype(F32)
    for hd in range(PEER_HEADS):
        q = jnp.dot(hb, wpq_ref[:, hd * PEER_QDIM:(hd + 1) * PEER_QDIM], preferred_element_type=F32).astype(BF16)
        s1 = lax.dot_general(ka, q[:, :PEER_HALF], _NT, preferred_element_type=F32)
        s2 = lax.dot_general(kb, q[:, PEER_HALF:], _NT, preferred_element_type=F32)
        v1, i1 = _top_rows(s1, PEER_TOPK)
        v2, i2 = _top_rows(s2, PEER_TOPK)
        cand, cid, fid = [], [], []

        def block(vals, ids, flat, valid):
            cand.append(jnp.where(valid, vals, NEG_INF) if valid is not None else vals)
            cid.append(ids)
            fid.append(flat)

        block(v1[0:1] + v2, i1[0:1] * PEER_NKEYS + i2, b16, None)
        block(v1[1:2] + v2[:8], i1[1:2] * PEER_NKEYS + i2[:8], PEER_TOPK + a8, None)
        block(v1[8:] + v2[0:1], i1[8:] * PEER_NKEYS + i2[0:1], (a8 + 8) * PEER_TOPK, None)
        for b in range(5):
            lim = PEER_TOPK // (b + 1)
            block(v1[:8] + v2[b:b + 1], i1[:8] * PEER_NKEYS + i2[b:b + 1], a8 * PEER_TOPK + b,
                  (a8 >= 2) & (a8 < lim))
        sc, eid = _top_rows(jnp.concatenate(cand, axis=0), PEER_TOPK,
                            payload=jnp.concatenate(cid, axis=0), order=jnp.concatenate(fid, axis=0))
        eid = eid.astype(I32)
        e = jnp.exp(sc - sc[0:1])
        rows = slice(hd * PEER_TOPK, (hd + 1) * PEER_TOPK)
        gate_ref[rows, :] = e / jnp.sum(e, axis=0, keepdims=True)
        row_ref[rows, :] = (eid & (HALF_EXPERTS - 1)) * SUBLANES
        shift_ref[rows, :] = (16 - ((eid >> HALF_BIT) << 4)).astype(F32)


def _peer_route(h2d, w_pq, keys_a, keys_b, tm):
    n, d = h2d.shape
    slot_blk = pl.BlockSpec((SLOTS, tm), lambda i: (0, i))
    return pl.pallas_call(
        _peer_route_kernel,
        grid=(n // tm,),
        in_specs=[pl.BlockSpec((tm, d), lambda i: (i, 0)),
                  pl.BlockSpec((d, PEER_HEADS * PEER_QDIM), lambda i: (0, 0)),
                  pl.BlockSpec((PEER_NKEYS, PEER_HALF), lambda i: (0, 0)),
                  pl.BlockSpec((PEER_NKEYS, PEER_HALF), lambda i: (0, 0))],
        out_specs=[slot_blk, slot_blk, slot_blk],
        out_shape=[jax.ShapeDtypeStruct((SLOTS, n), I32), jax.ShapeDtypeStruct((SLOTS, n), F32),
                   jax.ShapeDtypeStruct((SLOTS, n), F32)],
        compiler_params=_cparams(("parallel",)),
        name="peer_route",
    )(h2d, w_pq, keys_a, keys_b)


def _token_column(ref, n, lane):
    col = jnp.sum(jnp.where(lane == n, ref[...], 0.0), axis=1, keepdims=True)
    return jnp.broadcast_to(col, (SLOTS, LANES))


def _expert_row(tab_ref, row, shift_row):
    word = tab_ref[pl.ds(pl.multiple_of(row, SUBLANES), SUBLANES), :]
    return pltpu.bitcast(lax.shift_left(word, jnp.broadcast_to(shift_row, ROW_TILE)) & jnp.int32(HIGH_HALF), F32)


def _fold_sublanes(tiles, sub):
    step = SUBLANES // 2
    while len(tiles) > 1:
        half = len(tiles) // 2
        low = (sub & step) == 0
        nxt = []
        for i in range(half):
            a, b = tiles[i], tiles[i + half]
            if 2 * step == SUBLANES:
                folded = jnp.where(low, a, b) + pltpu.roll(jnp.where(low, b, a), step, 0)
            else:
                folded = jnp.where(low, a + pltpu.roll(a, SUBLANES - step, 0), b + pltpu.roll(b, step, 0))
            nxt.append(folded)
        tiles = nxt
        step //= 2
    return tiles[0]


TOKENS_PER_STEP = 8


def _peer_hidden_kernel(row_ref, x_ref, shift_ref, gate_ref, tab_ref, coef_ref, shv_ref, part_ref, hid_ref):
    tm = x_ref.shape[0]
    lane = lax.broadcasted_iota(I32, (SLOTS, tm), 1)
    sub = lax.broadcasted_iota(I32, ROW_TILE, 0)

    def shifts(n):
        return _token_column(shift_ref, n, lane).astype(I32)

    def gather(n, buf):
        x = x_ref[n]
        base = n * SLOTS
        for g in range(SLOTS // SUBLANES):
            prods = []
            for r in range(SUBLANES):
                j = g * SUBLANES + r
                prods.append(_expert_row(tab_ref, row_ref[base + j], shv_ref[buf, j:j + 1, :]) * x)
            part_ref[n, g * SUBLANES:(g + 1) * SUBLANES, :] = _fold_sublanes(prods, sub)

    shv_ref[0] = shifts(0)

    def finish(n_read, n_lane):
        total = jnp.sum(part_ref[n_read], axis=1, keepdims=True)
        hid_ref[...] = jnp.where(lane == n_lane, total, hid_ref[...])

    hid_ref[...] = jnp.zeros((SLOTS, tm), F32)
    part_ref[0] = jnp.zeros((SLOTS, LANES), F32)

    def token_group(i, carry):
        first = TOKENS_PER_STEP * i
        finish(jnp.maximum(first - 1, 0), first - 1)
        for k in range(TOKENS_PER_STEP):
            n = first + k
            shv_ref[(k + 1) % 2] = shifts(n + 1)
            gather(n, k % 2)
            if k > 0:
                finish(n - 1, n - 1)
        return carry
    lax.fori_loop(0, tm // TOKENS_PER_STEP, token_group, 0)
    finish(tm - 1, tm - 1)
    hid = hid_ref[...]
    gelu = 0.5 * hid * (1.0 + lax.erf(hid * (2.0 ** -0.5)))
    coef_ref[...] = gate_ref[...] * gelu


def _peer_hidden(rows_flat, x_tiles, shift, gate, table, tm):
    n = x_tiles.shape[0]
    slot_blk = pl.BlockSpec((SLOTS, tm), lambda i: (0, i))
    return pl.pallas_call(
        _peer_hidden_kernel,
        grid=(n // tm,),
        in_specs=[pl.BlockSpec((tm * SLOTS,), lambda i: (i,), memory_space=pltpu.SMEM),
                  pl.BlockSpec((tm, *ROW_TILE), lambda i: (i, 0, 0)),
                  slot_blk, slot_blk,
                  pl.BlockSpec(memory_space=pltpu.VMEM)],
        out_specs=slot_blk,
        out_shape=jax.ShapeDtypeStruct((SLOTS, n), F32),
        scratch_shapes=[pltpu.VMEM((2, SLOTS, LANES), I32), pltpu.VMEM((tm, SLOTS, LANES), F32),
                        pltpu.VMEM((SLOTS, tm), F32)],
        compiler_params=_cparams(("arbitrary",)),
        name="peer_hidden",
    )(rows_flat, x_tiles, shift, gate, table)


N_ACC = 4


def _peer_combine_kernel(row_ref, shift_ref, coef_ref, tab_ref, o_ref, shv_ref, cv_ref):
    tm = o_ref.shape[0]
    lane = lax.broadcasted_iota(I32, (SLOTS, tm), 1)

    def columns(n):
        return _token_column(shift_ref, n, lane).astype(I32), _token_column(coef_ref, n, lane)

    def combine(n, buf):
        base = n * SLOTS
        accs = [None] * N_ACC
        for j in range(SLOTS):
            term = jnp.broadcast_to(cv_ref[buf, j:j + 1, :], ROW_TILE) * _expert_row(
                tab_ref, row_ref[base + j], shv_ref[buf, j:j + 1, :])
            accs[j % N_ACC] = term if accs[j % N_ACC] is None else accs[j % N_ACC] + term
        o_ref[n] = (accs[0] + accs[1]) + (accs[2] + accs[3])

    shv_ref[0], cv_ref[0] = columns(0)

    def token_pair(i, carry):
        for k in range(2):
            n = 2 * i + k
            shv_ref[1 - k], cv_ref[1 - k] = columns(n + 1)
            combine(n, k)
        return carry
    lax.fori_loop(0, tm // 2, token_pair, 0)


def _peer_combine(rows_flat, shift, coef, table, n, tm):
    slot_blk = pl.BlockSpec((SLOTS, tm), lambda i: (0, i))
    return pl.pallas_call(
        _peer_combine_kernel,
        grid=(n // tm,),
        in_specs=[pl.BlockSpec((tm * SLOTS,), lambda i: (i,), memory_space=pltpu.SMEM), slot_blk, slot_blk,
                  pl.BlockSpec(memory_space=pltpu.VMEM)],
        out_specs=pl.BlockSpec((tm, *ROW_TILE), lambda i: (i, 0, 0)),
        out_shape=jax.ShapeDtypeStruct((n, *ROW_TILE), F32),
        scratch_shapes=[pltpu.VMEM((2, SLOTS, LANES), I32), pltpu.VMEM((2, SLOTS, LANES), F32)],
        compiler_params=_cparams(("arbitrary",)),
        name="peer_combine",
    )(rows_flat, shift, coef, table)


def _ln3_kernel(h_ref, p_ref, g_ref, b_ref, y_ref):
    y_ref[...] = _layer_norm(DN_ALPHA * h_ref[...] + p_ref[...], g_ref[...], b_ref[...])


def _ln3(h2d, peer2d, g, b, tm):
    n, d = h2d.shape
    tok = pl.BlockSpec((tm, d), lambda i: (i, 0))
    vec = pl.BlockSpec((1, d), lambda i: (0, 0))
    return pl.pallas_call(
        _ln3_kernel,
        grid=(n // tm,),
        in_specs=[tok, tok, vec, vec],
        out_specs=tok,
        out_shape=jax.ShapeDtypeStruct((n, d), F32),
        compiler_params=_cparams(("parallel",)),
        name="ln3",
    )(h2d, peer2d, g.reshape(1, d), b.reshape(1, d))


def _peer_ffn_ln(h2d, w_pq, keys_a, keys_b, tab_u, tab_v, ln3, tm_route, tm_gather):
    n, d = h2d.shape
    rows, shift, gate = _peer_route(h2d, w_pq, keys_a, keys_b, tm_route)
    rows_flat = rows.T.reshape(-1)
    coef = _peer_hidden(rows_flat, h2d.reshape(n, *ROW_TILE), shift, gate, tab_u, tm_gather)
    out = _peer_combine(rows_flat, shift, coef, tab_v, n, tm_gather)
    return _ln3(h2d, out.reshape(n, d), ln3[0], ln3[1], tm_route)


TOKEN_TILE = 512
QUERY_TILE = 128
ROUTE_TILE = 256
GATHER_TILE = 128
SAMPLE_TPAD = 16
PAGES_PER_STEP = 32


def _pad_axis1(a, rows):
    return jnp.pad(a, ((0, 0), (0, rows - a.shape[1])) + ((0, 0),) * (a.ndim - 2))


def kernel(x_prompt, x_sample, mem_prompt, state_ret, cache_k, cache_v, cache_idx_k, cache_mem_k, cache_mem_v,
           page_table, w_in, ret_gn_g, ret_gn_b, w_out, ln1_g, ln1_b, w_cq, w_ck, w_cv, w_co, ln2_g, ln2_b,
           w_pq, peer_keys_a, peer_keys_b, peer_u, peer_v, ln3_g, ln3_b):
    nb, t_len, d = x_prompt.shape
    db, ds, _ = x_sample.shape
    n_pages = page_table.shape[1]
    past = n_pages * PAGE_SIZE
    bf = lambda w: w.astype(BF16)

    w_in_p = _pad_w_in(w_in)
    w_out_b, w_cq_b, w_co_b, w_pq_b = bf(w_out), bf(w_cq), bf(w_co), bf(w_pq)
    keys_a, keys_b = bf(peer_keys_a), bf(peer_keys_b)
    tab_u, tab_v = _pack_expert_table(peer_u), _pack_expert_table(peer_v)
    ln1, ln2, ln3 = (ln1_g, ln1_b), (ln2_g, ln2_b), (ln3_g, ln3_b)

    def tail(x, ret_y, att_o, mem_k, mem_v, tm):
        h2 = _tail1(x, ret_y, att_o, mem_k, mem_v, w_out_b, w_cq_b, w_co_b, ln1, ln2, tm)
        return h2

    def peer(h2d):
        return _peer_ffn_ln(h2d, w_pq_b, keys_a, keys_b, tab_u, tab_v, ln3, ROUTE_TILE, GATHER_TILE)

    n = nb * t_len
    tabs = _rope_tables(jnp.arange(t_len, dtype=I32))
    rq, rk, rv, rg, aq, ak, akb, av, avt, iq, ikw, ik2, iwt = _project(
        x_prompt.reshape(n, d), w_in_p, tabs, t_len // TOKEN_TILE, TOKEN_TILE)
    ret_y, p_state = _retention(rq, rk, rv, rg, jnp.zeros((nb, RET_HEADS, RET_DK, RET_DV), F32),
                                _retention_tables(RET_CHUNK), ret_gn_g, ret_gn_b, nb, t_len // RET_CHUNK)
    att_o = _dsa_prompt(aq, iq, iwt, ik2, akb, avt, nb, t_len, min(IDX_TOPK_MAX, t_len // 4), tq=QUERY_TILE,
                        kc=TOKEN_TILE)
    mem_kv = _matmul(mem_prompt.reshape(nb * MEM_TOKENS, d), jnp.concatenate([bf(w_ck), bf(w_cv)], axis=1), MEM_TOKENS)
    p_mem_k = mem_kv[:, :d].reshape(nb, MEM_TOKENS, d)
    p_mem_v = mem_kv[:, d:].reshape(nb, MEM_TOKENS, d)
    h2 = tail(x_prompt, ret_y.reshape(nb, t_len, RET_W), att_o.reshape(nb, t_len, ATT_W), p_mem_k, p_mem_v, TOKEN_TILE)
    y_prompt = peer(h2.reshape(n, d)).reshape(nb, t_len, d)

    ns = db * ds
    pos_s = jnp.tile(past + jnp.arange(ds, dtype=I32), db)
    sq, sk, sv, sg, saq, sak, _, sav, _, siq, sikw, _, _ = _project(
        x_sample.reshape(ns, d), w_in_p, _rope_tables(pos_s), 1, ns)
    per_seq = lambda a: a.reshape(db, ds, a.shape[-1])
    chunk = lambda a: _pad_axis1(per_seq(a), RET_CHUNK).reshape(db * RET_CHUNK, a.shape[-1])
    ret_y_s, s_state = _retention(chunk(sq), chunk(sk), chunk(sv), chunk(sg), state_ret,
                                  _retention_tables(ds), ret_gn_g, ret_gn_b, db, 1)
    ret_y_s = ret_y_s.reshape(db, RET_CHUNK, RET_W)[:, :SAMPLE_TPAD]

    pt_flat = page_table.reshape(-1)
    iq_rows = _pad_axis1(per_seq(siq), QPAD).reshape(db, QPAD * IDX_HEADS, IDX_DIM)
    w_rows = _pad_axis1(per_seq(sikw[:, IDX_DIM:IDX_DIM + IDX_HEADS]), QPAD).reshape(db, QPAD * IDX_HEADS, 1)
    ik_new = _pad_axis1(per_seq(sikw[:, :IDX_DIM]), LANES)
    sel = _dsa_sample_select(pt_flat, iq_rows, w_rows, ik_new, cache_idx_k, db, n_pages, ds,
                             min(IDX_TOPK_MAX, (past + ds) // 4), PAGES_PER_STEP)
    page_rows = lambda a: a.reshape(a.shape[0], PAGE_SIZE * ATT_HEADS, ATT_HD)
    new_rows = lambda a: page_rows(_pad_axis1(per_seq(a), PAGE_SIZE))
    q_rows = _pad_axis1(per_seq(saq).astype(F32), QPAD).reshape(db, QPAD, ATT_HEADS, ATT_HD)
    q_rows = q_rows.transpose(0, 2, 1, 3).reshape(db, ATT_HEADS * QPAD, ATT_HD)
    att_s = _dsa_sample_attend(pt_flat, q_rows, sel, new_rows(sak), new_rows(sav),
                               page_rows(cache_k), page_rows(cache_v), db, n_pages, PAGES_PER_STEP)
    att_s = att_s.reshape(db, ATT_HEADS, QPAD, ATT_HD).transpose(0, 2, 1, 3).reshape(db, QPAD, ATT_W)
    h2_s = tail(_pad_axis1(x_sample, SAMPLE_TPAD), ret_y_s, _pad_axis1(att_s, SAMPLE_TPAD),
                cache_mem_k, cache_mem_v, SAMPLE_TPAD)
    y_sample = peer(h2_s[:, :ds].reshape(ns, d)).reshape(db, ds, d)

    heads = lambda a, b_, t_: a.reshape(b_, t_, ATT_HEADS, ATT_HD)
    mem_heads = lambda a: a.reshape(nb, MEM_TOKENS, MEM_HEADS, MEM_HD)
    return (y_prompt, y_sample, p_state, heads(ak, nb, t_len), heads(av, nb, t_len),
            ikw[:, :IDX_DIM].reshape(nb, t_len, IDX_DIM), mem_heads(p_mem_k), mem_heads(p_mem_v),
            s_state, heads(sak, db, ds), heads(sav, db, ds), sikw[:, :IDX_DIM].reshape(db, ds, IDX_DIM))
```
